```python
import math
import jax, jax.numpy as jnp
from jax import lax
import numpy as np

D_MODEL = 1024
BATCH = 4
SEQ = 4096
DEPTH = 1

D_MIX = D_MODEL
D_ATTN = D_MIX // 2
D_CONV = D_MIX - D_ATTN
HEAD_DIM = 64
N_HEADS = D_ATTN // HEAD_DIM
N_KV_HEADS = 2
GQA_GROUP = N_HEADS // N_KV_HEADS
WINDOW = 128
BLOCK = 128
NUM_BUCKETS = 32
MAX_DISTANCE = 128
CONV_WIDTH = 31
CONV_GROUPS = 8
D_Q = N_HEADS * HEAD_DIM
D_KV = N_KV_HEADS * HEAD_DIM
D_IN = D_Q + 2 * D_KV + D_ATTN + 2 * D_CONV + D_CONV
EPS = 1e-6
NEG_INF = -1e30

kernel_name = "hymba_conformer_swa_sink_block"


def rmsnorm(x, w, eps=EPS):
    xf = x.astype(jnp.float32)
    y = xf * lax.rsqrt(jnp.mean(xf * xf, axis=-1, keepdims=True) + eps)
    return (y * w.astype(jnp.float32)).astype(x.dtype)


def layernorm(x, w, b, eps=1e-5):
    xf = x.astype(jnp.float32)
    mu = jnp.mean(xf, axis=-1, keepdims=True)
    var = jnp.mean(jnp.square(xf - mu), axis=-1, keepdims=True)
    y = (xf - mu) * lax.rsqrt(var + eps)
    return (y * w.astype(jnp.float32) + b.astype(jnp.float32)).astype(x.dtype)


def t5_causal_bucket(dist):
    n = jnp.maximum(dist, 0)
    max_exact = NUM_BUCKETS // 2
    nf = jnp.maximum(n, 1).astype(jnp.float32)
    large = max_exact + (jnp.log(nf / max_exact) / math.log(MAX_DISTANCE / max_exact)
                         * (NUM_BUCKETS - max_exact)).astype(jnp.int32)
    large = jnp.minimum(large, NUM_BUCKETS - 1)
    return jnp.where(n < max_exact, n, large)


def sliding_window_attention(q, k, v, rel_bias, sinks):
    B, S = q.shape[0], q.shape[1]
    nb = S // BLOCK
    scale = HEAD_DIM ** -0.5
    qb = q.reshape(B, nb, BLOCK, N_KV_HEADS, GQA_GROUP, HEAD_DIM)
    pad = ((0, 0), (BLOCK, 0), (0, 0), (0, 0))
    kp = jnp.pad(k, pad).reshape(B, nb + 1, BLOCK, N_KV_HEADS, HEAD_DIM)
    vp = jnp.pad(v, pad).reshape(B, nb + 1, BLOCK, N_KV_HEADS, HEAD_DIM)
    kwin = jnp.concatenate([kp[:, :-1], kp[:, 1:]], axis=2)
    vwin = jnp.concatenate([vp[:, :-1], vp[:, 1:]], axis=2)

    logits = jnp.einsum('bnqkgd,bnskd->bnkgqs', qb, kwin,
                        preferred_element_type=jnp.float32) * scale
    qi = jnp.arange(BLOCK)[:, None]
    sj = jnp.arange(2 * BLOCK)[None, :]
    dist = qi + BLOCK - sj
    band = (dist >= 0) & (dist < WINDOW)
    bias = rel_bias.astype(jnp.float32)[t5_causal_bucket(dist)]
    bias = bias.transpose(2, 0, 1).reshape(N_KV_HEADS, GQA_GROUP, BLOCK, 2 * BLOCK)
    valid_key = (jnp.arange(nb)[:, None] > 0) | (sj >= BLOCK)
    mask = band[None, :, :] & valid_key[:, None, :]
    logits = jnp.where(mask[None, :, None, None, :, :], logits + bias[None, None], NEG_INF)

    sink = sinks.astype(jnp.float32).reshape(1, 1, N_KV_HEADS, GQA_GROUP, 1, 1)
    m = jnp.maximum(jnp.max(logits, axis=-1, keepdims=True), sink)
    p = jnp.exp(logits - m)
    denom = jnp.sum(p, axis=-1, keepdims=True) + jnp.exp(sink - m)
    probs = (p / denom).astype(v.dtype)
    out = jnp.einsum('bnkgqs,bnskd->bnqkgd', probs, vwin)
    return out.reshape(B, S, D_ATTN)


def conformer_conv(u_glu, dw_w, dw_b, ln_w, ln_b):
    a, g = jnp.split(u_glu, 2, axis=-1)
    u = a * jax.nn.sigmoid(g)
    y = lax.conv_general_dilated(
        u, dw_w[:, None, :].astype(u.dtype), window_strides=(1,),
        padding=[(CONV_WIDTH - 1, 0)], dimension_numbers=('NWC', 'WIO', 'NWC'),
        feature_group_count=D_CONV) + dw_b
    y = layernorm(y, ln_w, ln_b)
    return jax.nn.silu(y)


def hybrid_layer(x, norm_w, w_in, q_norm_w, k_norm_w, sinks, dw_w, dw_b,
                 ln_w, ln_b, w_out, rel_bias):
    B, S, _ = x.shape
    h = rmsnorm(x, norm_w)
    proj = h @ w_in
    splits = np.cumsum([D_Q, D_KV, D_KV, D_ATTN, 2 * D_CONV])
    q, k, v, z_attn, u_glu, z_conv = jnp.split(proj, splits, axis=-1)

    q = rmsnorm(q.reshape(B, S, N_KV_HEADS, GQA_GROUP, HEAD_DIM), q_norm_w)
    k = rmsnorm(k.reshape(B, S, N_KV_HEADS, HEAD_DIM), k_norm_w)
    v = v.reshape(B, S, N_KV_HEADS, HEAD_DIM)
    y_attn = sliding_window_attention(q, k, v, rel_bias, sinks) * jax.nn.silu(z_attn)

    y_conv = conformer_conv(u_glu, dw_w, dw_b, ln_w, ln_b) * jax.nn.silu(z_conv)

    y = jnp.concatenate([y_attn, y_conv], axis=-1) @ w_out
    return x + y


def setup_inputs(seed: int = 0) -> dict:
    key = jax.random.key(seed)
    ks = jax.random.split(key, 12)
    L = DEPTH
    f = jnp.float32
    return {
        "x": jax.random.normal(ks[0], (BATCH, SEQ, D_MODEL), f),
        "norm_w": 1.0 + 0.05 * jax.random.normal(ks[1], (L, D_MODEL), f),
        "w_in": jax.random.normal(ks[2], (L, D_MODEL, D_IN), f) * D_MODEL ** -0.5,
        "q_norm_w": 1.0 + 0.05 * jax.random.normal(ks[3], (L, HEAD_DIM), f),
        "k_norm_w": 1.0 + 0.05 * jax.random.normal(ks[4], (L, HEAD_DIM), f),
        "sinks": 0.5 * jax.random.normal(ks[5], (L, N_HEADS), f),
        "dw_w": jax.random.normal(ks[6], (L, CONV_WIDTH, D_CONV), f) * CONV_WIDTH ** -0.5,
        "dw_b": 0.02 * jax.random.normal(ks[7], (L, D_CONV), f),
        "ln_w": 1.0 + 0.05 * jax.random.normal(ks[8], (L, D_CONV), f),
        "ln_b": 0.02 * jax.random.normal(ks[9], (L, D_CONV), f),
        "w_out": jax.random.normal(ks[10], (L, D_MIX, D_MODEL), f) * D_MIX ** -0.5,
        "rel_bias": 0.5 * jax.random.normal(ks[11], (NUM_BUCKETS, N_HEADS), f),
    }


def reference(x, norm_w, w_in, q_norm_w, k_norm_w, sinks, dw_w, dw_b,
              ln_w, ln_b, w_out, rel_bias):
    for l in range(DEPTH):
        x = hybrid_layer(x, norm_w[l], w_in[l], q_norm_w[l], k_norm_w[l], sinks[l],
                         dw_w[l], dw_b[l], ln_w[l], ln_b[l], w_out[l], rel_bias)
    return x
```

```python
import functools
import math

import numpy as np
import jax
import jax.numpy as jnp
from jax import lax
from jax.experimental import pallas as pl
from jax.experimental.pallas import tpu as pltpu

D_MODEL = 1024
D_ATTN = 512
D_CONV = 512
HEAD_DIM = 64
N_HEADS = 8
N_KV_HEADS = 2
BLOCK = 128
NUM_BUCKETS = 32
MAX_DISTANCE = 128
CONV_WIDTH = 31
D_IN = 2816
EPS = 1e-6
LN_EPS = 1e-5
NEG_INF = -1e30

SEQ_TILE = 512
KV_HALO = BLOCK
U_HALO = 32
U_OFF = U_HALO - (CONV_WIDTH - 1)
ROW_CHUNK = 64
VMEM_LIMIT_BYTES = 56 * 1024 * 1024

Q0, K0, V0, ZA0, UA0, UG0, ZC0 = 0, 512, 640, 768, 1280, 1792, 2304

_F32 = jnp.float32
_BF16 = jnp.bfloat16


def _t5_bucket_table():
    qi = np.arange(BLOCK)[:, None]
    sj = np.arange(2 * BLOCK)[None, :]
    dist = qi + BLOCK - sj
    n = np.maximum(dist, 0)
    max_exact = NUM_BUCKETS // 2
    nf = np.maximum(n, 1).astype(np.float32)
    large = max_exact + (np.log(nf / max_exact) / math.log(MAX_DISTANCE / max_exact)
                         * (NUM_BUCKETS - max_exact)).astype(np.int32)
    large = np.minimum(large, NUM_BUCKETS - 1)
    bucket = np.where(n < max_exact, n, large).astype(np.int32)
    band = (dist >= 0) & (dist < BLOCK)
    return bucket, band, (sj >= BLOCK)


def _sigmoid(z):
    return jax.nn.sigmoid(z)


def _layer_kernel(sink_ref, x_ref, nw_ref, win_ref, qw_ref, kw_ref, bd_ref, dww_ref,
                  dwb_ref, lnw_ref, lnb_ref, wout_ref, bias_ref,
                  o_ref,
                  h_s, q_s, k_s, v_s, u_s, c_s, ga_s, gc_s, y_s):
    ts = SEQ_TILE
    i = pl.program_id(1)

    @pl.when(i == 0)
    def _():
        k_s[:, 0:KV_HALO, :] = jnp.zeros((4, KV_HALO, 128), _BF16)
        v_s[:, 0:KV_HALO, :] = jnp.zeros((4, KV_HALO, 256), _BF16)
        u_s[0:U_HALO, :] = jnp.zeros((U_HALO, D_CONV), _F32)

    @pl.when(i > 0)
    def _():
        k_s[:, 0:KV_HALO, :] = k_s[:, ts:ts + KV_HALO, :]
        v_s[:, 0:KV_HALO, :] = v_s[:, ts:ts + KV_HALO, :]
        u_s[0:U_HALO, :] = u_s[ts:ts + U_HALO, :]

    for r in range(0, ts, ROW_CHUNK):
        xc = x_ref[r:r + ROW_CHUNK, :]
        ss = jnp.sum(xc * xc, axis=-1, keepdims=True)
        rs = lax.rsqrt(ss * (1.0 / D_MODEL) + EPS)
        h_s[r:r + ROW_CHUNK, :] = ((xc * rs) * nw_ref[...]).astype(_BF16)

    def proj(c0, width):
        return jnp.dot(h_s[...], win_ref[:, c0:c0 + width], preferred_element_type=_F32)

    bd = bd_ref[...]

    for half in range(2):
        q = proj(Q0 + 256 * half, 256)
        ssq = jnp.dot((q * q).astype(_BF16), bd, preferred_element_type=_F32)
        qn = (q * lax.rsqrt(ssq * (1.0 / HEAD_DIM) + EPS)) * qw_ref[:, 256 * half:256 * half + 256]
        q_s[:, 256 * half:256 * half + 256] = qn.astype(_BF16)

    kv = proj(K0, 256)
    k = kv[:, 0:128]
    v = kv[:, 128:256]
    kss = jnp.dot((k * k).astype(_BF16), bd[0:128, 0:128], preferred_element_type=_F32)
    kn = (k * lax.rsqrt(kss * (1.0 / HEAD_DIM) + EPS)) * kw_ref[...]
    kn_sw = pltpu.roll(kn, 64, axis=1)
    v_sw = pltpu.roll(v, 64, axis=1)
    lane = lax.broadcasted_iota(jnp.int32, (ts, 128), 1)
    lo = lane < HEAD_DIM
    zero = jnp.zeros((ts, 128), _F32)
    one = jnp.ones((ts, 128), _F32)
    cur = slice(KV_HALO, KV_HALO + ts)
    k_s[0, cur, :] = jnp.where(lo, kn, zero).astype(_BF16)
    k_s[1, cur, :] = jnp.where(lo, zero, kn_sw).astype(_BF16)
    k_s[2, cur, :] = jnp.where(lo, kn_sw, zero).astype(_BF16)
    k_s[3, cur, :] = jnp.where(lo, zero, kn).astype(_BF16)
    ones_lo = jnp.where(lo, one, zero).astype(_BF16)
    ones_hi = jnp.where(lo, zero, one).astype(_BF16)
    v_s[0, cur, 0:128] = jnp.where(lo, v, zero).astype(_BF16)
    v_s[1, cur, 0:128] = jnp.where(lo, zero, v_sw).astype(_BF16)
    v_s[2, cur, 0:128] = jnp.where(lo, v_sw, zero).astype(_BF16)
    v_s[3, cur, 0:128] = jnp.where(lo, zero, v).astype(_BF16)
    v_s[0, cur, 128:256] = ones_lo
    v_s[1, cur, 128:256] = ones_hi
    v_s[2, cur, 128:256] = ones_lo
    v_s[3, cur, 128:256] = ones_hi

    for half in range(2):
        cs = slice(256 * half, 256 * half + 256)
        za = proj(ZA0 + 256 * half, 256)
        ga_s[:, cs] = za * _sigmoid(za)
        a = proj(UA0 + 256 * half, 256)
        g = proj(UG0 + 256 * half, 256)
        u_s[U_HALO:U_HALO + ts, cs] = a * _sigmoid(g)
        zc = proj(ZC0 + 256 * half, 256)
        gc_s[:, cs] = zc * _sigmoid(zc)

    for cb in range(D_CONV // 128):
        ls = slice(128 * cb, 128 * cb + 128)
        for r in range(0, ts, ROW_CHUNK):
            acc = jnp.zeros((ROW_CHUNK, 128), _F32)
            for j in range(CONV_WIDTH):
                r0 = r + U_OFF + j
                acc = acc + dww_ref[j:j + 1, ls] * u_s[r0:r0 + ROW_CHUNK, ls]
            c_s[r:r + ROW_CHUNK, ls] = acc + dwb_ref[:, ls]

    for r in range(0, ts, ROW_CHUNK):
        rr = slice(r, r + ROW_CHUNK)
        y = c_s[rr, :]
        mu = jnp.sum(y, axis=-1, keepdims=True) * (1.0 / D_CONV)
        d = y - mu
        var = jnp.sum(d * d, axis=-1, keepdims=True) * (1.0 / D_CONV)
        yn = (d * lax.rsqrt(var + LN_EPS)) * lnw_ref[...] + lnb_ref[...]
        y_s[rr, D_ATTN:D_ATTN + D_CONV] = ((yn * _sigmoid(yn)) * gc_s[rr, :]).astype(_BF16)

    lane_q = lax.broadcasted_iota(jnp.int32, (BLOCK, 128), 1)
    lo_q = lane_q < HEAD_DIM
    for b in range(ts // BLOCK):
        rows = slice(BLOCK * b, BLOCK * b + BLOCK)
        krows = slice(BLOCK * b, BLOCK * b + 2 * BLOCK)
        tbl = jnp.where(i == 0, 1, 0) if b == 0 else 0
        for g in range(N_KV_HEADS):
            c0 = 256 * g
            qg = jnp.concatenate([q_s[rows, c0:c0 + 128], q_s[rows, c0 + 128:c0 + 256]], axis=0)
            acc = None
            extra = []
            for e in range(2):
                s = lax.dot_general(qg, k_s[2 * g + e, krows, :], (((1,), (1,)), ((), ())),
                                    preferred_element_type=_F32)
                s = s + bias_ref[tbl, 2 * g + e]
                ps = []
                ex = []
                for j in range(2):
                    sink = sink_ref[4 * g + 2 * j + e]
                    sj = s[BLOCK * j:BLOCK * j + BLOCK, :]
                    m = jnp.maximum(jnp.max(sj, axis=-1, keepdims=True), sink)
                    ps.append(jnp.exp(sj - m))
                    ex.append(jnp.exp(sink - m))
                p = jnp.concatenate(ps, axis=0).astype(_BF16)
                o = jnp.dot(p, v_s[2 * g + e, krows, :], preferred_element_type=_F32)
                acc = o if acc is None else acc + o
                extra.append(ex)
            for j in range(2):
                oj = acc[BLOCK * j:BLOCK * j + BLOCK, :]
                den = oj[:, 128:256] + jnp.where(lo_q, extra[0][j], extra[1][j])
                cs = slice(c0 + 128 * j, c0 + 128 * j + 128)
                y_s[rows, cs] = ((oj[:, 0:128] / den) * ga_s[rows, cs]).astype(_BF16)

    for n in range(0, D_MODEL, 256):
        o_ref[:, n:n + 256] = x_ref[:, n:n + 256] + jnp.dot(
            y_s[...], wout_ref[:, n:n + 256], preferred_element_type=_F32)


def _bias_tables(rel_bias):
    bucket, band, cur_keys = _t5_bucket_table()
    bias = rel_bias.astype(_F32)[bucket]
    bias = jnp.where(band[:, :, None], bias, NEG_INF)
    first = jnp.where(cur_keys[:, :, None], bias, NEG_INF)
    both = jnp.stack([bias, first], axis=0)
    both = both.transpose(0, 3, 1, 2)
    both = both.reshape(2, N_KV_HEADS, 2, 2, BLOCK, 2 * BLOCK)
    both = both.transpose(0, 1, 3, 2, 4, 5)
    return both.reshape(2, 4, 2 * BLOCK, 2 * BLOCK)


def _layer(x, norm_w, w_in, q_norm_w, k_norm_w, sinks, dw_w, dw_b, ln_w, ln_b, w_out, rel_bias):
    B, S, D = x.shape
    ts = SEQ_TILE
    assert D == D_MODEL and S % ts == 0 and ts % BLOCK == 0

    idx = np.arange(256) // HEAD_DIM
    bd = jnp.asarray((idx[:, None] == idx[None, :]).astype(np.float32), dtype=_BF16)
    qw = (jnp.tile(q_norm_w.astype(_F32), N_HEADS) * (HEAD_DIM ** -0.5)).reshape(1, D_ATTN)
    kw = jnp.tile(k_norm_w.astype(_F32), N_KV_HEADS).reshape(1, 128)
    dww = jnp.concatenate([dw_w.astype(_F32), jnp.zeros((1, D_CONV), _F32)], axis=0)

    const2 = lambda b, i: (0, 0)
    grid = (B, S // ts)
    in_specs = [
        pl.BlockSpec(memory_space=pltpu.SMEM),
        pl.BlockSpec((None, ts, D), lambda b, i: (b, i, 0)),
        pl.BlockSpec((1, D), const2),
        pl.BlockSpec((D, D_IN), const2),
        pl.BlockSpec((1, D_ATTN), const2),
        pl.BlockSpec((1, 128), const2),
        pl.BlockSpec((256, 256), const2),
        pl.BlockSpec((32, D_CONV), const2),
        pl.BlockSpec((1, D_CONV), const2),
        pl.BlockSpec((1, D_CONV), const2),
        pl.BlockSpec((1, D_CONV), const2),
        pl.BlockSpec((D, D), const2),
        pl.BlockSpec((2, 4, 256, 256), lambda b, i: (0, 0, 0, 0)),
    ]
    scratch = [
        pltpu.VMEM((ts, D), _BF16),
        pltpu.VMEM((ts, D_ATTN), _BF16),
        pltpu.VMEM((4, ts + KV_HALO, 128), _BF16),
        pltpu.VMEM((4, ts + KV_HALO, 256), _BF16),
        pltpu.VMEM((ts + U_HALO, D_CONV), _F32),
        pltpu.VMEM((ts, D_CONV), _F32),
        pltpu.VMEM((ts, D_ATTN), _F32),
        pltpu.VMEM((ts, D_CONV), _F32),
        pltpu.VMEM((ts, D), _BF16),
    ]
    return pl.pallas_call(
        _layer_kernel,
        grid=grid,
        in_specs=in_specs,
        out_specs=pl.BlockSpec((None, ts, D), lambda b, i: (b, i, 0)),
        out_shape=jax.ShapeDtypeStruct((B, S, D), x.dtype),
        scratch_shapes=scratch,
        compiler_params=pltpu.CompilerParams(
            dimension_semantics=("arbitrary", "arbitrary"),
            vmem_limit_bytes=VMEM_LIMIT_BYTES),
        name="hybrid_layer",
    )(sinks.astype(_F32), x, norm_w.astype(_F32).reshape(1, D), w_in.astype(_BF16), qw, kw, bd,
      dww, dw_b.astype(_F32).reshape(1, D_CONV), ln_w.astype(_F32).reshape(1, D_CONV),
      ln_b.astype(_F32).reshape(1, D_CONV), w_out.astype(_BF16), _bias_tables(rel_bias))


def kernel(x, norm_w, w_in, q_norm_w, k_norm_w, sinks, dw_w, dw_b, ln_w, ln_b, w_out, rel_bias):
    depth = norm_w.shape[0]
    for l in range(depth):
        x = _layer(x, norm_w[l], w_in[l], q_norm_w[l], k_norm_w[l], sinks[l], dw_w[l], dw_b[l],
                   ln_w[l], ln_b[l], w_out[l], rel_bias)
    return x
```

```python
import functools
import math

import numpy as np
import jax
import jax.numpy as jnp
from jax import lax
from jax.experimental import pallas as pl
from jax.experimental.pallas import tpu as pltpu

D_MODEL = 1024
D_ATTN = 512
D_CONV = 512
HEAD_DIM = 64
N_HEADS = 8
N_KV_HEADS = 2
BLOCK = 128
NUM_BUCKETS = 32
MAX_DISTANCE = 128
CONV_WIDTH = 31
D_IN = 2816
EPS = 1e-6
LN_EPS = 1e-5
NEG_INF = -1e30

SEQ_TILE = 512
KV_HALO = BLOCK
U_HALO = 32
U_OFF = U_HALO - (CONV_WIDTH - 1)
ROW_CHUNK = 64
VMEM_LIMIT_BYTES = 56 * 1024 * 1024

Q0, K0, V0, ZA0, UA0, UG0, ZC0 = 0, 512, 640, 768, 1280, 1792, 2304

_F32 = jnp.float32
_BF16 = jnp.bfloat16


def _t5_bucket_table():
    qi = np.arange(BLOCK)[:, None]
    sj = np.arange(2 * BLOCK)[None, :]
    dist = qi + BLOCK - sj
    n = np.maximum(dist, 0)
    max_exact = NUM_BUCKETS // 2
    nf = np.maximum(n, 1).astype(np.float32)
    large = max_exact + (np.log(nf / max_exact) / math.log(MAX_DISTANCE / max_exact)
                         * (NUM_BUCKETS - max_exact)).astype(np.int32)
    large = np.minimum(large, NUM_BUCKETS - 1)
    bucket = np.where(n < max_exact, n, large).astype(np.int32)
    band = (dist >= 0) & (dist < BLOCK)
    return bucket, band


def _sigmoid(z):
    return jax.nn.sigmoid(z)


def _build_bias_tables(relb_ref, bucket_ref, bias_s):
    cur_keys = lax.broadcasted_iota(jnp.int32, (8, 2 * BLOCK), 1) >= BLOCK

    def body(c, carry):
        r0 = pl.multiple_of(c * 8, 8)
        bk = bucket_ref[pl.ds(r0, 8), :]
        accs = [jnp.full((8, 2 * BLOCK), NEG_INF, _F32) for _ in range(N_HEADS)]
        for b in range(NUM_BUCKETS):
            hit = bk == b
            for h in range(N_HEADS):
                accs[h] = jnp.where(hit, relb_ref[b * N_HEADS + h], accs[h])
        for h in range(N_HEADS):
            g, rem = divmod(h, 4)
            j, e = divmod(rem, 2)
            rows = pl.ds(BLOCK * j + r0, 8)
            bias_s[0, 2 * g + e, rows, :] = accs[h]
            bias_s[1, 2 * g + e, rows, :] = jnp.where(cur_keys, accs[h], NEG_INF)
        return carry

    lax.fori_loop(0, BLOCK // 8, body, 0)


def _layer_kernel(sink_ref, relb_ref, x_ref, nw_ref, win_ref, qw_ref, kw_ref, bd_ref, dww_ref,
                  dwb_ref, lnw_ref, lnb_ref, wout_ref, bucket_ref,
                  o_ref,
                  h_s, q_s, k_s, v_s, u_s, c_s, ga_s, gc_s, y_s, bias_s):
    ts = SEQ_TILE
    i = pl.program_id(1)

    @pl.when((pl.program_id(0) == 0) & (i == 0))
    def _():
        _build_bias_tables(relb_ref, bucket_ref, bias_s)

    @pl.when(i == 0)
    def _():
        k_s[:, 0:KV_HALO, :] = jnp.zeros((4, KV_HALO, 128), _BF16)
        v_s[:, 0:KV_HALO, :] = jnp.zeros((4, KV_HALO, 256), _BF16)
        u_s[0:U_HALO, :] = jnp.zeros((U_HALO, D_CONV), _F32)

    @pl.when(i > 0)
    def _():
        k_s[:, 0:KV_HALO, :] = k_s[:, ts:ts + KV_HALO, :]
        v_s[:, 0:KV_HALO, :] = v_s[:, ts:ts + KV_HALO, :]
        u_s[0:U_HALO, :] = u_s[ts:ts + U_HALO, :]

    for r in range(0, ts, ROW_CHUNK):
        xc = x_ref[r:r + ROW_CHUNK, :]
        ss = jnp.sum(xc * xc, axis=-1, keepdims=True)
        rs = lax.rsqrt(ss * (1.0 / D_MODEL) + EPS)
        h_s[r:r + ROW_CHUNK, :] = ((xc * rs) * nw_ref[...]).astype(_BF16)

    def proj(c0, width):
        return jnp.dot(h_s[...], win_ref[:, c0:c0 + width], preferred_element_type=_F32)

    bd = bd_ref[...]

    for half in range(2):
        q = proj(Q0 + 256 * half, 256)
        ssq = jnp.dot((q * q).astype(_BF16), bd, preferred_element_type=_F32)
        qn = (q * lax.rsqrt(ssq * (1.0 / HEAD_DIM) + EPS)) * qw_ref[:, 256 * half:256 * half + 256]
        q_s[:, 256 * half:256 * half + 256] = qn.astype(_BF16)

    kv = proj(K0, 256)
    k = kv[:, 0:128]
    v = kv[:, 128:256]
    kss = jnp.dot((k * k).astype(_BF16), bd[0:128, 0:128], preferred_element_type=_F32)
    kn = (k * lax.rsqrt(kss * (1.0 / HEAD_DIM) + EPS)) * kw_ref[...]
    kn_sw = pltpu.roll(kn, 64, axis=1)
    v_sw = pltpu.roll(v, 64, axis=1)
    lane = lax.broadcasted_iota(jnp.int32, (ts, 128), 1)
    lo = lane < HEAD_DIM
    zero = jnp.zeros((ts, 128), _F32)
    one = jnp.ones((ts, 128), _F32)
    cur = slice(KV_HALO, KV_HALO + ts)
    k_s[0, cur, :] = jnp.where(lo, kn, zero).astype(_BF16)
    k_s[1, cur, :] = jnp.where(lo, zero, kn_sw).astype(_BF16)
    k_s[2, cur, :] = jnp.where(lo, kn_sw, zero).astype(_BF16)
    k_s[3, cur, :] = jnp.where(lo, zero, kn).astype(_BF16)
    ones_lo = jnp.where(lo, one, zero).astype(_BF16)
    ones_hi = jnp.where(lo, zero, one).astype(_BF16)
    v_s[0, cur, 0:128] = jnp.where(lo, v, zero).astype(_BF16)
    v_s[1, cur, 0:128] = jnp.where(lo, zero, v_sw).astype(_BF16)
    v_s[2, cur, 0:128] = jnp.where(lo, v_sw, zero).astype(_BF16)
    v_s[3, cur, 0:128] = jnp.where(lo, zero, v).astype(_BF16)
    v_s[0, cur, 128:256] = ones_lo
    v_s[1, cur, 128:256] = ones_hi
    v_s[2, cur, 128:256] = ones_lo
    v_s[3, cur, 128:256] = ones_hi

    for half in range(2):
        cs = slice(256 * half, 256 * half + 256)
        za = proj(ZA0 + 256 * half, 256)
        ga_s[:, cs] = za * _sigmoid(za)
        a = proj(UA0 + 256 * half, 256)
        g = proj(UG0 + 256 * half, 256)
        u_s[U_HALO:U_HALO + ts, cs] = a * _sigmoid(g)
        zc = proj(ZC0 + 256 * half, 256)
        gc_s[:, cs] = zc * _sigmoid(zc)

    for cb in range(D_CONV // 128):
        ls = slice(128 * cb, 128 * cb + 128)
        for r in range(0, ts, ROW_CHUNK):
            acc = jnp.zeros((ROW_CHUNK, 128), _F32)
            for j in range(CONV_WIDTH):
                r0 = r + U_OFF + j
                acc = acc + dww_ref[j:j + 1, ls] * u_s[r0:r0 + ROW_CHUNK, ls]
            c_s[r:r + ROW_CHUNK, ls] = acc + dwb_ref[:, ls]

    for r in range(0, ts, ROW_CHUNK):
        rr = slice(r, r + ROW_CHUNK)
        y = c_s[rr, :]
        mu = jnp.sum(y, axis=-1, keepdims=True) * (1.0 / D_CONV)
        d = y - mu
        var = jnp.sum(d * d, axis=-1, keepdims=True) * (1.0 / D_CONV)
        yn = (d * lax.rsqrt(var + LN_EPS)) * lnw_ref[...] + lnb_ref[...]
        y_s[rr, D_ATTN:D_ATTN + D_CONV] = ((yn * _sigmoid(yn)) * gc_s[rr, :]).astype(_BF16)

    lane_q = lax.broadcasted_iota(jnp.int32, (BLOCK, 128), 1)
    lo_q = lane_q < HEAD_DIM
    for b in range(ts // BLOCK):
        rows = slice(BLOCK * b, BLOCK * b + BLOCK)
        krows = slice(BLOCK * b, BLOCK * b + 2 * BLOCK)
        tbl = jnp.where(i == 0, 1, 0) if b == 0 else 0
        for g in range(N_KV_HEADS):
            c0 = 256 * g
            qg = jnp.concatenate([q_s[rows, c0:c0 + 128], q_s[rows, c0 + 128:c0 + 256]], axis=0)
            acc = None
            extra = []
            for e in range(2):
                s = lax.dot_general(qg, k_s[2 * g + e, krows, :], (((1,), (1,)), ((), ())),
                                    preferred_element_type=_F32)
                s = s + bias_s[tbl, 2 * g + e]
                ps = []
                ex = []
                for j in range(2):
                    sink = sink_ref[4 * g + 2 * j + e]
                    sj = s[BLOCK * j:BLOCK * j + BLOCK, :]
                    m = jnp.maximum(jnp.max(sj, axis=-1, keepdims=True), sink)
                    ps.append(jnp.exp(sj - m))
                    ex.append(jnp.exp(sink - m))
                p = jnp.concatenate(ps, axis=0).astype(_BF16)
                o = jnp.dot(p, v_s[2 * g + e, krows, :], preferred_element_type=_F32)
                acc = o if acc is None else acc + o
                extra.append(ex)
            for j in range(2):
                oj = acc[BLOCK * j:BLOCK * j + BLOCK, :]
                den = oj[:, 128:256] + jnp.where(lo_q, extra[0][j], extra[1][j])
                cs = slice(c0 + 128 * j, c0 + 128 * j + 128)
                y_s[rows, cs] = ((oj[:, 0:128] / den) * ga_s[rows, cs]).astype(_BF16)

    for n in range(0, D_MODEL, 256):
        o_ref[:, n:n + 256] = x_ref[:, n:n + 256] + jnp.dot(
            y_s[...], wout_ref[:, n:n + 256], preferred_element_type=_F32)


def _banded_buckets():
    bucket, band = _t5_bucket_table()
    return np.where(band, bucket, -1).astype(np.int32)


def _layer(x, norm_w, w_in, q_norm_w, k_norm_w, sinks, dw_w, dw_b, ln_w, ln_b, w_out, rel_bias):
    B, S, D = x.shape
    ts = SEQ_TILE
    assert D == D_MODEL and S % ts == 0 and ts % BLOCK == 0

    idx = np.arange(256) // HEAD_DIM
    bd = jnp.asarray((idx[:, None] == idx[None, :]).astype(np.float32), dtype=_BF16)
    qw = (jnp.tile(q_norm_w.astype(_F32), N_HEADS) * (HEAD_DIM ** -0.5)).reshape(1, D_ATTN)
    kw = jnp.tile(k_norm_w.astype(_F32), N_KV_HEADS).reshape(1, 128)
    dww = jnp.concatenate([dw_w.astype(_F32), jnp.zeros((1, D_CONV), _F32)], axis=0)

    const2 = lambda b, i: (0, 0)
    grid = (B, S // ts)
    in_specs = [
        pl.BlockSpec(memory_space=pltpu.SMEM),
        pl.BlockSpec(memory_space=pltpu.SMEM),
        pl.BlockSpec((None, ts, D), lambda b, i: (b, i, 0)),
        pl.BlockSpec((1, D), const2),
        pl.BlockSpec((D, D_IN), const2),
        pl.BlockSpec((1, D_ATTN), const2),
        pl.BlockSpec((1, 128), const2),
        pl.BlockSpec((256, 256), const2),
        pl.BlockSpec((32, D_CONV), const2),
        pl.BlockSpec((1, D_CONV), const2),
        pl.BlockSpec((1, D_CONV), const2),
        pl.BlockSpec((1, D_CONV), const2),
        pl.BlockSpec((D, D), const2),
        pl.BlockSpec((BLOCK, 2 * BLOCK), const2),
    ]
    scratch = [
        pltpu.VMEM((ts, D), _BF16),
        pltpu.VMEM((ts, D_ATTN), _BF16),
        pltpu.VMEM((4, ts + KV_HALO, 128), _BF16),
        pltpu.VMEM((4, ts + KV_HALO, 256), _BF16),
        pltpu.VMEM((ts + U_HALO, D_CONV), _F32),
        pltpu.VMEM((ts, D_CONV), _F32),
        pltpu.VMEM((ts, D_ATTN), _F32),
        pltpu.VMEM((ts, D_CONV), _F32),
        pltpu.VMEM((ts, D), _BF16),
        pltpu.VMEM((2, 4, 2 * BLOCK, 2 * BLOCK), _F32),
    ]
    return pl.pallas_call(
        _layer_kernel,
        grid=grid,
        in_specs=in_specs,
        out_specs=pl.BlockSpec((None, ts, D), lambda b, i: (b, i, 0)),
        out_shape=jax.ShapeDtypeStruct((B, S, D), x.dtype),
        scratch_shapes=scratch,
        compiler_params=pltpu.CompilerParams(
            dimension_semantics=("arbitrary", "arbitrary"),
            vmem_limit_bytes=VMEM_LIMIT_BYTES),
        name="hybrid_layer",
    )(sinks.astype(_F32), rel_bias.astype(_F32).reshape(NUM_BUCKETS * N_HEADS), x,
      norm_w.astype(_F32).reshape(1, D), w_in.astype(_BF16), qw, kw, bd,
      dww, dw_b.astype(_F32).reshape(1, D_CONV), ln_w.astype(_F32).reshape(1, D_CONV),
      ln_b.astype(_F32).reshape(1, D_CONV), w_out.astype(_BF16), jnp.asarray(_banded_buckets()))


def kernel(x, norm_w, w_in, q_norm_w, k_norm_w, sinks, dw_w, dw_b, ln_w, ln_b, w_out, rel_bias):
    depth = norm_w.shape[0]
    for l in range(depth):
        x = _layer(x, norm_w[l], w_in[l], q_norm_w[l], k_norm_w[l], sinks[l], dw_w[l], dw_b[l],
                   ln_w[l], ln_b[l], w_out[l], rel_bias)
    return x
```

```python
import functools
import math

import numpy as np
import jax
import jax.numpy as jnp
from jax import lax
from jax.experimental import pallas as pl
from jax.experimental.pallas import tpu as pltpu

D_MODEL = 1024
D_ATTN = 512
D_CONV = 512
HEAD_DIM = 64
N_HEADS = 8
N_KV_HEADS = 2
BLOCK = 128
NUM_BUCKETS = 32
MAX_DISTANCE = 128
CONV_WIDTH = 31
D_IN = 2816
EPS = 1e-6
LN_EPS = 1e-5
NEG_INF = -1e30

SEQ_TILE = 512
KV_HALO = BLOCK
ROW_CHUNK = 64
SUBLANES = 8
T_BLK = SEQ_TILE // SUBLANES
U_PITCH = T_BLK + 1
U_ROWS = ((SUBLANES + 1) * U_PITCH + 7) // 8 * 8
C_PITCH = T_BLK + 8
C_ROWS = SUBLANES * C_PITCH
CONV_GROUP = 8
VMEM_LIMIT_BYTES = 56 * 1024 * 1024

Q0, K0, V0, ZA0, UA0, UG0, ZC0 = 0, 512, 640, 768, 1280, 1792, 2304

_F32 = jnp.float32
_BF16 = jnp.bfloat16


def _t5_bucket_table():
    qi = np.arange(BLOCK)[:, None]
    sj = np.arange(2 * BLOCK)[None, :]
    dist = qi + BLOCK - sj
    n = np.maximum(dist, 0)
    max_exact = NUM_BUCKETS // 2
    nf = np.maximum(n, 1).astype(np.float32)
    large = max_exact + (np.log(nf / max_exact) / math.log(MAX_DISTANCE / max_exact)
                         * (NUM_BUCKETS - max_exact)).astype(np.int32)
    large = np.minimum(large, NUM_BUCKETS - 1)
    bucket = np.where(n < max_exact, n, large).astype(np.int32)
    band = (dist >= 0) & (dist < BLOCK)
    return bucket, band


def _sigmoid(z):
    return jax.nn.sigmoid(z)


def _build_bias_tables(relb_ref, bucket_ref, bias_s):
    cur_keys = lax.broadcasted_iota(jnp.int32, (8, 2 * BLOCK), 1) >= BLOCK

    def body(c, carry):
        r0 = pl.multiple_of(c * 8, 8)
        bk = bucket_ref[pl.ds(r0, 8), :]
        accs = [jnp.full((8, 2 * BLOCK), NEG_INF, _F32) for _ in range(N_HEADS)]
        for b in range(NUM_BUCKETS):
            hit = bk == b
            for h in range(N_HEADS):
                accs[h] = jnp.where(hit, relb_ref[b * N_HEADS + h], accs[h])
        for h in range(N_HEADS):
            g, rem = divmod(h, 4)
            j, e = divmod(rem, 2)
            rows = pl.ds(BLOCK * j + r0, 8)
            bias_s[0, 2 * g + e, rows, :] = accs[h]
            bias_s[1, 2 * g + e, rows, :] = jnp.where(cur_keys, accs[h], NEG_INF)
        return carry

    lax.fori_loop(0, BLOCK // 8, body, 0)


def _layer_kernel(sink_ref, relb_ref, x_ref, nw_ref, win_ref, qw_ref, kw_ref, bd_ref, dww_ref,
                  dwb_ref, lnw_ref, lnb_ref, wout_ref, bucket_ref,
                  o_ref,
                  h_s, q_s, k_s, v_s, u_s, c_s, yc_s, ga_s, gc_s, y_s, bias_s):
    ts = SEQ_TILE
    i = pl.program_id(1)

    @pl.when((pl.program_id(0) == 0) & (i == 0))
    def _():
        _build_bias_tables(relb_ref, bucket_ref, bias_s)

    @pl.when(i == 0)
    def _():
        k_s[:, 0:KV_HALO, :] = jnp.zeros((4, KV_HALO, 128), _BF16)
        v_s[:, 0:KV_HALO, :] = jnp.zeros((4, KV_HALO, 256), _BF16)
        u_s[:, 0:T_BLK, :] = jnp.zeros((D_CONV // 128, T_BLK, 128), _F32)

    @pl.when(i > 0)
    def _():
        k_s[:, 0:KV_HALO, :] = k_s[:, ts:ts + KV_HALO, :]
        v_s[:, 0:KV_HALO, :] = v_s[:, ts:ts + KV_HALO, :]
        u_s[:, 0:T_BLK, :] = u_s[:, SUBLANES * U_PITCH:SUBLANES * U_PITCH + T_BLK, :]

    for r in range(0, ts, ROW_CHUNK):
        xc = x_ref[r:r + ROW_CHUNK, :]
        ss = jnp.sum(xc * xc, axis=-1, keepdims=True)
        rs = lax.rsqrt(ss * (1.0 / D_MODEL) + EPS)
        h_s[r:r + ROW_CHUNK, :] = ((xc * rs) * nw_ref[...]).astype(_BF16)

    def proj(c0, width):
        return jnp.dot(h_s[...], win_ref[:, c0:c0 + width], preferred_element_type=_F32)

    bd = bd_ref[...]

    for half in range(2):
        q = proj(Q0 + 256 * half, 256)
        ssq = jnp.dot((q * q).astype(_BF16), bd, preferred_element_type=_F32)
        qn = (q * lax.rsqrt(ssq * (1.0 / HEAD_DIM) + EPS)) * qw_ref[:, 256 * half:256 * half + 256]
        q_s[:, 256 * half:256 * half + 256] = qn.astype(_BF16)

    kv = proj(K0, 256)
    k = kv[:, 0:128]
    v = kv[:, 128:256]
    kss = jnp.dot((k * k).astype(_BF16), bd[0:128, 0:128], preferred_element_type=_F32)
    kn = (k * lax.rsqrt(kss * (1.0 / HEAD_DIM) + EPS)) * kw_ref[...]
    kn_sw = pltpu.roll(kn, 64, axis=1)
    v_sw = pltpu.roll(v, 64, axis=1)
    lane = lax.broadcasted_iota(jnp.int32, (ts, 128), 1)
    lo = lane < HEAD_DIM
    zero = jnp.zeros((ts, 128), _F32)
    one = jnp.ones((ts, 128), _F32)
    cur = slice(KV_HALO, KV_HALO + ts)
    k_s[0, cur, :] = jnp.where(lo, kn, zero).astype(_BF16)
    k_s[1, cur, :] = jnp.where(lo, zero, kn_sw).astype(_BF16)
    k_s[2, cur, :] = jnp.where(lo, kn_sw, zero).astype(_BF16)
    k_s[3, cur, :] = jnp.where(lo, zero, kn).astype(_BF16)
    ones_lo = jnp.where(lo, one, zero).astype(_BF16)
    ones_hi = jnp.where(lo, zero, one).astype(_BF16)
    v_s[0, cur, 0:128] = jnp.where(lo, v, zero).astype(_BF16)
    v_s[1, cur, 0:128] = jnp.where(lo, zero, v_sw).astype(_BF16)
    v_s[2, cur, 0:128] = jnp.where(lo, v_sw, zero).astype(_BF16)
    v_s[3, cur, 0:128] = jnp.where(lo, zero, v).astype(_BF16)
    v_s[0, cur, 128:256] = ones_lo
    v_s[1, cur, 128:256] = ones_hi
    v_s[2, cur, 128:256] = ones_lo
    v_s[3, cur, 128:256] = ones_hi

    for half in range(2):
        cs = slice(256 * half, 256 * half + 256)
        za = proj(ZA0 + 256 * half, 256)
        ga_s[:, cs] = za * _sigmoid(za)
        a = proj(UA0 + 256 * half, 256)
        g = proj(UG0 + 256 * half, 256)
        u = a * _sigmoid(g)
        for sl in range(2):
            for blk in range(SUBLANES):
                p0 = U_PITCH * (blk + 1)
                u_s[2 * half + sl, p0:p0 + T_BLK, :] = u[T_BLK * blk:T_BLK * (blk + 1),
                                                        128 * sl:128 * sl + 128]
        zc = proj(ZC0 + 256 * half, 256)
        gc_s[:, cs] = zc * _sigmoid(zc)

    taps = CONV_WIDTH - 1
    for cb in range(D_CONV // 128):
        ls = slice(128 * cb, 128 * cb + 128)
        wv = [jnp.broadcast_to(dww_ref[j:j + 1, ls], (SUBLANES, 128)) for j in range(CONV_WIDTH)]
        bv = jnp.broadcast_to(dwb_ref[:, ls], (SUBLANES, 128))
        for r0 in range(0, T_BLK, CONV_GROUP):
            accs = [None] * CONV_GROUP
            for tau in range(r0 - taps, r0 + CONV_GROUP):
                base = U_PITCH + tau if tau >= 0 else T_BLK + tau
                win = u_s[cb, pl.ds(base, SUBLANES, stride=U_PITCH), :]
                for n in range(CONV_GROUP):
                    j = tau - (r0 + n) + taps
                    if 0 <= j < CONV_WIDTH:
                        term = wv[j] * win
                        accs[n] = term if accs[n] is None else accs[n] + term
            for n in range(CONV_GROUP):
                row = SUBLANES * (r0 + n)
                c_s[row:row + SUBLANES, ls] = accs[n] + bv

    for r in range(0, ts, ROW_CHUNK):
        y = c_s[r:r + ROW_CHUNK, :]
        mu = jnp.sum(y, axis=-1, keepdims=True) * (1.0 / D_CONV)
        d = y - mu
        var = jnp.sum(d * d, axis=-1, keepdims=True) * (1.0 / D_CONV)
        yn = (d * lax.rsqrt(var + LN_EPS)) * lnw_ref[...] + lnb_ref[...]
        act = yn * _sigmoid(yn)
        for n in range(ROW_CHUNK // SUBLANES):
            sv = r // SUBLANES + n
            for cb in range(D_CONV // 128):
                yc_s[cb, pl.ds(sv, SUBLANES, stride=C_PITCH), :] = act[SUBLANES * n:SUBLANES * (n + 1),
                                                                       128 * cb:128 * cb + 128]

    for blk in range(SUBLANES):
        tr = slice(T_BLK * blk, T_BLK * (blk + 1))
        for cb in range(D_CONV // 128):
            ls = slice(128 * cb, 128 * cb + 128)
            val = yc_s[cb, C_PITCH * blk:C_PITCH * blk + T_BLK, :] * gc_s[tr, ls]
            y_s[tr, D_ATTN + 128 * cb:D_ATTN + 128 * cb + 128] = val.astype(_BF16)

    lane_q = lax.broadcasted_iota(jnp.int32, (BLOCK, 128), 1)
    lo_q = lane_q < HEAD_DIM
    for b in range(ts // BLOCK):
        rows = slice(BLOCK * b, BLOCK * b + BLOCK)
        krows = slice(BLOCK * b, BLOCK * b + 2 * BLOCK)
        tbl = jnp.where(i == 0, 1, 0) if b == 0 else 0
        for g in range(N_KV_HEADS):
            c0 = 256 * g
            qg = jnp.concatenate([q_s[rows, c0:c0 + 128], q_s[rows, c0 + 128:c0 + 256]], axis=0)
            acc = None
            extra = []
            for e in range(2):
                s = lax.dot_general(qg, k_s[2 * g + e, krows, :], (((1,), (1,)), ((), ())),
                                    preferred_element_type=_F32)
                s = s + bias_s[tbl, 2 * g + e]
                ps = []
                ex = []
                for j in range(2):
                    sink = sink_ref[4 * g + 2 * j + e]
                    sj = s[BLOCK * j:BLOCK * j + BLOCK, :]
                    m = jnp.maximum(jnp.max(sj, axis=-1, keepdims=True), sink)
                    ps.append(jnp.exp(sj - m))
                    ex.append(jnp.exp(sink - m))
                p = jnp.concatenate(ps, axis=0).astype(_BF16)
                o = jnp.dot(p, v_s[2 * g + e, krows, :], preferred_element_type=_F32)
                acc = o if acc is None else acc + o
                extra.append(ex)
            for j in range(2):
                oj = acc[BLOCK * j:BLOCK * j + BLOCK, :]
                den = oj[:, 128:256] + jnp.where(lo_q, extra[0][j], extra[1][j])
                cs = slice(c0 + 128 * j, c0 + 128 * j + 128)
                y_s[rows, cs] = ((oj[:, 0:128] / den) * ga_s[rows, cs]).astype(_BF16)

    for n in range(0, D_MODEL, 256):
        o_ref[:, n:n + 256] = x_ref[:, n:n + 256] + jnp.dot(
            y_s[...], wout_ref[:, n:n + 256], preferred_element_type=_F32)


def _banded_buckets():
    bucket, band = _t5_bucket_table()
    return np.where(band, bucket, -1).astype(np.int32)


def _layer(x, norm_w, w_in, q_norm_w, k_norm_w, sinks, dw_w, dw_b, ln_w, ln_b, w_out, rel_bias):
    B, S, D = x.shape
    ts = SEQ_TILE
    assert D == D_MODEL and S % ts == 0 and ts % BLOCK == 0

    idx = np.arange(256) // HEAD_DIM
    bd = jnp.asarray((idx[:, None] == idx[None, :]).astype(np.float32), dtype=_BF16)
    qw = (jnp.tile(q_norm_w.astype(_F32), N_HEADS) * (HEAD_DIM ** -0.5)).reshape(1, D_ATTN)
    kw = jnp.tile(k_norm_w.astype(_F32), N_KV_HEADS).reshape(1, 128)
    dww = jnp.concatenate([dw_w.astype(_F32), jnp.zeros((1, D_CONV), _F32)], axis=0)

    const2 = lambda b, i: (0, 0)
    grid = (B, S // ts)
    in_specs = [
        pl.BlockSpec(memory_space=pltpu.SMEM),
        pl.BlockSpec(memory_space=pltpu.SMEM),
        pl.BlockSpec((None, ts, D), lambda b, i: (b, i, 0)),
        pl.BlockSpec((1, D), const2),
        pl.BlockSpec((D, D_IN), const2),
        pl.BlockSpec((1, D_ATTN), const2),
        pl.BlockSpec((1, 128), const2),
        pl.BlockSpec((256, 256), const2),
        pl.BlockSpec((32, D_CONV), const2),
        pl.BlockSpec((1, D_CONV), const2),
        pl.BlockSpec((1, D_CONV), const2),
        pl.BlockSpec((1, D_CONV), const2),
        pl.BlockSpec((D, D), const2),
        pl.BlockSpec((BLOCK, 2 * BLOCK), const2),
    ]
    scratch = [
        pltpu.VMEM((ts, D), _BF16),
        pltpu.VMEM((ts, D_ATTN), _BF16),
        pltpu.VMEM((4, ts + KV_HALO, 128), _BF16),
        pltpu.VMEM((4, ts + KV_HALO, 256), _BF16),
        pltpu.VMEM((D_CONV // 128, U_ROWS, 128), _F32),
        pltpu.VMEM((ts, D_CONV), _F32),
        pltpu.VMEM((D_CONV // 128, C_ROWS, 128), _F32),
        pltpu.VMEM((ts, D_ATTN), _F32),
        pltpu.VMEM((ts, D_CONV), _F32),
        pltpu.VMEM((ts, D), _BF16),
        pltpu.VMEM((2, 4, 2 * BLOCK, 2 * BLOCK), _F32),
    ]
    return pl.pallas_call(
        _layer_kernel,
        grid=grid,
        in_specs=in_specs,
        out_specs=pl.BlockSpec((None, ts, D), lambda b, i: (b, i, 0)),
        out_shape=jax.ShapeDtypeStruct((B, S, D), x.dtype),
        scratch_shapes=scratch,
        compiler_params=pltpu.CompilerParams(
            dimension_semantics=("arbitrary", "arbitrary"),
            vmem_limit_bytes=VMEM_LIMIT_BYTES),
        name="hybrid_layer",
    )(sinks.astype(_F32), rel_bias.astype(_F32).reshape(NUM_BUCKETS * N_HEADS), x,
      norm_w.astype(_F32).reshape(1, D), w_in.astype(_BF16), qw, kw, bd,
      dww, dw_b.astype(_F32).reshape(1, D_CONV), ln_w.astype(_F32).reshape(1, D_CONV),
      ln_b.astype(_F32).reshape(1, D_CONV), w_out.astype(_BF16), jnp.asarray(_banded_buckets()))


def kernel(x, norm_w, w_in, q_norm_w, k_norm_w, sinks, dw_w, dw_b, ln_w, ln_b, w_out, rel_bias):
    depth = norm_w.shape[0]
    for l in range(depth):
        x = _layer(x, norm_w[l], w_in[l], q_norm_w[l], k_norm_w[l], sinks[l], dw_w[l], dw_b[l],
                   ln_w[l], ln_b[l], w_out[l], rel_bias)
    return x
```

```python
import functools
import math

import numpy as np
import jax
import jax.numpy as jnp
from jax import lax
from jax.experimental import pallas as pl
from jax.experimental.pallas import tpu as pltpu

D_MODEL = 1024
D_ATTN = 512
D_CONV = 512
HEAD_DIM = 64
N_HEADS = 8
N_KV_HEADS = 2
BLOCK = 128
NUM_BUCKETS = 32
MAX_DISTANCE = 128
CONV_WIDTH = 31
D_IN = 2816
EPS = 1e-6
LN_EPS = 1e-5
NEG_INF = -1e30

SEQ_TILE = 512
KV_HALO = BLOCK
ROW_CHUNK = 64
SUBLANES = 8
N_SLABS = D_CONV // 128
T_BLK = SEQ_TILE // SUBLANES
U_PITCH = T_BLK + 1
U_ROWS = ((SUBLANES + 1) * U_PITCH + 7) // 8 * 8
C_PITCH = T_BLK + 8
C_ROWS = SUBLANES * C_PITCH
CONV_GROUP = 8
TAP_BLOCK = CONV_WIDTH
N_CHUNK = 256
ATTN_SLOTS = 4
VMEM_LIMIT_BYTES = 56 * 1024 * 1024

Q0, K0, V0, ZA0, UA0, UG0, ZC0 = 0, 512, 640, 768, 1280, 1792, 2304

_F32 = jnp.float32
_BF16 = jnp.bfloat16


def _t5_bucket_table():
    qi = np.arange(BLOCK)[:, None]
    sj = np.arange(2 * BLOCK)[None, :]
    dist = qi + BLOCK - sj
    n = np.maximum(dist, 0)
    max_exact = NUM_BUCKETS // 2
    nf = np.maximum(n, 1).astype(np.float32)
    large = max_exact + (np.log(nf / max_exact) / math.log(MAX_DISTANCE / max_exact)
                         * (NUM_BUCKETS - max_exact)).astype(np.int32)
    large = np.minimum(large, NUM_BUCKETS - 1)
    bucket = np.where(n < max_exact, n, large).astype(np.int32)
    band = (dist >= 0) & (dist < BLOCK)
    return bucket, band


def _banded_buckets():
    bucket, band = _t5_bucket_table()
    return np.where(band, bucket, -1).astype(np.int32)


def _sigmoid(z):
    return 0.5 * jnp.tanh(0.5 * z) + 0.5


def _build_bias_tables(relb_ref, bucket_ref, bias_s):
    cur_keys = lax.broadcasted_iota(jnp.int32, (8, 2 * BLOCK), 1) >= BLOCK

    def body(c, carry):
        r0 = pl.multiple_of(c * 8, 8)
        bk = bucket_ref[pl.ds(r0, 8), :]
        accs = [jnp.full((8, 2 * BLOCK), NEG_INF, _F32) for _ in range(N_HEADS)]
        for b in range(NUM_BUCKETS):
            hit = bk == b
            for h in range(N_HEADS):
                accs[h] = jnp.where(hit, relb_ref[b * N_HEADS + h], accs[h])
        for h in range(N_HEADS):
            g, rem = divmod(h, 4)
            j, e = divmod(rem, 2)
            rows = pl.ds(BLOCK * j + r0, 8)
            bias_s[0, 2 * g + e, rows, :] = accs[h]
            bias_s[1, 2 * g + e, rows, :] = jnp.where(cur_keys, accs[h], NEG_INF)
        return carry

    lax.fori_loop(0, BLOCK // 8, body, 0)


class _RawProj:
    bounds = (Q0, UA0, ZC0, D_IN)

    def __init__(self, *refs):
        self.refs = refs

    def _find(self, c0, c1):
        for ref, lo, hi in zip(self.refs, self.bounds[:-1], self.bounds[1:]):
            if lo <= c0 and c1 <= hi:
                return ref, c0 - lo, c1 - lo
        raise ValueError((c0, c1))

    def load(self, rows, c0, c1):
        ref, a, b = self._find(c0, c1)
        return ref[rows, a:b]

    def store(self, c0, c1, val):
        ref, a, b = self._find(c0, c1)
        ref[:, a:b] = val


def _row_chunks():
    return [slice(r, r + ROW_CHUNK) for r in range(0, SEQ_TILE, ROW_CHUNK)]


def _stream_a(sa, x_ref, nw_ref, win_ref, qw_ref, kw_ref, bd_ref,
              h_s, p_s, ssq_s, q_s, k_s, v_s, u_s, ga_s, gc_s):
    ts = SEQ_TILE

    def rms(r):
        def run():
            xc = x_ref[r:r + ROW_CHUNK, :]
            ss = jnp.sum(xc * xc, axis=-1, keepdims=True)
            rs = lax.rsqrt(ss * (1.0 / D_MODEL) + EPS)
            h_s[r:r + ROW_CHUNK, :] = ((xc * rs) * nw_ref[...]).astype(_BF16)
        return run

    def dot(c0, c1):
        def run():
            p_s.store(c0, c1, jnp.dot(h_s[...], win_ref[:, c0:c1], preferred_element_type=_F32))
        return run

    def q_sumsq(half):
        def run():
            cs = slice(256 * half, 256 * half + 256)
            q = p_s.load(slice(None), Q0 + cs.start, Q0 + cs.stop)
            ssq_s[:, cs] = jnp.dot((q * q).astype(_BF16), bd_ref[...], preferred_element_type=_F32)
        return run

    def q_norm(half):
        def run():
            cs = slice(256 * half, 256 * half + 256)
            for rr in _row_chunks():
                q = p_s.load(rr, Q0 + cs.start, Q0 + cs.stop)
                qn =(q * lax.rsqrt(ssq_s[rr, cs] * (1.0 / HEAD_DIM) + EPS)) * qw_ref[:, cs]
                q_s[sa, rr, cs] = qn.astype(_BF16)
        return run

    def k_sumsq():
        k = p_s.load(slice(None), K0, K0 + 128)
        ssq_s[:, D_ATTN:D_ATTN + 128] = jnp.dot((k * k).astype(_BF16), bd_ref[0:128, 0:128],
                                                preferred_element_type=_F32)

    def kv_store():
        lo = lax.broadcasted_iota(jnp.int32, (ROW_CHUNK, 128), 1) < HEAD_DIM
        zero = jnp.zeros((ROW_CHUNK, 128), _F32)
        one = jnp.ones((ROW_CHUNK, 128), _F32)
        ones_lo = jnp.where(lo, one, zero).astype(_BF16)
        ones_hi = jnp.where(lo, zero, one).astype(_BF16)
        for rr in _row_chunks():
            cur = slice(KV_HALO + rr.start, KV_HALO + rr.stop)
            k = p_s.load(rr, K0, K0 + 128)
            v = p_s.load(rr, V0, V0 + 128)
            kn = (k * lax.rsqrt(ssq_s[rr, D_ATTN:D_ATTN + 128] * (1.0 / HEAD_DIM) + EPS)) * kw_ref[...]
            kn_sw = pltpu.roll(kn, 64, axis=1)
            v_sw = pltpu.roll(v, 64, axis=1)
            k_s[sa, 0, cur, :] = jnp.where(lo, kn, zero).astype(_BF16)
            k_s[sa, 1, cur, :] = jnp.where(lo, zero, kn_sw).astype(_BF16)
            k_s[sa, 2, cur, :] = jnp.where(lo, kn_sw, zero).astype(_BF16)
            k_s[sa, 3, cur, :] = jnp.where(lo, zero, kn).astype(_BF16)
            v_s[sa, 0, cur, 0:128] = jnp.where(lo, v, zero).astype(_BF16)
            v_s[sa, 1, cur, 0:128] = jnp.where(lo, zero, v_sw).astype(_BF16)
            v_s[sa, 2, cur, 0:128] = jnp.where(lo, v_sw, zero).astype(_BF16)
            v_s[sa, 3, cur, 0:128] = jnp.where(lo, zero, v).astype(_BF16)
            v_s[sa, 0, cur, 128:256] = ones_lo
            v_s[sa, 1, cur, 128:256] = ones_hi
            v_s[sa, 2, cur, 128:256] = ones_lo
            v_s[sa, 3, cur, 128:256] = ones_hi

    def gate(c0, dst, half):
        def run():
            cs = slice(256 * half, 256 * half + 256)
            for rr in _row_chunks():
                z = p_s.load(rr, c0 + 256 * half, c0 + 256 * half + 256)
                dst[sa, rr, cs] = z * _sigmoid(z)
        return run

    def glu(half):
        def run():
            for blk in range(SUBLANES):
                tr = slice(T_BLK * blk, T_BLK * (blk + 1))
                a = p_s.load(tr, UA0 + 256 * half, UA0 + 256 * half + 256)
                g = p_s.load(tr, UG0 + 256 * half, UG0 + 256 * half + 256)
                u = a * _sigmoid(g)
                p0 = U_PITCH * (blk + 1)
                for sl in range(2):
                    u_s[sa * N_SLABS + 2 * half + sl, p0:p0 + T_BLK, :] = u[:, 128 * sl:128 * sl + 128]
        return run

    return dict(
        rms=[rms(r) for r in range(0, ts, ROW_CHUNK)],
        dot_qkvz=dot(Q0, UA0), dot_ag=dot(UA0, ZC0), dot_zc=dot(ZC0, D_IN),
        q_sumsq=[q_sumsq(0), q_sumsq(1)], q_norm=[q_norm(0), q_norm(1)],
        k_sumsq=k_sumsq, kv_store=kv_store,
        gate_a=[gate(ZA0, ga_s, 0), gate(ZA0, ga_s, 1)],
        gate_c=[gate(ZC0, gc_s, 0), gate(ZC0, gc_s, 1)],
        glu=[glu(0), glu(1)])


def _stream_b(sb, first_b, sink_ref, dww_ref, dwb_ref, lnw_ref, lnb_ref,
              q_s, k_s, v_s, u_s, ga_s, gc_s, c_s, yc_s, y_s, bias_s, o_s, dn_s):
    ts = SEQ_TILE
    taps = CONV_WIDTH - 1

    def conv(r0, cb):
        def run():
            ls = slice(128 * cb, 128 * cb + 128)
            rows = [slice(SUBLANES * (r0 + n), SUBLANES * (r0 + n + 1)) for n in range(CONV_GROUP)]
            for j0 in range(0, CONV_WIDTH, TAP_BLOCK):
                j1 = min(j0 + TAP_BLOCK, CONV_WIDTH)
                wv = {j: jnp.broadcast_to(dww_ref[j:j + 1, ls], (SUBLANES, 128)) for j in range(j0, j1)}
                if j0 == 0:
                    accs = [jnp.broadcast_to(dwb_ref[:, ls], (SUBLANES, 128))] * CONV_GROUP
                else:
                    accs = [c_s[rows[n], ls] for n in range(CONV_GROUP)]
                for tau in range(r0 - taps + j0, r0 + CONV_GROUP - 1 - taps + j1):
                    base = U_PITCH + tau if tau >= 0 else T_BLK + tau
                    win = u_s[sb * N_SLABS + cb, pl.ds(base, SUBLANES, stride=U_PITCH), :]
                    for n in range(CONV_GROUP):
                        j = tau - (r0 + n) + taps
                        if j0 <= j < j1:
                            accs[n] = accs[n] + wv[j] * win
                for n in range(CONV_GROUP):
                    c_s[rows[n], ls] = accs[n]
        return run

    def ln(r):
        def run():
            y = c_s[r:r + ROW_CHUNK, :]
            mu = jnp.sum(y, axis=-1, keepdims=True) * (1.0 / D_CONV)
            d = y - mu
            var = jnp.sum(d * d, axis=-1, keepdims=True) * (1.0 / D_CONV)
            yn = (d * lax.rsqrt(var + LN_EPS)) * lnw_ref[...] + lnb_ref[...]
            act = yn * _sigmoid(yn)
            for n in range(ROW_CHUNK // SUBLANES):
                sv = r // SUBLANES + n
                for cb in range(N_SLABS):
                    yc_s[cb, pl.ds(sv, SUBLANES, stride=C_PITCH), :] = (
                        act[SUBLANES * n:SUBLANES * (n + 1), 128 * cb:128 * cb + 128])
        return run

    def regate(blk):
        def run():
            tr = slice(T_BLK * blk, T_BLK * (blk + 1))
            for cb in range(N_SLABS):
                ls = slice(128 * cb, 128 * cb + 128)
                val = yc_s[cb, C_PITCH * blk:C_PITCH * blk + T_BLK, :] * gc_s[sb, tr, ls]
                y_s[sb, tr, D_ATTN + 128 * cb:D_ATTN + 128 * cb + 128] = val.astype(_BF16)
        return run

    def attn_front(b, g):
        def run():
            slot = (2 * b + g) % ATTN_SLOTS
            rows = slice(BLOCK * b, BLOCK * b + BLOCK)
            krows = slice(BLOCK * b, BLOCK * b + 2 * BLOCK)
            tbl = first_b if b == 0 else 0
            lo_q = lax.broadcasted_iota(jnp.int32, (BLOCK, 128), 1) < HEAD_DIM
            c0 = 256 * g
            qg = jnp.concatenate([q_s[sb, rows, c0:c0 + 128], q_s[sb, rows, c0 + 128:c0 + 256]], axis=0)
            kk = jnp.concatenate([k_s[sb, 2 * g, krows, :], k_s[sb, 2 * g + 1, krows, :]], axis=0)
            s = lax.dot_general(qg, kk, (((1,), (1,)), ((), ())),
                                preferred_element_type=_F32)
            ps = [[None, None], [None, None]]
            ex = [[None, None], [None, None]]
            for e in range(2):
                se = s[:, 2 * BLOCK * e:2 * BLOCK * (e + 1)] + bias_s[tbl, 2 * g + e]
                for j in range(2):
                    sink = sink_ref[4 * g + 2 * j + e]
                    sj = se[BLOCK * j:BLOCK * j + BLOCK, :]
                    m = jnp.maximum(jnp.max(sj, axis=-1, keepdims=True), sink)
                    ps[j][e] = jnp.exp(sj - m).astype(_BF16)
                    ex[j][e] = jnp.exp(sink - m)
            p = jnp.concatenate([jnp.concatenate(ps[0], axis=1), jnp.concatenate(ps[1], axis=1)], axis=0)
            vv = jnp.concatenate([v_s[sb, 2 * g, krows, :], v_s[sb, 2 * g + 1, krows, :]], axis=0)
            o_s[slot] = jnp.dot(p, vv, preferred_element_type=_F32)
            for j in range(2):
                dn_s[slot, j] = jnp.where(lo_q, ex[j][0], ex[j][1])
        return run

    def attn_norm(b, g):
        def run():
            slot = (2 * b + g) % ATTN_SLOTS
            rows = slice(BLOCK * b, BLOCK * b + BLOCK)
            for j in range(2):
                oj = o_s[slot, BLOCK * j:BLOCK * j + BLOCK, :]
                den = oj[:, 128:256] + dn_s[slot, j]
                cs = slice(256 * g + 128 * j, 256 * g + 128 * j + 128)
                y_s[sb, rows, cs] = ((oj[:, 0:128] / den) * ga_s[sb, rows, cs]).astype(_BF16)
        return run

    pairs = [(b, g) for b in range(ts // BLOCK) for g in range(N_KV_HEADS)]
    return dict(attn_front=[attn_front(b, g) for b, g in pairs],
                attn_norm=[attn_norm(b, g) for b, g in pairs],
                conv=[conv(r0, cb) for r0 in range(0, T_BLK, CONV_GROUP) for cb in range(N_SLABS)],
                ln=[ln(r) for r in range(0, ts, ROW_CHUNK)],
                regate=[regate(blk) for blk in range(SUBLANES)])


def _stream_c(sy, xres_ref, wout_ref, o_ref, y_s):
    def outproj(n):
        def run():
            o_ref[:, n:n + N_CHUNK] = xres_ref[:, n:n + N_CHUNK] + jnp.dot(
                y_s[sy], wout_ref[:, n:n + N_CHUNK], preferred_element_type=_F32)
        return run

    return [outproj(n) for n in range(0, D_MODEL, N_CHUNK)]


def _layer_kernel(sink_ref, relb_ref, x_ref, xres_ref, nw_ref, win_ref, qw_ref, kw_ref, bd_ref, dww_ref,
                  dwb_ref, lnw_ref, lnb_ref, wout_ref, bucket_ref,
                  o_ref,
                  h_s, p1_s, p2_s, p3_s, ssq_s, q_s, k_s, v_s, u_s, ga_s, gc_s, c_s, yc_s, y_s, bias_s,
                  o_s, dn_s, *, tiles_per_seq):
    p_s = _RawProj(p1_s, p2_s, p3_s)
    ts = SEQ_TILE
    t = pl.program_id(0)
    sa = lax.rem(t, 2)
    sb = 1 - sa
    first_a = lax.rem(t, tiles_per_seq) == 0
    first_b = jnp.where(lax.rem(t + tiles_per_seq - 1, tiles_per_seq) == 0, 1, 0)

    @pl.when(t == 0)
    def _():
        _build_bias_tables(relb_ref, bucket_ref, bias_s)
        q_s[1] = jnp.zeros(q_s.shape[1:], _BF16)
        k_s[1] = jnp.zeros(k_s.shape[1:], _BF16)
        v_s[1] = jnp.zeros(v_s.shape[1:], _BF16)
        u_s[N_SLABS:2 * N_SLABS] = jnp.zeros((N_SLABS,) + u_s.shape[1:], _F32)
        ga_s[1] = jnp.zeros(ga_s.shape[1:], _F32)
        gc_s[1] = jnp.zeros(gc_s.shape[1:], _F32)
        y_s[...] = jnp.zeros(y_s.shape, _BF16)

    @pl.when(first_a)
    def _():
        k_s[sa, :, 0:KV_HALO, :] = jnp.zeros((4, KV_HALO, 128), _BF16)
        v_s[sa, :, 0:KV_HALO, :] = jnp.zeros((4, KV_HALO, 256), _BF16)
        for cb in range(N_SLABS):
            u_s[sa * N_SLABS + cb, 0:T_BLK, :] = jnp.zeros((T_BLK, 128), _F32)

    @pl.when(jnp.logical_not(first_a))
    def _():
        k_s[sa, :, 0:KV_HALO, :] = k_s[sb, :, ts:ts + KV_HALO, :]
        v_s[sa, :, 0:KV_HALO, :] = v_s[sb, :, ts:ts + KV_HALO, :]
        for cb in range(N_SLABS):
            u_s[sa * N_SLABS + cb, 0:T_BLK, :] = u_s[sb * N_SLABS + cb,
                                                     SUBLANES * U_PITCH:SUBLANES * U_PITCH + T_BLK, :]

    A = _stream_a(sa, x_ref, nw_ref, win_ref, qw_ref, kw_ref, bd_ref,
                  h_s, p_s, ssq_s, q_s, k_s, v_s, u_s, ga_s, gc_s)
    Bm = _stream_b(sb, first_b, sink_ref, dww_ref, dwb_ref, lnw_ref, lnb_ref,
                   q_s, k_s, v_s, u_s, ga_s, gc_s, c_s, yc_s, y_s, bias_s, o_s, dn_s)
    C = _stream_c(sa, xres_ref, wout_ref, o_ref, y_s)
    cv, af, an = Bm["conv"], Bm["attn_front"], Bm["attn_norm"]

    regions = [
        A["rms"] + cv,
        Bm["ln"] + Bm["regate"] + af[0:4] + an[0:4] + [A["dot_qkvz"], A["dot_zc"]],
        A["q_sumsq"] + [A["k_sumsq"]] + A["gate_a"] + A["gate_c"] + A["q_norm"]
        + [A["kv_store"]] + [A["dot_ag"]],
        A["glu"] + C + af[4:8] + an[4:8],
    ]
    one = jnp.where(t >= 0, 1, 0)
    for region in regions:
        def body(_, carry, region=region):
            for item in region:
                item()
            return carry
        lax.fori_loop(0, one, body, 0)


def _layer(x, norm_w, w_in, q_norm_w, k_norm_w, sinks, dw_w, dw_b, ln_w, ln_b, w_out, rel_bias):
    B, S, D = x.shape
    ts = SEQ_TILE
    assert D == D_MODEL and S % ts == 0 and ts % BLOCK == 0
    n_tiles = B * S // ts

    idx = np.arange(256) // HEAD_DIM
    bd = jnp.asarray((idx[:, None] == idx[None, :]).astype(np.float32), dtype=_BF16)
    qw = (jnp.tile(q_norm_w.astype(_F32), N_HEADS) * (HEAD_DIM ** -0.5)).reshape(1, D_ATTN)
    kw = jnp.tile(k_norm_w.astype(_F32), N_KV_HEADS).reshape(1, 128)
    dww = jnp.concatenate([dw_w.astype(_F32), jnp.zeros((1, D_CONV), _F32)], axis=0)
    xt = x.reshape(n_tiles, ts, D)

    const2 = lambda t: (0, 0)
    resident = pl.Buffered(1)
    in_specs = [
        pl.BlockSpec(memory_space=pltpu.SMEM),
        pl.BlockSpec(memory_space=pltpu.SMEM),
        pl.BlockSpec((None, ts, D), lambda t: (jnp.minimum(t, n_tiles - 1), 0, 0)),
        pl.BlockSpec((None, ts, D), lambda t: (jnp.maximum(t - 2, 0), 0, 0)),
        pl.BlockSpec((1, D), const2),
        pl.BlockSpec((D, D_IN), const2, pipeline_mode=resident),
        pl.BlockSpec((1, D_ATTN), const2),
        pl.BlockSpec((1, 128), const2),
        pl.BlockSpec((256, 256), const2),
        pl.BlockSpec((32, D_CONV), const2),
        pl.BlockSpec((1, D_CONV), const2),
        pl.BlockSpec((1, D_CONV), const2),
        pl.BlockSpec((1, D_CONV), const2),
        pl.BlockSpec((D, D), const2, pipeline_mode=resident),
        pl.BlockSpec((BLOCK, 2 * BLOCK), const2),
    ]
    scratch = [
        pltpu.VMEM((ts, D), _BF16),
        pltpu.VMEM((ts, UA0 - Q0), _F32),
        pltpu.VMEM((ts, ZC0 - UA0), _F32),
        pltpu.VMEM((ts, D_IN - ZC0), _F32),
        pltpu.VMEM((ts, D_ATTN + 128), _F32),
        pltpu.VMEM((2, ts, D_ATTN), _BF16),
        pltpu.VMEM((2, 4, ts + KV_HALO, 128), _BF16),
        pltpu.VMEM((2, 4, ts + KV_HALO, 256), _BF16),
        pltpu.VMEM((2 * N_SLABS, U_ROWS, 128), _F32),
        pltpu.VMEM((2, ts, D_ATTN), _F32),
        pltpu.VMEM((2, ts, D_CONV), _F32),
        pltpu.VMEM((ts, D_CONV), _F32),
        pltpu.VMEM((N_SLABS, C_ROWS, 128), _F32),
        pltpu.VMEM((2, ts, D), _BF16),
        pltpu.VMEM((2, 4, 2 * BLOCK, 2 * BLOCK), _F32),
        pltpu.VMEM((ATTN_SLOTS, 2 * BLOCK, 2 * BLOCK), _F32),
        pltpu.VMEM((ATTN_SLOTS, 2, BLOCK, 128), _F32),
    ]
    out = pl.pallas_call(
        functools.partial(_layer_kernel, tiles_per_seq=S // ts),
        grid=(n_tiles + 2,),
        in_specs=in_specs,
        out_specs=pl.BlockSpec((None, ts, D), lambda t: (jnp.maximum(t - 2, 0), 0, 0)),
        out_shape=jax.ShapeDtypeStruct((n_tiles, ts, D), x.dtype),
        scratch_shapes=scratch,
        compiler_params=pltpu.CompilerParams(
            dimension_semantics=("arbitrary",),
            vmem_limit_bytes=VMEM_LIMIT_BYTES),
        name="hybrid_layer",
    )(sinks.astype(_F32), rel_bias.astype(_F32).reshape(NUM_BUCKETS * N_HEADS), xt, xt,
      norm_w.astype(_F32).reshape(1, D), w_in.astype(_BF16), qw, kw, bd,
      dww, dw_b.astype(_F32).reshape(1, D_CONV), ln_w.astype(_F32).reshape(1, D_CONV),
      ln_b.astype(_F32).reshape(1, D_CONV), w_out.astype(_BF16), jnp.asarray(_banded_buckets()))
    return out.reshape(B, S, D)


def kernel(x, norm_w, w_in, q_norm_w, k_norm_w, sinks, dw_w, dw_b, ln_w, ln_b, w_out, rel_bias):
    depth = norm_w.shape[0]
    for l in range(depth):
        x = _layer(x, norm_w[l], w_in[l], q_norm_w[l], k_norm_w[l], sinks[l], dw_w[l], dw_b[l],
                   ln_w[l], ln_b[l], w_out[l], rel_bias)
    return x
```

```python
import functools
import math

import numpy as np
import jax
import jax.numpy as jnp
from jax import lax
from jax.experimental import pallas as pl
from jax.experimental.pallas import tpu as pltpu

D_MODEL = 1024
D_ATTN = 512
D_CONV = 512
HEAD_DIM = 64
N_HEADS = 8
N_KV_HEADS = 2
BLOCK = 128
NUM_BUCKETS = 32
MAX_DISTANCE = 128
CONV_WIDTH = 31
D_IN = 2816
EPS = 1e-6
LN_EPS = 1e-5
NEG_INF = -1e30

SEQ_TILE = 512
KV_HALO = BLOCK
ROW_CHUNK = 64
SUBLANES = 8
N_SLABS = D_CONV // 128
T_BLK = SEQ_TILE // SUBLANES
U_PITCH = T_BLK + 1
U_ROWS = ((SUBLANES + 1) * U_PITCH + 7) // 8 * 8
C_PITCH = T_BLK + 8
C_ROWS = SUBLANES * C_PITCH
CONV_GROUP = 8
TAP_BLOCK = CONV_WIDTH
N_CHUNK = 256
ATTN_SLOTS = 4
VMEM_LIMIT_BYTES = 56 * 1024 * 1024

Q0, K0, V0, ZA0, UA0, UG0, ZC0 = 0, 512, 640, 768, 1280, 1792, 2304

_F32 = jnp.float32
_BF16 = jnp.bfloat16


def _t5_bucket_table():
    qi = np.arange(BLOCK)[:, None]
    sj = np.arange(2 * BLOCK)[None, :]
    dist = qi + BLOCK - sj
    n = np.maximum(dist, 0)
    max_exact = NUM_BUCKETS // 2
    nf = np.maximum(n, 1).astype(np.float32)
    large = max_exact + (np.log(nf / max_exact) / math.log(MAX_DISTANCE / max_exact)
                         * (NUM_BUCKETS - max_exact)).astype(np.int32)
    large = np.minimum(large, NUM_BUCKETS - 1)
    bucket = np.where(n < max_exact, n, large).astype(np.int32)
    band = (dist >= 0) & (dist < BLOCK)
    return bucket, band


def _banded_buckets():
    bucket, band = _t5_bucket_table()
    return np.where(band, bucket, -1).astype(np.int32)


def _sigmoid(z):
    return 0.5 * jnp.tanh(0.5 * z) + 0.5


def _build_bias_tables(relb_ref, bucket_ref, bias_s):
    cur_keys = lax.broadcasted_iota(jnp.int32, (8, 2 * BLOCK), 1) >= BLOCK

    def body(c, carry):
        r0 = pl.multiple_of(c * 8, 8)
        bk = bucket_ref[pl.ds(r0, 8), :]
        accs = [jnp.full((8, 2 * BLOCK), NEG_INF, _F32) for _ in range(N_HEADS)]
        for b in range(NUM_BUCKETS):
            hit = bk == b
            for h in range(N_HEADS):
                accs[h] = jnp.where(hit, relb_ref[b * N_HEADS + h], accs[h])
        for h in range(N_HEADS):
            g, rem = divmod(h, 4)
            j, e = divmod(rem, 2)
            rows = pl.ds(BLOCK * j + r0, 8)
            bias_s[0, 2 * g + e, rows, :] = accs[h]
            bias_s[1, 2 * g + e, rows, :] = jnp.where(cur_keys, accs[h], NEG_INF)
        return carry

    lax.fori_loop(0, BLOCK // 8, body, 0)


class _RawProj:
    bounds = (Q0, UA0, ZC0, D_IN)

    def __init__(self, *refs):
        self.refs = refs

    def _find(self, c0, c1):
        for ref, lo, hi in zip(self.refs, self.bounds[:-1], self.bounds[1:]):
            if lo <= c0 and c1 <= hi:
                return ref, c0 - lo, c1 - lo
        raise ValueError((c0, c1))

    def load(self, rows, c0, c1):
        ref, a, b = self._find(c0, c1)
        return ref[rows, a:b]

    def store(self, c0, c1, val):
        ref, a, b = self._find(c0, c1)
        ref[:, a:b] = val


def _row_chunks():
    return [slice(r, r + ROW_CHUNK) for r in range(0, SEQ_TILE, ROW_CHUNK)]


def _stream_a(sa, x_ref, nw_ref, win_ref, qw_ref, kw_ref, bd_ref,
              h_s, p_s, ssq_s, q_s, k_s, v_s, u_s, ga_s, gc_s):
    ts = SEQ_TILE

    def rms(r):
        def run():
            xc = x_ref[r:r + ROW_CHUNK, :]
            ss = jnp.sum(xc * xc, axis=-1, keepdims=True)
            rs = lax.rsqrt(ss * (1.0 / D_MODEL) + EPS)
            h_s[r:r + ROW_CHUNK, :] = ((xc * rs) * nw_ref[...]).astype(_BF16)
        return run

    def dot(c0, c1):
        def run():
            p_s.store(c0, c1, jnp.dot(h_s[...], win_ref[:, c0:c1], preferred_element_type=_F32))
        return run

    def q_sumsq(half):
        def run():
            cs = slice(256 * half, 256 * half + 256)
            q = p_s.load(slice(None), Q0 + cs.start, Q0 + cs.stop)
            ssq_s[:, cs] = jnp.dot((q * q).astype(_BF16), bd_ref[...], preferred_element_type=_F32)
        return run

    def q_norm(half):
        def run():
            cs = slice(256 * half, 256 * half + 256)
            for rr in _row_chunks():
                q = p_s.load(rr, Q0 + cs.start, Q0 + cs.stop)
                qn =(q * lax.rsqrt(ssq_s[rr, cs] * (1.0 / HEAD_DIM) + EPS)) * qw_ref[:, cs]
                q_s[sa, rr, cs] = qn.astype(_BF16)
        return run

    def k_sumsq():
        k = p_s.load(slice(None), K0, K0 + 128)
        ssq_s[:, D_ATTN:D_ATTN + 128] = jnp.dot((k * k).astype(_BF16), bd_ref[0:128, 0:128],
                                                preferred_element_type=_F32)

    def kv_store():
        lo = lax.broadcasted_iota(jnp.int32, (ROW_CHUNK, 128), 1) < HEAD_DIM
        zero = jnp.zeros((ROW_CHUNK, 128), _F32)
        one = jnp.ones((ROW_CHUNK, 128), _F32)
        ones_lo = jnp.where(lo, one, zero).astype(_BF16)
        ones_hi = jnp.where(lo, zero, one).astype(_BF16)
        for rr in _row_chunks():
            cur = slice(KV_HALO + rr.start, KV_HALO + rr.stop)
            k = p_s.load(rr, K0, K0 + 128)
            v = p_s.load(rr, V0, V0 + 128)
            kn = (k * lax.rsqrt(ssq_s[rr, D_ATTN:D_ATTN + 128] * (1.0 / HEAD_DIM) + EPS)) * kw_ref[...]
            kn_sw = pltpu.roll(kn, 64, axis=1)
            v_sw = pltpu.roll(v, 64, axis=1)
            k_s[sa, 0, cur, :] = jnp.where(lo, kn, zero).astype(_BF16)
            k_s[sa, 1, cur, :] = jnp.where(lo, zero, kn_sw).astype(_BF16)
            k_s[sa, 2, cur, :] = jnp.where(lo, kn_sw, zero).astype(_BF16)
            k_s[sa, 3, cur, :] = jnp.where(lo, zero, kn).astype(_BF16)
            v_s[sa, 0, cur, 0:128] = jnp.where(lo, v, zero).astype(_BF16)
            v_s[sa, 1, cur, 0:128] = jnp.where(lo, zero, v_sw).astype(_BF16)
            v_s[sa, 2, cur, 0:128] = jnp.where(lo, v_sw, zero).astype(_BF16)
            v_s[sa, 3, cur, 0:128] = jnp.where(lo, zero, v).astype(_BF16)
            v_s[sa, 0, cur, 128:256] = ones_lo
            v_s[sa, 1, cur, 128:256] = ones_hi
            v_s[sa, 2, cur, 128:256] = ones_lo
            v_s[sa, 3, cur, 128:256] = ones_hi

    def gate(c0, dst, half):
        def run():
            cs = slice(256 * half, 256 * half + 256)
            for rr in _row_chunks():
                z = p_s.load(rr, c0 + 256 * half, c0 + 256 * half + 256)
                dst[sa, rr, cs] = z * _sigmoid(z)
        return run

    def glu(half):
        def run():
            for blk in range(SUBLANES):
                tr = slice(T_BLK * blk, T_BLK * (blk + 1))
                a = p_s.load(tr, UA0 + 256 * half, UA0 + 256 * half + 256)
                g = p_s.load(tr, UG0 + 256 * half, UG0 + 256 * half + 256)
                u = a * _sigmoid(g)
                p0 = U_PITCH * (blk + 1)
                for sl in range(2):
                    u_s[sa * N_SLABS + 2 * half + sl, p0:p0 + T_BLK, :] = u[:, 128 * sl:128 * sl + 128]
        return run

    return dict(
        rms=[rms(r) for r in range(0, ts, ROW_CHUNK)],
        dot_qkvz=dot(Q0, UA0), dot_ag=dot(UA0, ZC0), dot_zc=dot(ZC0, D_IN),
        q_sumsq=[q_sumsq(0), q_sumsq(1)], q_norm=[q_norm(0), q_norm(1)],
        k_sumsq=k_sumsq, kv_store=kv_store,
        gate_a=[gate(ZA0, ga_s, 0), gate(ZA0, ga_s, 1)],
        gate_c=[gate(ZC0, gc_s, 0), gate(ZC0, gc_s, 1)],
        glu=[glu(0), glu(1)])


def _stream_b(sb, first_b, sink_ref, dww_ref, dwb_ref, lnw_ref, lnb_ref,
              q_s, k_s, v_s, u_s, ga_s, gc_s, c_s, yc_s, y_s, bias_s, o_s, dn_s):
    ts = SEQ_TILE
    taps = CONV_WIDTH - 1

    def conv(r0, cb):
        def run():
            ls = slice(128 * cb, 128 * cb + 128)
            rows = [slice(SUBLANES * (r0 + n), SUBLANES * (r0 + n + 1)) for n in range(CONV_GROUP)]
            for j0 in range(0, CONV_WIDTH, TAP_BLOCK):
                j1 = min(j0 + TAP_BLOCK, CONV_WIDTH)
                wv = {j: jnp.broadcast_to(dww_ref[j:j + 1, ls], (SUBLANES, 128)) for j in range(j0, j1)}
                if j0 == 0:
                    accs = [jnp.broadcast_to(dwb_ref[:, ls], (SUBLANES, 128))] * CONV_GROUP
                else:
                    accs = [c_s[rows[n], ls] for n in range(CONV_GROUP)]
                for tau in range(r0 - taps + j0, r0 + CONV_GROUP - 1 - taps + j1):
                    base = U_PITCH + tau if tau >= 0 else T_BLK + tau
                    win = u_s[sb * N_SLABS + cb, pl.ds(base, SUBLANES, stride=U_PITCH), :]
                    for n in range(CONV_GROUP):
                        j = tau - (r0 + n) + taps
                        if j0 <= j < j1:
                            accs[n] = accs[n] + wv[j] * win
                for n in range(CONV_GROUP):
                    c_s[rows[n], ls] = accs[n]
        return run

    def ln(r):
        def run():
            y = c_s[r:r + ROW_CHUNK, :]
            mu = jnp.sum(y, axis=-1, keepdims=True) * (1.0 / D_CONV)
            d = y - mu
            var = jnp.sum(d * d, axis=-1, keepdims=True) * (1.0 / D_CONV)
            yn = (d * lax.rsqrt(var + LN_EPS)) * lnw_ref[...] + lnb_ref[...]
            act = yn * _sigmoid(yn)
            for n in range(ROW_CHUNK // SUBLANES):
                sv = r // SUBLANES + n
                for cb in range(N_SLABS):
                    yc_s[cb, pl.ds(sv, SUBLANES, stride=C_PITCH), :] = (
                        act[SUBLANES * n:SUBLANES * (n + 1), 128 * cb:128 * cb + 128])
        return run

    def regate(blk):
        def run():
            tr = slice(T_BLK * blk, T_BLK * (blk + 1))
            for cb in range(N_SLABS):
                ls = slice(128 * cb, 128 * cb + 128)
                val = yc_s[cb, C_PITCH * blk:C_PITCH * blk + T_BLK, :] * gc_s[sb, tr, ls]
                y_s[sb, tr, D_ATTN + 128 * cb:D_ATTN + 128 * cb + 128] = val.astype(_BF16)
        return run

    def attn_front(b, g):
        def run():
            slot = (2 * b + g) % ATTN_SLOTS
            rows = slice(BLOCK * b, BLOCK * b + BLOCK)
            krows = slice(BLOCK * b, BLOCK * b + 2 * BLOCK)
            tbl = first_b if b == 0 else 0
            lo_q = lax.broadcasted_iota(jnp.int32, (BLOCK, 128), 1) < HEAD_DIM
            c0 = 256 * g
            qg = jnp.concatenate([q_s[sb, rows, c0:c0 + 128], q_s[sb, rows, c0 + 128:c0 + 256]], axis=0)
            kk = jnp.concatenate([k_s[sb, 2 * g, krows, :], k_s[sb, 2 * g + 1, krows, :]], axis=0)
            s = lax.dot_general(qg, kk, (((1,), (1,)), ((), ())),
                                preferred_element_type=_F32)
            ps = [[None, None], [None, None]]
            ex = [[None, None], [None, None]]
            for e in range(2):
                se = s[:, 2 * BLOCK * e:2 * BLOCK * (e + 1)] + bias_s[tbl, 2 * g + e]
                for j in range(2):
                    sink = sink_ref[4 * g + 2 * j + e]
                    sj = se[BLOCK * j:BLOCK * j + BLOCK, :]
                    m = jnp.maximum(jnp.max(sj, axis=-1, keepdims=True), sink)
                    ps[j][e] = jnp.exp(sj - m).astype(_BF16)
                    ex[j][e] = jnp.exp(sink - m)
            p = jnp.concatenate([jnp.concatenate(ps[0], axis=1), jnp.concatenate(ps[1], axis=1)], axis=0)
            vv = jnp.concatenate([v_s[sb, 2 * g, krows, :], v_s[sb, 2 * g + 1, krows, :]], axis=0)
            o_s[slot] = jnp.dot(p, vv, preferred_element_type=_F32)
            for j in range(2):
                dn_s[slot, j] = jnp.where(lo_q, ex[j][0], ex[j][1])
        return run

    def attn_norm(b, g):
        def run():
            slot = (2 * b + g) % ATTN_SLOTS
            rows = slice(BLOCK * b, BLOCK * b + BLOCK)
            for j in range(2):
                oj = o_s[slot, BLOCK * j:BLOCK * j + BLOCK, :]
                den = oj[:, 128:256] + dn_s[slot, j]
                cs = slice(256 * g + 128 * j, 256 * g + 128 * j + 128)
                y_s[sb, rows, cs] = ((oj[:, 0:128] / den) * ga_s[sb, rows, cs]).astype(_BF16)
        return run

    pairs = [(b, g) for b in range(ts // BLOCK) for g in range(N_KV_HEADS)]
    return dict(attn_front=[attn_front(b, g) for b, g in pairs],
                attn_norm=[attn_norm(b, g) for b, g in pairs],
                conv=[conv(r0, cb) for r0 in range(0, T_BLK, CONV_GROUP) for cb in range(N_SLABS)],
                ln=[ln(r) for r in range(0, ts, ROW_CHUNK)],
                regate=[regate(blk) for blk in range(SUBLANES)])


def _stream_c(sy, xres_ref, wout_ref, o_ref, y_s):
    def outproj(n):
        def run():
            o_ref[:, n:n + N_CHUNK] = xres_ref[:, n:n + N_CHUNK] + jnp.dot(
                y_s[sy], wout_ref[:, n:n + N_CHUNK], preferred_element_type=_F32)
        return run

    return [outproj(n) for n in range(0, D_MODEL, N_CHUNK)]


def _layer_kernel(sink_ref, relb_ref, x_ref, xres_ref, nw_ref, win_ref, qw_ref, kw_ref, bd_ref, dww_ref,
                  dwb_ref, lnw_ref, lnb_ref, wout_ref, bucket_ref,
                  o_ref,
                  h_s, p1_s, p2_s, p3_s, ssq_s, q_s, k_s, v_s, u_s, ga_s, gc_s, c_s, yc_s, y_s, bias_s,
                  o_s, dn_s, *, tiles_per_seq, n_tiles):
    p_s = _RawProj(p1_s, p2_s, p3_s)
    ts = SEQ_TILE
    t = pl.program_id(0)
    sa = lax.rem(t, 2)
    sb = 1 - sa
    first_a = lax.rem(t, tiles_per_seq) == 0
    first_b = jnp.where(lax.rem(t + tiles_per_seq - 1, tiles_per_seq) == 0, 1, 0)

    @pl.when(t == 0)
    def _():
        _build_bias_tables(relb_ref, bucket_ref, bias_s)
        y_s[1] = jnp.zeros(y_s.shape[1:], _BF16)

    @pl.when(first_a)
    def _():
        k_s[sa, :, 0:KV_HALO, :] = jnp.zeros((4, KV_HALO, 128), _BF16)
        v_s[sa, :, 0:KV_HALO, :] = jnp.zeros((4, KV_HALO, 256), _BF16)
        for cb in range(N_SLABS):
            u_s[sa * N_SLABS + cb, 0:T_BLK, :] = jnp.zeros((T_BLK, 128), _F32)

    @pl.when(jnp.logical_not(first_a))
    def _():
        k_s[sa, :, 0:KV_HALO, :] = k_s[sb, :, ts:ts + KV_HALO, :]
        v_s[sa, :, 0:KV_HALO, :] = v_s[sb, :, ts:ts + KV_HALO, :]
        for cb in range(N_SLABS):
            u_s[sa * N_SLABS + cb, 0:T_BLK, :] = u_s[sb * N_SLABS + cb,
                                                     SUBLANES * U_PITCH:SUBLANES * U_PITCH + T_BLK, :]

    A = _stream_a(sa, x_ref, nw_ref, win_ref, qw_ref, kw_ref, bd_ref,
                  h_s, p_s, ssq_s, q_s, k_s, v_s, u_s, ga_s, gc_s)
    Bm = _stream_b(sb, first_b, sink_ref, dww_ref, dwb_ref, lnw_ref, lnb_ref,
                   q_s, k_s, v_s, u_s, ga_s, gc_s, c_s, yc_s, y_s, bias_s, o_s, dn_s)
    C = _stream_c(sa, xres_ref, wout_ref, o_ref, y_s)
    cv, af, an = Bm["conv"], Bm["attn_front"], Bm["attn_norm"]

    regions = [
        A["rms"] + cv,
        Bm["ln"] + Bm["regate"] + af[0:4] + an[0:4] + [A["dot_qkvz"], A["dot_zc"]],
        A["q_sumsq"] + [A["k_sumsq"]] + A["gate_a"] + A["gate_c"] + A["q_norm"]
        + [A["kv_store"]] + [A["dot_ag"]],
        A["glu"] + C + af[4:8] + an[4:8],
    ]
    fill_regions = [
        A["rms"],
        [A["dot_qkvz"], A["dot_zc"]],
        A["q_sumsq"] + [A["k_sumsq"]] + A["gate_a"] + A["gate_c"] + A["q_norm"]
        + [A["kv_store"]] + [A["dot_ag"]],
        A["glu"],
    ]
    drain_regions = [C]
    schedule = [(regions, (t >= 1) & (t <= n_tiles)),
                (fill_regions, t == 0),
                (drain_regions, t == n_tiles + 1)]
    for group, active in schedule:
        trips = jnp.where(active, 1, 0)
        for region in group:
            def body(_, carry, region=region):
                for item in region:
                    item()
                return carry
            lax.fori_loop(0, trips, body, 0)


def _layer(x, norm_w, w_in, q_norm_w, k_norm_w, sinks, dw_w, dw_b, ln_w, ln_b, w_out, rel_bias):
    B, S, D = x.shape
    ts = SEQ_TILE
    assert D == D_MODEL and S % ts == 0 and ts % BLOCK == 0
    n_tiles = B * S // ts

    idx = np.arange(256) // HEAD_DIM
    bd = jnp.asarray((idx[:, None] == idx[None, :]).astype(np.float32), dtype=_BF16)
    qw = (jnp.tile(q_norm_w.astype(_F32), N_HEADS) * (HEAD_DIM ** -0.5)).reshape(1, D_ATTN)
    kw = jnp.tile(k_norm_w.astype(_F32), N_KV_HEADS).reshape(1, 128)
    xt = x.reshape(n_tiles, ts, D)

    const2 = lambda t: (0, 0)
    resident = pl.Buffered(1)
    in_specs = [
        pl.BlockSpec(memory_space=pltpu.SMEM),
        pl.BlockSpec(memory_space=pltpu.SMEM),
        pl.BlockSpec((None, ts, D), lambda t: (jnp.minimum(t, n_tiles - 1), 0, 0)),
        pl.BlockSpec((None, ts, D), lambda t: (jnp.maximum(t - 2, 0), 0, 0)),
        pl.BlockSpec((1, D), const2),
        pl.BlockSpec((D, D_IN), const2, pipeline_mode=resident),
        pl.BlockSpec((1, D_ATTN), const2),
        pl.BlockSpec((1, 128), const2),
        pl.BlockSpec((256, 256), const2),
        pl.BlockSpec((CONV_WIDTH, D_CONV), const2),
        pl.BlockSpec((1, D_CONV), const2),
        pl.BlockSpec((1, D_CONV), const2),
        pl.BlockSpec((1, D_CONV), const2),
        pl.BlockSpec((D, D), const2, pipeline_mode=resident),
        pl.BlockSpec((BLOCK, 2 * BLOCK), const2),
    ]
    scratch = [
        pltpu.VMEM((ts, D), _BF16),
        pltpu.VMEM((ts, UA0 - Q0), _F32),
        pltpu.VMEM((ts, ZC0 - UA0), _F32),
        pltpu.VMEM((ts, D_IN - ZC0), _F32),
        pltpu.VMEM((ts, D_ATTN + 128), _F32),
        pltpu.VMEM((2, ts, D_ATTN), _BF16),
        pltpu.VMEM((2, 4, ts + KV_HALO, 128), _BF16),
        pltpu.VMEM((2, 4, ts + KV_HALO, 256), _BF16),
        pltpu.VMEM((2 * N_SLABS, U_ROWS, 128), _F32),
        pltpu.VMEM((2, ts, D_ATTN), _F32),
        pltpu.VMEM((2, ts, D_CONV), _F32),
        pltpu.VMEM((ts, D_CONV), _F32),
        pltpu.VMEM((N_SLABS, C_ROWS, 128), _F32),
        pltpu.VMEM((2, ts, D), _BF16),
        pltpu.VMEM((2, 4, 2 * BLOCK, 2 * BLOCK), _F32),
        pltpu.VMEM((ATTN_SLOTS, 2 * BLOCK, 2 * BLOCK), _F32),
        pltpu.VMEM((ATTN_SLOTS, 2, BLOCK, 128), _F32),
    ]
    out = pl.pallas_call(
        functools.partial(_layer_kernel, tiles_per_seq=S // ts, n_tiles=n_tiles),
        grid=(n_tiles + 2,),
        in_specs=in_specs,
        out_specs=pl.BlockSpec((None, ts, D), lambda t: (jnp.maximum(t - 2, 0), 0, 0)),
        out_shape=jax.ShapeDtypeStruct((n_tiles, ts, D), x.dtype),
        scratch_shapes=scratch,
        compiler_params=pltpu.CompilerParams(
            dimension_semantics=("arbitrary",),
            vmem_limit_bytes=VMEM_LIMIT_BYTES),
        name="hybrid_layer",
    )(sinks.astype(_F32), rel_bias.astype(_F32).reshape(NUM_BUCKETS * N_HEADS), xt, xt,
      norm_w.astype(_F32).reshape(1, D), w_in.astype(_BF16), qw, kw, bd,
      dw_w.astype(_F32), dw_b.astype(_F32).reshape(1, D_CONV), ln_w.astype(_F32).reshape(1, D_CONV),
      ln_b.astype(_F32).reshape(1, D_CONV), w_out.astype(_BF16), jnp.asarray(_banded_buckets()))
    return out.reshape(B, S, D)


def kernel(x, norm_w, w_in, q_norm_w, k_norm_w, sinks, dw_w, dw_b, ln_w, ln_b, w_out, rel_bias):
    depth = norm_w.shape[0]
    for l in range(depth):
        x = _layer(x, norm_w[l], w_in[l], q_norm_w[l], k_norm_w[l], sinks[l], dw_w[l], dw_b[l],
                   ln_w[l], ln_b[l], w_out[l], rel_bias)
    return x
```

```python
import functools
import math

import numpy as np
import jax
import jax.numpy as jnp
from jax import lax
from jax.experimental import pallas as pl
from jax.experimental.pallas import tpu as pltpu

D_MODEL = 1024
D_ATTN = 512
D_CONV = 512
HEAD_DIM = 64
N_HEADS = 8
N_KV_HEADS = 2
BLOCK = 128
NUM_BUCKETS = 32
MAX_DISTANCE = 128
CONV_WIDTH = 31
D_IN = 2816
EPS = 1e-6
LN_EPS = 1e-5
NEG_INF = -1e30

SEQ_TILE = 512
KV_HALO = BLOCK
ROW_CHUNK = 64
SUBLANES = 8
N_SLABS = D_CONV // 128
T_BLK = SEQ_TILE // SUBLANES
U_PITCH = T_BLK + 1
U_ROWS = ((SUBLANES + 1) * U_PITCH + 7) // 8 * 8
C_PITCH = T_BLK + 8
C_ROWS = SUBLANES * C_PITCH
CONV_GROUP = 8
TAP_BLOCK = CONV_WIDTH
N_CHUNK = 256
ATTN_SLOTS = 8
VMEM_LIMIT_BYTES = 56 * 1024 * 1024

Q0, K0, V0, ZA0, UA0, UG0, ZC0 = 0, 512, 640, 768, 1280, 1792, 2304

_F32 = jnp.float32
_BF16 = jnp.bfloat16


def _t5_bucket_table():
    qi = np.arange(BLOCK)[:, None]
    sj = np.arange(2 * BLOCK)[None, :]
    dist = qi + BLOCK - sj
    n = np.maximum(dist, 0)
    max_exact = NUM_BUCKETS // 2
    nf = np.maximum(n, 1).astype(np.float32)
    large = max_exact + (np.log(nf / max_exact) / math.log(MAX_DISTANCE / max_exact)
                         * (NUM_BUCKETS - max_exact)).astype(np.int32)
    large = np.minimum(large, NUM_BUCKETS - 1)
    bucket = np.where(n < max_exact, n, large).astype(np.int32)
    band = (dist >= 0) & (dist < BLOCK)
    return bucket, band


def _banded_buckets():
    bucket, band = _t5_bucket_table()
    return np.where(band, bucket, -1).astype(np.int32)


def _sigmoid(z):
    return 0.5 * jnp.tanh(0.5 * z) + 0.5


def _silu(z):
    h = 0.5 * z
    return h + h * jnp.tanh(h)


def _build_bias_tables(relb_ref, bucket_ref, bias_s):
    cur_keys = lax.broadcasted_iota(jnp.int32, (8, 2 * BLOCK), 1) >= BLOCK

    def body(c, carry):
        r0 = pl.multiple_of(c * 8, 8)
        bk = bucket_ref[pl.ds(r0, 8), :]
        accs = [jnp.full((8, 2 * BLOCK), NEG_INF, _F32) for _ in range(N_HEADS)]
        for b in range(NUM_BUCKETS):
            hit = bk == b
            for h in range(N_HEADS):
                accs[h] = jnp.where(hit, relb_ref[b * N_HEADS + h], accs[h])
        for h in range(N_HEADS):
            g, rem = divmod(h, 4)
            j, e = divmod(rem, 2)
            rows = pl.ds(BLOCK * j + r0, 8)
            bias_s[0, 2 * g + e, rows, :] = accs[h]
            bias_s[1, 2 * g + e, rows, :] = jnp.where(cur_keys, accs[h], NEG_INF)
        return carry

    lax.fori_loop(0, BLOCK // 8, body, 0)


class _RawProj:
    bounds = (Q0, UA0, ZC0, D_IN)

    def __init__(self, *refs):
        self.refs = refs

    def _find(self, c0, c1):
        for ref, lo, hi in zip(self.refs, self.bounds[:-1], self.bounds[1:]):
            if lo <= c0 and c1 <= hi:
                return ref, c0 - lo, c1 - lo
        raise ValueError((c0, c1))

    def load(self, rows, c0, c1):
        ref, a, b = self._find(c0, c1)
        return ref[rows, a:b]

    def store(self, c0, c1, val):
        ref, a, b = self._find(c0, c1)
        ref[:, a:b] = val


def _row_chunks():
    return [slice(r, r + ROW_CHUNK) for r in range(0, SEQ_TILE, ROW_CHUNK)]


def _stream_a(sa, x_ref, x0_ref, nw_ref, win_ref, qw_ref, kw_ref, bd_ref,
              h_s, p_s, ssq_s, q_s, k_s, v_s, u_s, ga_s, gc_s):
    ts = SEQ_TILE

    def rms(src_ref, r):
        def run():
            xc = src_ref[r:r + ROW_CHUNK, :]
            ss = jnp.sum(xc * xc, axis=-1, keepdims=True)
            rs = lax.rsqrt(ss * (1.0 / D_MODEL) + EPS)
            h_s[r:r + ROW_CHUNK, :] = ((xc * rs) * nw_ref[...]).astype(_BF16)
        return run

    def dot(c0, c1):
        def run():
            p_s.store(c0, c1, jnp.dot(h_s[...], win_ref[:, c0:c1], preferred_element_type=_F32))
        return run

    def q_sumsq(half):
        def run():
            cs = slice(256 * half, 256 * half + 256)
            q = p_s.load(slice(None), Q0 + cs.start, Q0 + cs.stop)
            ssq_s[:, cs] = jnp.dot((q * q).astype(_BF16), bd_ref[...], preferred_element_type=_F32)
        return run

    def q_norm(half):
        def run():
            cs = slice(256 * half, 256 * half + 256)
            for rr in _row_chunks():
                q = p_s.load(rr, Q0 + cs.start, Q0 + cs.stop)
                qn =(q * lax.rsqrt(ssq_s[rr, cs] * (1.0 / HEAD_DIM) + EPS)) * qw_ref[:, cs]
                q_s[sa, rr, cs] = qn.astype(_BF16)
        return run

    def k_sumsq():
        k = p_s.load(slice(None), K0, K0 + 128)
        ssq_s[:, D_ATTN:D_ATTN + 128] = jnp.dot((k * k).astype(_BF16), bd_ref[0:128, 0:128],
                                                preferred_element_type=_F32)

    def kv_store():
        lo = lax.broadcasted_iota(jnp.int32, (ROW_CHUNK, 128), 1) < HEAD_DIM
        zero = jnp.zeros((ROW_CHUNK, 128), _F32)
        one = jnp.ones((ROW_CHUNK, 128), _F32)
        ones_lo = jnp.where(lo, one, zero).astype(_BF16)
        ones_hi = jnp.where(lo, zero, one).astype(_BF16)
        for rr in _row_chunks():
            cur = slice(KV_HALO + rr.start, KV_HALO + rr.stop)
            k = p_s.load(rr, K0, K0 + 128)
            v = p_s.load(rr, V0, V0 + 128)
            kn = (k * lax.rsqrt(ssq_s[rr, D_ATTN:D_ATTN + 128] * (1.0 / HEAD_DIM) + EPS)) * kw_ref[...]
            kn_sw = pltpu.roll(kn, 64, axis=1)
            v_sw = pltpu.roll(v, 64, axis=1)
            k_s[sa, 0, cur, :] = jnp.where(lo, kn, zero).astype(_BF16)
            k_s[sa, 1, cur, :] = jnp.where(lo, zero, kn_sw).astype(_BF16)
            k_s[sa, 2, cur, :] = jnp.where(lo, kn_sw, zero).astype(_BF16)
            k_s[sa, 3, cur, :] = jnp.where(lo, zero, kn).astype(_BF16)
            v_s[sa, 0, cur, 0:128] = jnp.where(lo, v, zero).astype(_BF16)
            v_s[sa, 1, cur, 0:128] = jnp.where(lo, zero, v_sw).astype(_BF16)
            v_s[sa, 2, cur, 0:128] = jnp.where(lo, v_sw, zero).astype(_BF16)
            v_s[sa, 3, cur, 0:128] = jnp.where(lo, zero, v).astype(_BF16)
            v_s[sa, 0, cur, 128:256] = ones_lo
            v_s[sa, 1, cur, 128:256] = ones_hi
            v_s[sa, 2, cur, 128:256] = ones_lo
            v_s[sa, 3, cur, 128:256] = ones_hi

    def gate(c0, dst, half):
        def run():
            cs = slice(256 * half, 256 * half + 256)
            for rr in _row_chunks():
                z = p_s.load(rr, c0 + 256 * half, c0 + 256 * half + 256)
                dst[sa, rr, cs] = _silu(z)
        return run

    def glu(half):
        def run():
            for blk in range(SUBLANES):
                tr = slice(T_BLK * blk, T_BLK * (blk + 1))
                a = p_s.load(tr, UA0 + 256 * half, UA0 + 256 * half + 256)
                g = p_s.load(tr, UG0 + 256 * half, UG0 + 256 * half + 256)
                u = a * _sigmoid(g)
                p0 = U_PITCH * (blk + 1)
                for sl in range(2):
                    u_s[sa * N_SLABS + 2 * half + sl, p0:p0 + T_BLK, :] = u[:, 128 * sl:128 * sl + 128]
        return run

    return dict(
        rms_next=[rms(x_ref, r) for r in range(0, ts, ROW_CHUNK)],
        rms_first=[rms(x0_ref, r) for r in range(0, ts, ROW_CHUNK)],
        dot_qkvz=dot(Q0, UA0), dot_ag=dot(UA0, ZC0), dot_zc=dot(ZC0, D_IN),
        q_sumsq=[q_sumsq(0), q_sumsq(1)], q_norm=[q_norm(0), q_norm(1)],
        k_sumsq=k_sumsq, kv_store=kv_store,
        gate_a=[gate(ZA0, ga_s, 0), gate(ZA0, ga_s, 1)],
        gate_c=[gate(ZC0, gc_s, 0), gate(ZC0, gc_s, 1)],
        glu=[glu(0), glu(1)])


def _stream_b(sb, first_b, sink_ref, dww_ref, dwb_ref, lnw_ref, lnb_ref,
              q_s, k_s, v_s, u_s, ga_s, gc_s, c_s, yc_s, y_s, bias_s, o_s, dn_s):
    ts = SEQ_TILE
    taps = CONV_WIDTH - 1

    def conv(r0, cb):
        def run():
            ls = slice(128 * cb, 128 * cb + 128)
            rows = [slice(SUBLANES * (r0 + n), SUBLANES * (r0 + n + 1)) for n in range(CONV_GROUP)]
            for j0 in range(0, CONV_WIDTH, TAP_BLOCK):
                j1 = min(j0 + TAP_BLOCK, CONV_WIDTH)
                wv = {j: jnp.broadcast_to(dww_ref[j:j + 1, ls], (SUBLANES, 128)) for j in range(j0, j1)}
                if j0 == 0:
                    accs = [jnp.broadcast_to(dwb_ref[:, ls], (SUBLANES, 128))] * CONV_GROUP
                else:
                    accs = [c_s[rows[n], ls] for n in range(CONV_GROUP)]
                for tau in range(r0 - taps + j0, r0 + CONV_GROUP - 1 - taps + j1):
                    base = U_PITCH + tau if tau >= 0 else T_BLK + tau
                    win = u_s[sb * N_SLABS + cb, pl.ds(base, SUBLANES, stride=U_PITCH), :]
                    for n in range(CONV_GROUP):
                        j = tau - (r0 + n) + taps
                        if j0 <= j < j1:
                            accs[n] = accs[n] + wv[j] * win
                for n in range(CONV_GROUP):
                    c_s[rows[n], ls] = accs[n]
        return run

    def ln(r):
        def run():
            y = c_s[r:r + ROW_CHUNK, :]
            mu = jnp.sum(y, axis=-1, keepdims=True) * (1.0 / D_CONV)
            d = y - mu
            var = jnp.sum(d * d, axis=-1, keepdims=True) * (1.0 / D_CONV)
            yn = (d * lax.rsqrt(var + LN_EPS)) * lnw_ref[...] + lnb_ref[...]
            act = _silu(yn)
            for n in range(ROW_CHUNK // SUBLANES):
                sv = r // SUBLANES + n
                for cb in range(N_SLABS):
                    yc_s[cb, pl.ds(sv, SUBLANES, stride=C_PITCH), :] = (
                        act[SUBLANES * n:SUBLANES * (n + 1), 128 * cb:128 * cb + 128])
        return run

    def regate(blk):
        def run():
            tr = slice(T_BLK * blk, T_BLK * (blk + 1))
            for cb in range(N_SLABS):
                ls = slice(128 * cb, 128 * cb + 128)
                val = yc_s[cb, C_PITCH * blk:C_PITCH * blk + T_BLK, :] * gc_s[sb, tr, ls]
                y_s[sb, tr, D_ATTN + 128 * cb:D_ATTN + 128 * cb + 128] = val.astype(_BF16)
        return run

    def attn_front(b, g):
        def run():
            slot = (2 * b + g) % ATTN_SLOTS
            rows = slice(BLOCK * b, BLOCK * b + BLOCK)
            krows = slice(BLOCK * b, BLOCK * b + 2 * BLOCK)
            tbl = first_b if b == 0 else 0
            lo_q = lax.broadcasted_iota(jnp.int32, (BLOCK, 128), 1) < HEAD_DIM
            c0 = 256 * g
            qg = jnp.concatenate([q_s[sb, rows, c0:c0 + 128], q_s[sb, rows, c0 + 128:c0 + 256]], axis=0)
            kk = jnp.concatenate([k_s[sb, 2 * g, krows, :], k_s[sb, 2 * g + 1, krows, :]], axis=0)
            s = lax.dot_general(qg, kk, (((1,), (1,)), ((), ())),
                                preferred_element_type=_F32)
            ps = [[None, None], [None, None]]
            ex = [[None, None], [None, None]]
            for e in range(2):
                se = s[:, 2 * BLOCK * e:2 * BLOCK * (e + 1)] + bias_s[tbl, 2 * g + e]
                for j in range(2):
                    sink = sink_ref[4 * g + 2 * j + e]
                    sj = se[BLOCK * j:BLOCK * j + BLOCK, :]
                    m = jnp.maximum(jnp.max(sj, axis=-1, keepdims=True), sink)
                    ps[j][e] = jnp.exp(sj - m).astype(_BF16)
                    ex[j][e] = jnp.exp(sink - m)
            p = jnp.concatenate([jnp.concatenate(ps[0], axis=1), jnp.concatenate(ps[1], axis=1)], axis=0)
            vv = jnp.concatenate([v_s[sb, 2 * g, krows, :], v_s[sb, 2 * g + 1, krows, :]], axis=0)
            o_s[slot] = jnp.dot(p, vv, preferred_element_type=_F32)
            for j in range(2):
                dn_s[slot, j] = jnp.where(lo_q, ex[j][0], ex[j][1])
        return run

    def attn_norm(b, g):
        def run():
            slot = (2 * b + g) % ATTN_SLOTS
            rows = slice(BLOCK * b, BLOCK * b + BLOCK)
            for j in range(2):
                oj = o_s[slot, BLOCK * j:BLOCK * j + BLOCK, :]
                den = oj[:, 128:256] + dn_s[slot, j]
                cs = slice(256 * g + 128 * j, 256 * g + 128 * j + 128)
                y_s[sb, rows, cs] = ((oj[:, 0:128] / den) * ga_s[sb, rows, cs]).astype(_BF16)
        return run

    pairs = [(b, g) for b in range(ts // BLOCK) for g in range(N_KV_HEADS)]
    return dict(attn_front=[attn_front(b, g) for b, g in pairs],
                attn_norm=[attn_norm(b, g) for b, g in pairs],
                conv=[conv(r0, cb) for r0 in range(0, T_BLK, CONV_GROUP) for cb in range(N_SLABS)],
                ln=[ln(r) for r in range(0, ts, ROW_CHUNK)],
                regate=[regate(blk) for blk in range(SUBLANES)])


def _stream_c(sy, xres_ref, wout_ref, o_ref, y_s):
    def outproj(n):
        def run():
            o_ref[:, n:n + N_CHUNK] = xres_ref[:, n:n + N_CHUNK] + jnp.dot(
                y_s[sy], wout_ref[:, n:n + N_CHUNK], preferred_element_type=_F32)
        return run

    return [outproj(n) for n in range(0, D_MODEL, N_CHUNK)]


def _layer_kernel(sink_ref, relb_ref, x_ref, xres_ref, nw_ref, win_ref, qw_ref, kw_ref, bd_ref, dww_ref,
                  dwb_ref, lnw_ref, lnb_ref, wout_ref, bucket_ref,
                  o_ref,
                  h_s, p1_s, p2_s, p3_s, ssq_s, q_s, k_s, v_s, u_s, ga_s, gc_s, c_s, yc_s, y_s, bias_s,
                  o_s, dn_s, *, tiles_per_seq, n_tiles):
    p_s = _RawProj(p1_s, p2_s, p3_s)
    ts = SEQ_TILE
    t = pl.program_id(0)
    sa = lax.rem(t, 2)
    sb = 1 - sa
    first_a = lax.rem(t, tiles_per_seq) == 0
    first_b = jnp.where(lax.rem(t + tiles_per_seq - 1, tiles_per_seq) == 0, 1, 0)

    @pl.when(t == 0)
    def _():
        _build_bias_tables(relb_ref, bucket_ref, bias_s)
        y_s[1] = jnp.zeros(y_s.shape[1:], _BF16)

    @pl.when(first_a)
    def _():
        k_s[sa, :, 0:KV_HALO, :] = jnp.zeros((4, KV_HALO, 128), _BF16)
        v_s[sa, :, 0:KV_HALO, :] = jnp.zeros((4, KV_HALO, 256), _BF16)
        for cb in range(N_SLABS):
            u_s[sa * N_SLABS + cb, 0:T_BLK, :] = jnp.zeros((T_BLK, 128), _F32)

    @pl.when(jnp.logical_not(first_a))
    def _():
        k_s[sa, :, 0:KV_HALO, :] = k_s[sb, :, ts:ts + KV_HALO, :]
        v_s[sa, :, 0:KV_HALO, :] = v_s[sb, :, ts:ts + KV_HALO, :]
        for cb in range(N_SLABS):
            u_s[sa * N_SLABS + cb, 0:T_BLK, :] = u_s[sb * N_SLABS + cb,
                                                     SUBLANES * U_PITCH:SUBLANES * U_PITCH + T_BLK, :]

    A = _stream_a(sa, x_ref, xres_ref, nw_ref, win_ref, qw_ref, kw_ref, bd_ref,
                  h_s, p_s, ssq_s, q_s, k_s, v_s, u_s, ga_s, gc_s)
    Bm = _stream_b(sb, first_b, sink_ref, dww_ref, dwb_ref, lnw_ref, lnb_ref,
                   q_s, k_s, v_s, u_s, ga_s, gc_s, c_s, yc_s, y_s, bias_s, o_s, dn_s)
    C = _stream_c(sa, xres_ref, wout_ref, o_ref, y_s)
    cv, af, an = Bm["conv"], Bm["attn_front"], Bm["attn_norm"]

    regions = [
        cv,
        Bm["ln"] + Bm["regate"] + af + an + [A["dot_qkvz"], A["dot_zc"]],
        A["q_sumsq"] + [A["k_sumsq"]] + A["gate_a"] + A["gate_c"] + A["q_norm"]
        + [A["kv_store"]] + [A["dot_ag"]],
        A["glu"] + C + A["rms_next"],
    ]
    fill_regions = [
        A["rms_first"],
        [A["dot_qkvz"], A["dot_zc"]],
        A["q_sumsq"] + [A["k_sumsq"]] + A["gate_a"] + A["gate_c"] + A["q_norm"]
        + [A["kv_store"]] + [A["dot_ag"]],
        A["glu"] + A["rms_next"],
    ]
    drain_regions = [C]
    schedule = [(regions, (t >= 1) & (t <= n_tiles)),
                (fill_regions, t == 0),
                (drain_regions, t == n_tiles + 1)]
    for group, active in schedule:
        trips = jnp.where(active, 1, 0)
        for region in group:
            def body(_, carry, region=region):
                for item in region:
                    item()
                return carry
            lax.fori_loop(0, trips, body, 0)


def _layer(x, norm_w, w_in, q_norm_w, k_norm_w, sinks, dw_w, dw_b, ln_w, ln_b, w_out, rel_bias):
    B, S, D = x.shape
    ts = SEQ_TILE
    assert D == D_MODEL and S % ts == 0 and ts % BLOCK == 0
    n_tiles = B * S // ts

    idx = np.arange(256) // HEAD_DIM
    bd = jnp.asarray((idx[:, None] == idx[None, :]).astype(np.float32), dtype=_BF16)
    qw = (jnp.tile(q_norm_w.astype(_F32), N_HEADS) * (HEAD_DIM ** -0.5)).reshape(1, D_ATTN)
    kw = jnp.tile(k_norm_w.astype(_F32), N_KV_HEADS).reshape(1, 128)
    xt = x.reshape(n_tiles, ts, D)

    const2 = lambda t: (0, 0)
    resident = pl.Buffered(1)
    in_specs = [
        pl.BlockSpec(memory_space=pltpu.SMEM),
        pl.BlockSpec(memory_space=pltpu.SMEM),
        pl.BlockSpec((None, ts, D), lambda t: (jnp.minimum(t + 1, n_tiles - 1), 0, 0)),
        pl.BlockSpec((None, ts, D), lambda t: (jnp.maximum(t - 2, 0), 0, 0)),
        pl.BlockSpec((1, D), const2),
        pl.BlockSpec((D, D_IN), const2, pipeline_mode=resident),
        pl.BlockSpec((1, D_ATTN), const2),
        pl.BlockSpec((1, 128), const2),
        pl.BlockSpec((256, 256), const2),
        pl.BlockSpec((CONV_WIDTH, D_CONV), const2),
        pl.BlockSpec((1, D_CONV), const2),
        pl.BlockSpec((1, D_CONV), const2),
        pl.BlockSpec((1, D_CONV), const2),
        pl.BlockSpec((D, D), const2, pipeline_mode=resident),
        pl.BlockSpec((BLOCK, 2 * BLOCK), const2),
    ]
    scratch = [
        pltpu.VMEM((ts, D), _BF16),
        pltpu.VMEM((ts, UA0 - Q0), _F32),
        pltpu.VMEM((ts, ZC0 - UA0), _F32),
        pltpu.VMEM((ts, D_IN - ZC0), _F32),
        pltpu.VMEM((ts, D_ATTN + 128), _F32),
        pltpu.VMEM((2, ts, D_ATTN), _BF16),
        pltpu.VMEM((2, 4, ts + KV_HALO, 128), _BF16),
        pltpu.VMEM((2, 4, ts + KV_HALO, 256), _BF16),
        pltpu.VMEM((2 * N_SLABS, U_ROWS, 128), _F32),
        pltpu.VMEM((2, ts, D_ATTN), _F32),
        pltpu.VMEM((2, ts, D_CONV), _F32),
        pltpu.VMEM((ts, D_CONV), _F32),
        pltpu.VMEM((N_SLABS, C_ROWS, 128), _F32),
        pltpu.VMEM((2, ts, D), _BF16),
        pltpu.VMEM((2, 4, 2 * BLOCK, 2 * BLOCK), _F32),
        pltpu.VMEM((ATTN_SLOTS, 2 * BLOCK, 2 * BLOCK), _F32),
        pltpu.VMEM((ATTN_SLOTS, 2, BLOCK, 128), _F32),
    ]
    out = pl.pallas_call(
        functools.partial(_layer_kernel, tiles_per_seq=S // ts, n_tiles=n_tiles),
        grid=(n_tiles + 2,),
        in_specs=in_specs,
        out_specs=pl.BlockSpec((None, ts, D), lambda t: (jnp.maximum(t - 2, 0), 0, 0)),
        out_shape=jax.ShapeDtypeStruct((n_tiles, ts, D), x.dtype),
        scratch_shapes=scratch,
        compiler_params=pltpu.CompilerParams(
            dimension_semantics=("arbitrary",),
            vmem_limit_bytes=VMEM_LIMIT_BYTES),
        name="hybrid_layer",
    )(sinks.astype(_F32), rel_bias.astype(_F32).reshape(NUM_BUCKETS * N_HEADS), xt, xt,
      norm_w.astype(_F32).reshape(1, D), w_in.astype(_BF16), qw, kw, bd,
      dw_w.astype(_F32), dw_b.astype(_F32).reshape(1, D_CONV), ln_w.astype(_F32).reshape(1, D_CONV),
      ln_b.astype(_F32).reshape(1, D_CONV), w_out.astype(_BF16), jnp.asarray(_banded_buckets()))
    return out.reshape(B, S, D)


def kernel(x, norm_w, w_in, q_norm_w, k_norm_w, sinks, dw_w, dw_b, ln_w, ln_b, w_out, rel_bias):
    depth = norm_w.shape[0]
    for l in range(depth):
        x = _layer(x, norm_w[l], w_in[l], q_norm_w[l], k_norm_w[l], sinks[l], dw_w[l], dw_b[l],
                   ln_w[l], ln_b[l], w_out[l], rel_bias)
    return x
```

```python
import functools
import math

import numpy as np
import jax
import jax.numpy as jnp
from jax import lax
from jax.experimental import pallas as pl
from jax.experimental.pallas import tpu as pltpu

D_MODEL = 1024
D_ATTN = 512
D_CONV = 512
HEAD_DIM = 64
N_HEADS = 8
N_KV_HEADS = 2
BLOCK = 128
NUM_BUCKETS = 32
MAX_DISTANCE = 128
CONV_WIDTH = 31
D_IN = 2816
EPS = 1e-6
LN_EPS = 1e-5
NEG_INF = -1e30

SEQ_TILE = 512
KV_HALO = BLOCK
ROW_CHUNK = 64
SUBLANES = 8
N_SLABS = D_CONV // 128
BF16_SUBLANES = 16
N_VACC = 4
N_TBLK = BF16_SUBLANES
T_BLK = SEQ_TILE // N_TBLK
U_PITCH = T_BLK + 1
U_ROWS = ((N_TBLK + 1) * U_PITCH + 7) // 8 * 8
N_UTILES = CONV_WIDTH - 1 + T_BLK
C_PITCH = T_BLK + 8
C_ROWS = N_TBLK * C_PITCH
N_CHUNK = 256
ATTN_SLOTS = 8
VMEM_LIMIT_BYTES = 56 * 1024 * 1024

Q0, K0, V0, ZA0, UA0, UG0, ZC0 = 0, 512, 640, 768, 1280, 1792, 2304

_F32 = jnp.float32
_BF16 = jnp.bfloat16


def _t5_bucket_table():
    qi = np.arange(BLOCK)[:, None]
    sj = np.arange(2 * BLOCK)[None, :]
    dist = qi + BLOCK - sj
    n = np.maximum(dist, 0)
    max_exact = NUM_BUCKETS // 2
    nf = np.maximum(n, 1).astype(np.float32)
    large = max_exact + (np.log(nf / max_exact) / math.log(MAX_DISTANCE / max_exact)
                         * (NUM_BUCKETS - max_exact)).astype(np.int32)
    large = np.minimum(large, NUM_BUCKETS - 1)
    bucket = np.where(n < max_exact, n, large).astype(np.int32)
    band = (dist >= 0) & (dist < BLOCK)
    return bucket, band


def _banded_buckets():
    bucket, band = _t5_bucket_table()
    return np.where(band, bucket, -1).astype(np.int32)


def _sigmoid(z):
    return 0.5 * jnp.tanh(0.5 * z) + 0.5


def _silu(z):
    h = 0.5 * z
    return h + h * jnp.tanh(h)


def _build_bias_tables(relb_ref, bucket_ref, bias_s):
    cur_keys = lax.broadcasted_iota(jnp.int32, (8, 2 * BLOCK), 1) >= BLOCK

    def body(c, carry):
        r0 = pl.multiple_of(c * 8, 8)
        bk = bucket_ref[pl.ds(r0, 8), :]
        accs = [jnp.full((8, 2 * BLOCK), NEG_INF, _F32) for _ in range(N_HEADS)]
        for b in range(NUM_BUCKETS):
            hit = bk == b
            for h in range(N_HEADS):
                accs[h] = jnp.where(hit, relb_ref[b * N_HEADS + h], accs[h])
        for h in range(N_HEADS):
            g, rem = divmod(h, 4)
            j, e = divmod(rem, 2)
            rows = pl.ds(BLOCK * j + r0, 8)
            bias_s[0, 2 * g + e, rows, :] = accs[h]
            bias_s[1, 2 * g + e, rows, :] = jnp.where(cur_keys, accs[h], NEG_INF)
        return carry

    lax.fori_loop(0, BLOCK // 8, body, 0)


class _RawProj:
    bounds = (Q0, UA0, ZC0, D_IN)

    def __init__(self, *refs):
        self.refs = refs

    def _find(self, c0, c1):
        for ref, lo, hi in zip(self.refs, self.bounds[:-1], self.bounds[1:]):
            if lo <= c0 and c1 <= hi:
                return ref, c0 - lo, c1 - lo
        raise ValueError((c0, c1))

    def load(self, rows, c0, c1):
        ref, a, b = self._find(c0, c1)
        return ref[rows, a:b]

    def store(self, c0, c1, val):
        ref, a, b = self._find(c0, c1)
        ref[:, a:b] = val


def _row_chunks():
    return [slice(r, r + ROW_CHUNK) for r in range(0, SEQ_TILE, ROW_CHUNK)]


def _stream_a(sa, x_ref, x0_ref, nw_ref, win_ref, qw_ref, kw_ref, bd_ref,
              h_s, p_s, ssq_s, q_s, k_s, v_s, u_s, ga_s, gc_s):
    ts = SEQ_TILE

    def rms(src_ref, r):
        def run():
            xc = src_ref[r:r + ROW_CHUNK, :]
            ss = jnp.sum(xc * xc, axis=-1, keepdims=True)
            rs = lax.rsqrt(ss * (1.0 / D_MODEL) + EPS)
            h_s[r:r + ROW_CHUNK, :] = ((xc * rs) * nw_ref[...]).astype(_BF16)
        return run

    def dot(c0, c1):
        def run():
            p_s.store(c0, c1, jnp.dot(h_s[...], win_ref[:, c0:c1], preferred_element_type=_F32))
        return run

    def q_sumsq(half):
        def run():
            cs = slice(256 * half, 256 * half + 256)
            q = p_s.load(slice(None), Q0 + cs.start, Q0 + cs.stop)
            ssq_s[:, cs] = jnp.dot((q * q).astype(_BF16), bd_ref[...], preferred_element_type=_F32)
        return run

    def q_norm(half):
        def run():
            cs = slice(256 * half, 256 * half + 256)
            for rr in _row_chunks():
                q = p_s.load(rr, Q0 + cs.start, Q0 + cs.stop)
                qn =(q * lax.rsqrt(ssq_s[rr, cs] * (1.0 / HEAD_DIM) + EPS)) * qw_ref[:, cs]
                q_s[sa, rr, cs] = qn.astype(_BF16)
        return run

    def k_sumsq():
        k = p_s.load(slice(None), K0, K0 + 128)
        ssq_s[:, D_ATTN:D_ATTN + 128] = jnp.dot((k * k).astype(_BF16), bd_ref[0:128, 0:128],
                                                preferred_element_type=_F32)

    def kv_store():
        lo = lax.broadcasted_iota(jnp.int32, (ROW_CHUNK, 128), 1) < HEAD_DIM
        zero = jnp.zeros((ROW_CHUNK, 128), _F32)
        one = jnp.ones((ROW_CHUNK, 128), _F32)
        ones_lo = jnp.where(lo, one, zero).astype(_BF16)
        ones_hi = jnp.where(lo, zero, one).astype(_BF16)
        for rr in _row_chunks():
            cur = slice(KV_HALO + rr.start, KV_HALO + rr.stop)
            k = p_s.load(rr, K0, K0 + 128)
            v = p_s.load(rr, V0, V0 + 128)
            kn = (k * lax.rsqrt(ssq_s[rr, D_ATTN:D_ATTN + 128] * (1.0 / HEAD_DIM) + EPS)) * kw_ref[...]
            kn_sw = pltpu.roll(kn, 64, axis=1)
            v_sw = pltpu.roll(v, 64, axis=1)
            k_s[sa, 0, cur, :] = jnp.where(lo, kn, zero).astype(_BF16)
            k_s[sa, 1, cur, :] = jnp.where(lo, zero, kn_sw).astype(_BF16)
            k_s[sa, 2, cur, :] = jnp.where(lo, kn_sw, zero).astype(_BF16)
            k_s[sa, 3, cur, :] = jnp.where(lo, zero, kn).astype(_BF16)
            v_s[sa, 0, cur, 0:128] = jnp.where(lo, v, zero).astype(_BF16)
            v_s[sa, 1, cur, 0:128] = jnp.where(lo, zero, v_sw).astype(_BF16)
            v_s[sa, 2, cur, 0:128] = jnp.where(lo, v_sw, zero).astype(_BF16)
            v_s[sa, 3, cur, 0:128] = jnp.where(lo, zero, v).astype(_BF16)
            v_s[sa, 0, cur, 128:256] = ones_lo
            v_s[sa, 1, cur, 128:256] = ones_hi
            v_s[sa, 2, cur, 128:256] = ones_lo
            v_s[sa, 3, cur, 128:256] = ones_hi

    def gate(c0, dst, half):
        def run():
            cs = slice(256 * half, 256 * half + 256)
            for rr in _row_chunks():
                z = p_s.load(rr, c0 + 256 * half, c0 + 256 * half + 256)
                dst[sa, rr, cs] = _silu(z)
        return run

    def glu(half):
        def run():
            for blk in range(N_TBLK):
                tr = slice(T_BLK * blk, T_BLK * (blk + 1))
                a = p_s.load(tr, UA0 + 256 * half, UA0 + 256 * half + 256)
                g = p_s.load(tr, UG0 + 256 * half, UG0 + 256 * half + 256)
                u = a * _sigmoid(g)
                p0 = U_PITCH * (blk + 1)
                for sl in range(2):
                    u_s[sa * N_SLABS + 2 * half + sl, p0:p0 + T_BLK, :] = u[:, 128 * sl:128 * sl + 128]
        return run

    return dict(
        rms_next=[rms(x_ref, r) for r in range(0, ts, ROW_CHUNK)],
        rms_first=[rms(x0_ref, r) for r in range(0, ts, ROW_CHUNK)],
        dot_qkvz=dot(Q0, UA0), dot_ag=dot(UA0, ZC0), dot_zc=dot(ZC0, D_IN),
        q_sumsq=[q_sumsq(0), q_sumsq(1)], q_norm=[q_norm(0), q_norm(1)],
        k_sumsq=k_sumsq, kv_store=kv_store,
        gate_a=[gate(ZA0, ga_s, 0), gate(ZA0, ga_s, 1)],
        gate_c=[gate(ZC0, gc_s, 0), gate(ZC0, gc_s, 1)],
        glu=[glu(0), glu(1)])


def _stream_b(sb, first_b, sink_ref, wb_ref, dwb_ref, lnw_ref, lnb_ref,
              q_s, k_s, v_s, u_s, u2_s, ga_s, gc_s, c_s, yc_s, y_s, bias_s, o_s, dn_s):
    ts = SEQ_TILE
    taps = CONV_WIDTH - 1

    def pack(cb):
        def run():
            for k in range(N_UTILES):
                tau = k - taps
                base = U_PITCH + tau if tau >= 0 else T_BLK + tau
                halves = [u_s[sb * N_SLABS + cb, pl.ds(base + h * SUBLANES * U_PITCH, SUBLANES, stride=U_PITCH), :]
                          for h in range(N_TBLK // SUBLANES)]
                u2_s[cb, BF16_SUBLANES * k:BF16_SUBLANES * (k + 1), :] = (
                    jnp.concatenate(halves, axis=0).astype(_BF16))
        return run

    def conv(cb):
        def run():
            ls = slice(128 * cb, 128 * cb + 128)
            bv = jnp.broadcast_to(dwb_ref[:, ls], (BF16_SUBLANES, 128))
            wv = [wb_ref[BF16_SUBLANES * j:BF16_SUBLANES * (j + 1), ls].astype(_F32) for j in range(CONV_WIDTH)]
            for r0 in range(0, T_BLK, N_VACC):
                accs = [None] * N_VACC
                for k in range(r0, r0 + N_VACC + taps):
                    win = u2_s[cb, BF16_SUBLANES * k:BF16_SUBLANES * (k + 1), :].astype(_F32)
                    for n in range(N_VACC):
                        j = k - (r0 + n)
                        if 0 <= j < CONV_WIDTH:
                            term = win * wv[j]
                            accs[n] = term if accs[n] is None else accs[n] + term
                for n in range(N_VACC):
                    r = r0 + n
                    c_s[BF16_SUBLANES * r:BF16_SUBLANES * (r + 1), ls] = accs[n] + bv
        return run

    def ln(r):
        def run():
            y = c_s[r:r + ROW_CHUNK, :]
            mu = jnp.sum(y, axis=-1, keepdims=True) * (1.0 / D_CONV)
            d = y - mu
            var = jnp.sum(d * d, axis=-1, keepdims=True) * (1.0 / D_CONV)
            yn = (d * lax.rsqrt(var + LN_EPS)) * lnw_ref[...] + lnb_ref[...]
            act = _silu(yn)
            for n in range(ROW_CHUNK // SUBLANES):
                row = r + SUBLANES * n
                tau, blk0 = divmod(row, BF16_SUBLANES)
                for cb in range(N_SLABS):
                    yc_s[cb, pl.ds(C_PITCH * blk0 + tau, SUBLANES, stride=C_PITCH), :] = (
                        act[SUBLANES * n:SUBLANES * (n + 1), 128 * cb:128 * cb + 128])
        return run

    def regate(blk):
        def run():
            tr = slice(T_BLK * blk, T_BLK * (blk + 1))
            for cb in range(N_SLABS):
                ls = slice(128 * cb, 128 * cb + 128)
                val = yc_s[cb, C_PITCH * blk:C_PITCH * blk + T_BLK, :] * gc_s[sb, tr, ls]
                y_s[sb, tr, D_ATTN + 128 * cb:D_ATTN + 128 * cb + 128] = val.astype(_BF16)
        return run

    def attn_front(b, g):
        def run():
            slot = (2 * b + g) % ATTN_SLOTS
            rows = slice(BLOCK * b, BLOCK * b + BLOCK)
            krows = slice(BLOCK * b, BLOCK * b + 2 * BLOCK)
            tbl = first_b if b == 0 else 0
            lo_q = lax.broadcasted_iota(jnp.int32, (BLOCK, 128), 1) < HEAD_DIM
            c0 = 256 * g
            qg = jnp.concatenate([q_s[sb, rows, c0:c0 + 128], q_s[sb, rows, c0 + 128:c0 + 256]], axis=0)
            kk = jnp.concatenate([k_s[sb, 2 * g, krows, :], k_s[sb, 2 * g + 1, krows, :]], axis=0)
            s = lax.dot_general(qg, kk, (((1,), (1,)), ((), ())),
                                preferred_element_type=_F32)
            ps = [[None, None], [None, None]]
            ex = [[None, None], [None, None]]
            for e in range(2):
                se = s[:, 2 * BLOCK * e:2 * BLOCK * (e + 1)] + bias_s[tbl, 2 * g + e]
                for j in range(2):
                    sink = sink_ref[4 * g + 2 * j + e]
                    sj = se[BLOCK * j:BLOCK * j + BLOCK, :]
                    m = jnp.maximum(jnp.max(sj, axis=-1, keepdims=True), sink)
                    ps[j][e] = jnp.exp(sj - m).astype(_BF16)
                    ex[j][e] = jnp.exp(sink - m)
            p = jnp.concatenate([jnp.concatenate(ps[0], axis=1), jnp.concatenate(ps[1], axis=1)], axis=0)
            vv = jnp.concatenate([v_s[sb, 2 * g, krows, :], v_s[sb, 2 * g + 1, krows, :]], axis=0)
            o_s[slot] = jnp.dot(p, vv, preferred_element_type=_F32)
            for j in range(2):
                dn_s[slot, j] = jnp.where(lo_q, ex[j][0], ex[j][1])
        return run

    def attn_norm(b, g):
        def run():
            slot = (2 * b + g) % ATTN_SLOTS
            rows = slice(BLOCK * b, BLOCK * b + BLOCK)
            for j in range(2):
                oj = o_s[slot, BLOCK * j:BLOCK * j + BLOCK, :]
                den = oj[:, 128:256] + dn_s[slot, j]
                cs = slice(256 * g + 128 * j, 256 * g + 128 * j + 128)
                y_s[sb, rows, cs] = ((oj[:, 0:128] / den) * ga_s[sb, rows, cs]).astype(_BF16)
        return run

    pairs = [(b, g) for b in range(ts // BLOCK) for g in range(N_KV_HEADS)]
    return dict(attn_front=[attn_front(b, g) for b, g in pairs],
                attn_norm=[attn_norm(b, g) for b, g in pairs],
                pack=[pack(cb) for cb in range(N_SLABS)],
                conv=[conv(cb) for cb in range(N_SLABS)],
                ln=[ln(r) for r in range(0, ts, ROW_CHUNK)],
                regate=[regate(blk) for blk in range(N_TBLK)])


def _stream_c(sy, xres_ref, wout_ref, o_ref, y_s):
    def outproj(n):
        def run():
            o_ref[:, n:n + N_CHUNK] = xres_ref[:, n:n + N_CHUNK] + jnp.dot(
                y_s[sy], wout_ref[:, n:n + N_CHUNK], preferred_element_type=_F32)
        return run

    return [outproj(n) for n in range(0, D_MODEL, N_CHUNK)]


def _layer_kernel(sink_ref, relb_ref, x_ref, xres_ref, nw_ref, win_ref, qw_ref, kw_ref, bd_ref, wb_ref,
                  dwb_ref, lnw_ref, lnb_ref, wout_ref, bucket_ref,
                  o_ref,
                  h_s, p1_s, p2_s, p3_s, ssq_s, q_s, k_s, v_s, u_s, u2_s, ga_s, gc_s, c_s, yc_s, y_s,
                  bias_s, o_s, dn_s, *, tiles_per_seq, n_tiles):
    p_s = _RawProj(p1_s, p2_s, p3_s)
    ts = SEQ_TILE
    t = pl.program_id(0)
    sa = lax.rem(t, 2)
    sb = 1 - sa
    first_a = lax.rem(t, tiles_per_seq) == 0
    first_b = jnp.where(lax.rem(t + tiles_per_seq - 1, tiles_per_seq) == 0, 1, 0)

    @pl.when(t == 0)
    def _():
        _build_bias_tables(relb_ref, bucket_ref, bias_s)
        y_s[1] = jnp.zeros(y_s.shape[1:], _BF16)

    @pl.when(first_a)
    def _():
        k_s[sa, :, 0:KV_HALO, :] = jnp.zeros((4, KV_HALO, 128), _BF16)
        v_s[sa, :, 0:KV_HALO, :] = jnp.zeros((4, KV_HALO, 256), _BF16)
        for cb in range(N_SLABS):
            u_s[sa * N_SLABS + cb, 0:T_BLK, :] = jnp.zeros((T_BLK, 128), _F32)

    @pl.when(jnp.logical_not(first_a))
    def _():
        k_s[sa, :, 0:KV_HALO, :] = k_s[sb, :, ts:ts + KV_HALO, :]
        v_s[sa, :, 0:KV_HALO, :] = v_s[sb, :, ts:ts + KV_HALO, :]
        for cb in range(N_SLABS):
            u_s[sa * N_SLABS + cb, 0:T_BLK, :] = u_s[sb * N_SLABS + cb,
                                                     N_TBLK * U_PITCH:N_TBLK * U_PITCH + T_BLK, :]

    A = _stream_a(sa, x_ref, xres_ref, nw_ref, win_ref, qw_ref, kw_ref, bd_ref,
                  h_s, p_s, ssq_s, q_s, k_s, v_s, u_s, ga_s, gc_s)
    Bm = _stream_b(sb, first_b, sink_ref, wb_ref, dwb_ref, lnw_ref, lnb_ref,
                   q_s, k_s, v_s, u_s, u2_s, ga_s, gc_s, c_s, yc_s, y_s, bias_s, o_s, dn_s)
    C = _stream_c(sa, xres_ref, wout_ref, o_ref, y_s)
    af, an = Bm["attn_front"], Bm["attn_norm"]

    regions = [
        Bm["pack"] + Bm["conv"],
        Bm["ln"] + Bm["regate"] + af + an + [A["dot_qkvz"], A["dot_zc"]],
        A["q_sumsq"] + [A["k_sumsq"]] + A["gate_a"] + A["gate_c"] + A["q_norm"]
        + [A["kv_store"]] + [A["dot_ag"]],
        A["glu"] + C + A["rms_next"],
    ]
    fill_regions = [
        A["rms_first"],
        [A["dot_qkvz"], A["dot_zc"]],
        A["q_sumsq"] + [A["k_sumsq"]] + A["gate_a"] + A["gate_c"] + A["q_norm"]
        + [A["kv_store"]] + [A["dot_ag"]],
        A["glu"] + A["rms_next"],
    ]
    drain_regions = [C]
    schedule = [(regions, (t >= 1) & (t <= n_tiles)),
                (fill_regions, t == 0),
                (drain_regions, t == n_tiles + 1)]
    for group, active in schedule:
        trips = jnp.where(active, 1, 0)
        for region in group:
            def body(_, carry, region=region):
                for item in region:
                    item()
                return carry
            lax.fori_loop(0, trips, body, 0)


def _layer(x, norm_w, w_in, q_norm_w, k_norm_w, sinks, dw_w, dw_b, ln_w, ln_b, w_out, rel_bias):
    B, S, D = x.shape
    ts = SEQ_TILE
    assert D == D_MODEL and S % ts == 0 and ts % BLOCK == 0
    n_tiles = B * S // ts

    idx = np.arange(256) // HEAD_DIM
    bd = jnp.asarray((idx[:, None] == idx[None, :]).astype(np.float32), dtype=_BF16)
    qw = (jnp.tile(q_norm_w.astype(_F32), N_HEADS) * (HEAD_DIM ** -0.5)).reshape(1, D_ATTN)
    kw = jnp.tile(k_norm_w.astype(_F32), N_KV_HEADS).reshape(1, 128)
    xt = x.reshape(n_tiles, ts, D)
    wb = jnp.broadcast_to(dw_w.astype(_BF16)[:, None, :], (CONV_WIDTH, BF16_SUBLANES, D_CONV))
    wb = wb.reshape(CONV_WIDTH * BF16_SUBLANES, D_CONV)

    const2 = lambda t: (0, 0)
    resident = pl.Buffered(1)
    in_specs = [
        pl.BlockSpec(memory_space=pltpu.SMEM),
        pl.BlockSpec(memory_space=pltpu.SMEM),
        pl.BlockSpec((None, ts, D), lambda t: (jnp.minimum(t + 1, n_tiles - 1), 0, 0)),
        pl.BlockSpec((None, ts, D), lambda t: (jnp.maximum(t - 2, 0), 0, 0)),
        pl.BlockSpec((1, D), const2),
        pl.BlockSpec((D, D_IN), const2, pipeline_mode=resident),
        pl.BlockSpec((1, D_ATTN), const2),
        pl.BlockSpec((1, 128), const2),
        pl.BlockSpec((256, 256), const2),
        pl.BlockSpec((CONV_WIDTH * BF16_SUBLANES, D_CONV), const2),
        pl.BlockSpec((1, D_CONV), const2),
        pl.BlockSpec((1, D_CONV), const2),
        pl.BlockSpec((1, D_CONV), const2),
        pl.BlockSpec((D, D), const2, pipeline_mode=resident),
        pl.BlockSpec((BLOCK, 2 * BLOCK), const2),
    ]
    scratch = [
        pltpu.VMEM((ts, D), _BF16),
        pltpu.VMEM((ts, UA0 - Q0), _F32),
        pltpu.VMEM((ts, ZC0 - UA0), _F32),
        pltpu.VMEM((ts, D_IN - ZC0), _F32),
        pltpu.VMEM((ts, D_ATTN + 128), _F32),
        pltpu.VMEM((2, ts, D_ATTN), _BF16),
        pltpu.VMEM((2, 4, ts + KV_HALO, 128), _BF16),
        pltpu.VMEM((2, 4, ts + KV_HALO, 256), _BF16),
        pltpu.VMEM((2 * N_SLABS, U_ROWS, 128), _F32),
        pltpu.VMEM((N_SLABS, N_UTILES * BF16_SUBLANES, 128), _BF16),
        pltpu.VMEM((2, ts, D_ATTN), _F32),
        pltpu.VMEM((2, ts, D_CONV), _F32),
        pltpu.VMEM((ts, D_CONV), _F32),
        pltpu.VMEM((N_SLABS, C_ROWS, 128), _F32),
        pltpu.VMEM((2, ts, D), _BF16),
        pltpu.VMEM((2, 4, 2 * BLOCK, 2 * BLOCK), _F32),
        pltpu.VMEM((ATTN_SLOTS, 2 * BLOCK, 2 * BLOCK), _F32),
        pltpu.VMEM((ATTN_SLOTS, 2, BLOCK, 128), _F32),
    ]
    out = pl.pallas_call(
        functools.partial(_layer_kernel, tiles_per_seq=S // ts, n_tiles=n_tiles),
        grid=(n_tiles + 2,),
        in_specs=in_specs,
        out_specs=pl.BlockSpec((None, ts, D), lambda t: (jnp.maximum(t - 2, 0), 0, 0)),
        out_shape=jax.ShapeDtypeStruct((n_tiles, ts, D), x.dtype),
        scratch_shapes=scratch,
        compiler_params=pltpu.CompilerParams(
            dimension_semantics=("arbitrary",),
            vmem_limit_bytes=VMEM_LIMIT_BYTES),
        name="hybrid_layer",
    )(sinks.astype(_F32), rel_bias.astype(_F32).reshape(NUM_BUCKETS * N_HEADS), xt, xt,
      norm_w.astype(_F32).reshape(1, D), w_in.astype(_BF16), qw, kw, bd,
      wb, dw_b.astype(_F32).reshape(1, D_CONV), ln_w.astype(_F32).reshape(1, D_CONV),
      ln_b.astype(_F32).reshape(1, D_CONV), w_out.astype(_BF16), jnp.asarray(_banded_buckets()))
    return out.reshape(B, S, D)


def kernel(x, norm_w, w_in, q_norm_w, k_norm_w, sinks, dw_w, dw_b, ln_w, ln_b, w_out, rel_bias):
    depth = norm_w.shape[0]
    for l in range(depth):
        x = _layer(x, norm_w[l], w_in[l], q_norm_w[l], k_norm_w[l], sinks[l], dw_w[l], dw_b[l],
                   ln_w[l], ln_b[l], w_out[l], rel_bias)
    return x
```

```python
import functools
import math

import numpy as np
import jax
import jax.numpy as jnp
from jax import lax
from jax.experimental import pallas as pl
from jax.experimental.pallas import tpu as pltpu

D_MODEL = 1024
D_ATTN = 512
D_CONV = 512
HEAD_DIM = 64
N_HEADS = 8
N_KV_HEADS = 2
BLOCK = 128
NUM_BUCKETS = 32
MAX_DISTANCE = 128
CONV_WIDTH = 31
D_IN = 2816
EPS = 1e-6
LN_EPS = 1e-5
NEG_INF = -1e30
LOG2E = math.log2(math.e)

SEQ_TILE = 512
KV_HALO = BLOCK
ROW_CHUNK = 64
SUBLANES = 8
N_SLABS = D_CONV // 128
BF16_SUBLANES = 16
N_VACC = 4
N_TBLK = BF16_SUBLANES
T_BLK = SEQ_TILE // N_TBLK
U_PITCH = T_BLK + 1
U_ROWS = ((N_TBLK + 1) * U_PITCH + 7) // 8 * 8
N_UTILES = CONV_WIDTH - 1 + T_BLK
C_PITCH = T_BLK + 8
C_ROWS = N_TBLK * C_PITCH
N_CHUNK = 256
ATTN_SLOTS = 8
VMEM_LIMIT_BYTES = 56 * 1024 * 1024

Q0, K0, V0, ZA0, UA0, UG0, ZC0 = 0, 512, 640, 768, 1280, 1792, 2304

_F32 = jnp.float32
_BF16 = jnp.bfloat16


def _t5_bucket_table():
    qi = np.arange(BLOCK)[:, None]
    sj = np.arange(2 * BLOCK)[None, :]
    dist = qi + BLOCK - sj
    n = np.maximum(dist, 0)
    max_exact = NUM_BUCKETS // 2
    nf = np.maximum(n, 1).astype(np.float32)
    large = max_exact + (np.log(nf / max_exact) / math.log(MAX_DISTANCE / max_exact)
                         * (NUM_BUCKETS - max_exact)).astype(np.int32)
    large = np.minimum(large, NUM_BUCKETS - 1)
    bucket = np.where(n < max_exact, n, large).astype(np.int32)
    band = (dist >= 0) & (dist < BLOCK)
    return bucket, band


def _banded_buckets():
    bucket, band = _t5_bucket_table()
    return np.where(band, bucket, -1).astype(np.int32)


def _sigmoid(z):
    return 0.5 * jnp.tanh(0.5 * z) + 0.5


def _silu(z):
    h = 0.5 * z
    return h + h * jnp.tanh(h)


def _build_bias_tables(relb_ref, bucket_ref, bias_s):
    cur_keys = lax.broadcasted_iota(jnp.int32, (8, 2 * BLOCK), 1) >= BLOCK

    def body(c, carry):
        r0 = pl.multiple_of(c * 8, 8)
        bk = bucket_ref[pl.ds(r0, 8), :]
        accs = [jnp.full((8, 2 * BLOCK), NEG_INF, _F32) for _ in range(N_HEADS)]
        for b in range(NUM_BUCKETS):
            hit = bk == b
            for h in range(N_HEADS):
                accs[h] = jnp.where(hit, relb_ref[b * N_HEADS + h] * LOG2E, accs[h])
        for h in range(N_HEADS):
            g, rem = divmod(h, 4)
            j, e = divmod(rem, 2)
            rows = pl.ds(BLOCK * j + r0, 8)
            bias_s[0, 2 * g + e, rows, :] = accs[h]
            bias_s[1, 2 * g + e, rows, :] = jnp.where(cur_keys, accs[h], NEG_INF)
        return carry

    lax.fori_loop(0, BLOCK // 8, body, 0)


class _RawProj:
    bounds = (Q0, UA0, ZC0, D_IN)

    def __init__(self, *refs):
        self.refs = refs

    def _find(self, c0, c1):
        for ref, lo, hi in zip(self.refs, self.bounds[:-1], self.bounds[1:]):
            if lo <= c0 and c1 <= hi:
                return ref, c0 - lo, c1 - lo
        raise ValueError((c0, c1))

    def load(self, rows, c0, c1):
        ref, a, b = self._find(c0, c1)
        return ref[rows, a:b]

    def store(self, c0, c1, val):
        ref, a, b = self._find(c0, c1)
        ref[:, a:b] = val


def _row_chunks():
    return [slice(r, r + ROW_CHUNK) for r in range(0, SEQ_TILE, ROW_CHUNK)]


def _stream_a(sa, x_ref, x0_ref, nw_ref, win_ref, qw_ref, kw_ref, bd_ref,
              h_s, p_s, ssq_s, q_s, k_s, v_s, u_s, ga_s, gc_s):
    ts = SEQ_TILE

    def rms(src_ref, r):
        def run():
            xc = src_ref[r:r + ROW_CHUNK, :]
            ss = jnp.sum(xc * xc, axis=-1, keepdims=True)
            rs = lax.rsqrt(ss * (1.0 / D_MODEL) + EPS)
            h_s[r:r + ROW_CHUNK, :] = ((xc * rs) * nw_ref[...]).astype(_BF16)
        return run

    def dot(c0, c1):
        def run():
            p_s.store(c0, c1, jnp.dot(h_s[...], win_ref[:, c0:c1], preferred_element_type=_F32))
        return run

    def q_sumsq(half):
        def run():
            cs = slice(256 * half, 256 * half + 256)
            q = p_s.load(slice(None), Q0 + cs.start, Q0 + cs.stop)
            ssq_s[:, cs] = jnp.dot((q * q).astype(_BF16), bd_ref[...], preferred_element_type=_F32)
        return run

    def q_norm(half):
        def run():
            cs = slice(256 * half, 256 * half + 256)
            for rr in _row_chunks():
                q = p_s.load(rr, Q0 + cs.start, Q0 + cs.stop)
                qn =(q * lax.rsqrt(ssq_s[rr, cs] * (1.0 / HEAD_DIM) + EPS)) * qw_ref[:, cs]
                q_s[sa, rr, cs] = qn.astype(_BF16)
        return run

    def k_sumsq():
        k = p_s.load(slice(None), K0, K0 + 128)
        ssq_s[:, D_ATTN:D_ATTN + 128] = jnp.dot((k * k).astype(_BF16), bd_ref[0:128, 0:128],
                                                preferred_element_type=_F32)

    def kv_store():
        lo = lax.broadcasted_iota(jnp.int32, (ROW_CHUNK, 128), 1) < HEAD_DIM
        zero = jnp.zeros((ROW_CHUNK, 128), _F32)
        one = jnp.ones((ROW_CHUNK, 128), _F32)
        ones_lo = jnp.where(lo, one, zero).astype(_BF16)
        ones_hi = jnp.where(lo, zero, one).astype(_BF16)
        for rr in _row_chunks():
            cur = slice(KV_HALO + rr.start, KV_HALO + rr.stop)
            k = p_s.load(rr, K0, K0 + 128)
            v = p_s.load(rr, V0, V0 + 128)
            kn = (k * lax.rsqrt(ssq_s[rr, D_ATTN:D_ATTN + 128] * (1.0 / HEAD_DIM) + EPS)) * kw_ref[...]
            kn_sw = pltpu.roll(kn, 64, axis=1)
            v_sw = pltpu.roll(v, 64, axis=1)
            k_s[sa, 0, cur, :] = jnp.where(lo, kn, zero).astype(_BF16)
            k_s[sa, 1, cur, :] = jnp.where(lo, zero, kn_sw).astype(_BF16)
            k_s[sa, 2, cur, :] = jnp.where(lo, kn_sw, zero).astype(_BF16)
            k_s[sa, 3, cur, :] = jnp.where(lo, zero, kn).astype(_BF16)
            v_s[sa, 0, cur, 0:128] = jnp.where(lo, v, zero).astype(_BF16)
            v_s[sa, 1, cur, 0:128] = jnp.where(lo, zero, v_sw).astype(_BF16)
            v_s[sa, 2, cur, 0:128] = jnp.where(lo, v_sw, zero).astype(_BF16)
            v_s[sa, 3, cur, 0:128] = jnp.where(lo, zero, v).astype(_BF16)
            v_s[sa, 0, cur, 128:256] = ones_lo
            v_s[sa, 1, cur, 128:256] = ones_hi
            v_s[sa, 2, cur, 128:256] = ones_lo
            v_s[sa, 3, cur, 128:256] = ones_hi

    def gate(c0, dst, half):
        def run():
            cs = slice(256 * half, 256 * half + 256)
            for rr in _row_chunks():
                z = p_s.load(rr, c0 + 256 * half, c0 + 256 * half + 256)
                dst[sa, rr, cs] = _silu(z)
        return run

    def glu(half):
        def run():
            for blk in range(N_TBLK):
                tr = slice(T_BLK * blk, T_BLK * (blk + 1))
                a = p_s.load(tr, UA0 + 256 * half, UA0 + 256 * half + 256)
                g = p_s.load(tr, UG0 + 256 * half, UG0 + 256 * half + 256)
                u = a * _sigmoid(g)
                p0 = U_PITCH * (blk + 1)
                for sl in range(2):
                    u_s[sa * N_SLABS + 2 * half + sl, p0:p0 + T_BLK, :] = u[:, 128 * sl:128 * sl + 128]
        return run

    return dict(
        rms_next=[rms(x_ref, r) for r in range(0, ts, ROW_CHUNK)],
        rms_first=[rms(x0_ref, r) for r in range(0, ts, ROW_CHUNK)],
        dot_qkvz=dot(Q0, UA0), dot_ag=dot(UA0, ZC0), dot_zc=dot(ZC0, D_IN),
        q_sumsq=[q_sumsq(0), q_sumsq(1)], q_norm=[q_norm(0), q_norm(1)],
        k_sumsq=k_sumsq, kv_store=kv_store,
        gate_a=[gate(ZA0, ga_s, 0), gate(ZA0, ga_s, 1)],
        gate_c=[gate(ZC0, gc_s, 0), gate(ZC0, gc_s, 1)],
        glu=[glu(0), glu(1)])


def _stream_b(sb, first_b, sink_ref, wb_ref, dwb_ref, lnw_ref, lnb_ref,
              q_s, k_s, v_s, u_s, u2_s, ga_s, gc_s, c_s, yc_s, y_s, bias_s, o_s, dn_s):
    ts = SEQ_TILE
    taps = CONV_WIDTH - 1

    def pack(cb):
        def run():
            for k in range(N_UTILES):
                tau = k - taps
                base = U_PITCH + tau if tau >= 0 else T_BLK + tau
                halves = [u_s[sb * N_SLABS + cb, pl.ds(base + h * SUBLANES * U_PITCH, SUBLANES, stride=U_PITCH), :]
                          for h in range(N_TBLK // SUBLANES)]
                u2_s[cb, BF16_SUBLANES * k:BF16_SUBLANES * (k + 1), :] = (
                    jnp.concatenate(halves, axis=0).astype(_BF16))
        return run

    def conv(cb):
        def run():
            ls = slice(128 * cb, 128 * cb + 128)
            bv = jnp.broadcast_to(dwb_ref[:, ls], (BF16_SUBLANES, 128))
            wv = [wb_ref[BF16_SUBLANES * j:BF16_SUBLANES * (j + 1), ls].astype(_F32) for j in range(CONV_WIDTH)]
            for r0 in range(0, T_BLK, N_VACC):
                accs = [None] * N_VACC
                for k in range(r0, r0 + N_VACC + taps):
                    win = u2_s[cb, BF16_SUBLANES * k:BF16_SUBLANES * (k + 1), :].astype(_F32)
                    for n in range(N_VACC):
                        j = k - (r0 + n)
                        if 0 <= j < CONV_WIDTH:
                            term = win * wv[j]
                            accs[n] = term if accs[n] is None else accs[n] + term
                for n in range(N_VACC):
                    r = r0 + n
                    c_s[BF16_SUBLANES * r:BF16_SUBLANES * (r + 1), ls] = accs[n] + bv
        return run

    def ln(r):
        def run():
            y = c_s[r:r + ROW_CHUNK, :]
            mu = jnp.sum(y, axis=-1, keepdims=True) * (1.0 / D_CONV)
            d = y - mu
            var = jnp.sum(d * d, axis=-1, keepdims=True) * (1.0 / D_CONV)
            yn = (d * lax.rsqrt(var + LN_EPS)) * lnw_ref[...] + lnb_ref[...]
            act = _silu(yn)
            for n in range(ROW_CHUNK // SUBLANES):
                row = r + SUBLANES * n
                tau, blk0 = divmod(row, BF16_SUBLANES)
                for cb in range(N_SLABS):
                    yc_s[cb, pl.ds(C_PITCH * blk0 + tau, SUBLANES, stride=C_PITCH), :] = (
                        act[SUBLANES * n:SUBLANES * (n + 1), 128 * cb:128 * cb + 128])
        return run

    def regate(blk):
        def run():
            tr = slice(T_BLK * blk, T_BLK * (blk + 1))
            for cb in range(N_SLABS):
                ls = slice(128 * cb, 128 * cb + 128)
                val = yc_s[cb, C_PITCH * blk:C_PITCH * blk + T_BLK, :] * gc_s[sb, tr, ls]
                y_s[sb, tr, D_ATTN + 128 * cb:D_ATTN + 128 * cb + 128] = val.astype(_BF16)
        return run

    def attn_front(b, g):
        def run():
            slot = (2 * b + g) % ATTN_SLOTS
            rows = slice(BLOCK * b, BLOCK * b + BLOCK)
            krows = slice(BLOCK * b, BLOCK * b + 2 * BLOCK)
            tbl = first_b if b == 0 else 0
            lo_q = lax.broadcasted_iota(jnp.int32, (BLOCK, 128), 1) < HEAD_DIM
            c0 = 256 * g
            qg = jnp.concatenate([q_s[sb, rows, c0:c0 + 128], q_s[sb, rows, c0 + 128:c0 + 256]], axis=0)
            kk = jnp.concatenate([k_s[sb, 2 * g, krows, :], k_s[sb, 2 * g + 1, krows, :]], axis=0)
            s = lax.dot_general(qg, kk, (((1,), (1,)), ((), ())),
                                preferred_element_type=_F32)
            ps = [[None, None], [None, None]]
            ex = [[None, None], [None, None]]
            for e in range(2):
                se = s[:, 2 * BLOCK * e:2 * BLOCK * (e + 1)] + bias_s[tbl, 2 * g + e]
                for j in range(2):
                    sink = sink_ref[4 * g + 2 * j + e] * LOG2E
                    sj = se[BLOCK * j:BLOCK * j + BLOCK, :]
                    m = jnp.maximum(jnp.max(sj, axis=-1, keepdims=True), sink)
                    ps[j][e] = jnp.exp2(sj - m).astype(_BF16)
                    ex[j][e] = jnp.exp2(sink - m)
            p = jnp.concatenate([jnp.concatenate(ps[0], axis=1), jnp.concatenate(ps[1], axis=1)], axis=0)
            vv = jnp.concatenate([v_s[sb, 2 * g, krows, :], v_s[sb, 2 * g + 1, krows, :]], axis=0)
            o_s[slot] = jnp.dot(p, vv, preferred_element_type=_F32)
            for j in range(2):
                dn_s[slot, j] = jnp.where(lo_q, ex[j][0], ex[j][1])
        return run

    def attn_norm(b, g):
        def run():
            slot = (2 * b + g) % ATTN_SLOTS
            rows = slice(BLOCK * b, BLOCK * b + BLOCK)
            for j in range(2):
                oj = o_s[slot, BLOCK * j:BLOCK * j + BLOCK, :]
                den = oj[:, 128:256] + dn_s[slot, j]
                cs = slice(256 * g + 128 * j, 256 * g + 128 * j + 128)
                y_s[sb, rows, cs] = ((oj[:, 0:128] / den) * ga_s[sb, rows, cs]).astype(_BF16)
        return run

    pairs = [(b, g) for b in range(ts // BLOCK) for g in range(N_KV_HEADS)]
    return dict(attn_front=[attn_front(b, g) for b, g in pairs],
                attn_norm=[attn_norm(b, g) for b, g in pairs],
                pack=[pack(cb) for cb in range(N_SLABS)],
                conv=[conv(cb) for cb in range(N_SLABS)],
                ln=[ln(r) for r in range(0, ts, ROW_CHUNK)],
                regate=[regate(blk) for blk in range(N_TBLK)])


def _stream_c(sy, xres_ref, wout_ref, o_ref, y_s):
    def outproj(n):
        def run():
            o_ref[:, n:n + N_CHUNK] = xres_ref[:, n:n + N_CHUNK] + jnp.dot(
                y_s[sy], wout_ref[:, n:n + N_CHUNK], preferred_element_type=_F32)
        return run

    return [outproj(n) for n in range(0, D_MODEL, N_CHUNK)]


def _layer_kernel(sink_ref, relb_ref, x_ref, xres_ref, nw_ref, win_ref, qw_ref, kw_ref, bd_ref, wb_ref,
                  dwb_ref, lnw_ref, lnb_ref, wout_ref, bucket_ref,
                  o_ref,
                  h_s, p1_s, p2_s, p3_s, ssq_s, q_s, k_s, v_s, u_s, u2_s, ga_s, gc_s, c_s, yc_s, y_s,
                  bias_s, o_s, dn_s, *, tiles_per_seq, n_tiles):
    p_s = _RawProj(p1_s, p2_s, p3_s)
    ts = SEQ_TILE
    t = pl.program_id(0)
    sa = lax.rem(t, 2)
    sb = 1 - sa
    first_a = lax.rem(t, tiles_per_seq) == 0
    first_b = jnp.where(lax.rem(t + tiles_per_seq - 1, tiles_per_seq) == 0, 1, 0)

    @pl.when(t == 0)
    def _():
        _build_bias_tables(relb_ref, bucket_ref, bias_s)
        y_s[1] = jnp.zeros(y_s.shape[1:], _BF16)

    @pl.when(first_a)
    def _():
        k_s[sa, :, 0:KV_HALO, :] = jnp.zeros((4, KV_HALO, 128), _BF16)
        v_s[sa, :, 0:KV_HALO, :] = jnp.zeros((4, KV_HALO, 256), _BF16)
        for cb in range(N_SLABS):
            u_s[sa * N_SLABS + cb, 0:T_BLK, :] = jnp.zeros((T_BLK, 128), _F32)

    @pl.when(jnp.logical_not(first_a))
    def _():
        k_s[sa, :, 0:KV_HALO, :] = k_s[sb, :, ts:ts + KV_HALO, :]
        v_s[sa, :, 0:KV_HALO, :] = v_s[sb, :, ts:ts + KV_HALO, :]
        for cb in range(N_SLABS):
            u_s[sa * N_SLABS + cb, 0:T_BLK, :] = u_s[sb * N_SLABS + cb,
                                                     N_TBLK * U_PITCH:N_TBLK * U_PITCH + T_BLK, :]

    A = _stream_a(sa, x_ref, xres_ref, nw_ref, win_ref, qw_ref, kw_ref, bd_ref,
                  h_s, p_s, ssq_s, q_s, k_s, v_s, u_s, ga_s, gc_s)
    Bm = _stream_b(sb, first_b, sink_ref, wb_ref, dwb_ref, lnw_ref, lnb_ref,
                   q_s, k_s, v_s, u_s, u2_s, ga_s, gc_s, c_s, yc_s, y_s, bias_s, o_s, dn_s)
    C = _stream_c(sa, xres_ref, wout_ref, o_ref, y_s)
    af, an = Bm["attn_front"], Bm["attn_norm"]

    regions = [
        Bm["pack"] + af + Bm["conv"] + an,
        Bm["ln"] + Bm["regate"] + [A["dot_qkvz"], A["dot_zc"]],
        A["q_sumsq"] + [A["k_sumsq"]] + A["gate_a"] + A["gate_c"] + A["q_norm"]
        + [A["kv_store"]] + [A["dot_ag"]],
        A["glu"] + C + A["rms_next"],
    ]
    fill_regions = [
        A["rms_first"],
        [A["dot_qkvz"], A["dot_zc"]],
        A["q_sumsq"] + [A["k_sumsq"]] + A["gate_a"] + A["gate_c"] + A["q_norm"]
        + [A["kv_store"]] + [A["dot_ag"]],
        A["glu"] + A["rms_next"],
    ]
    drain_regions = [C]
    schedule = [(regions, (t >= 1) & (t <= n_tiles)),
                (fill_regions, t == 0),
                (drain_regions, t == n_tiles + 1)]
    for group, active in schedule:
        trips = jnp.where(active, 1, 0)
        for region in group:
            def body(_, carry, region=region):
                for item in region:
                    item()
                return carry
            lax.fori_loop(0, trips, body, 0)


def _layer(x, norm_w, w_in, q_norm_w, k_norm_w, sinks, dw_w, dw_b, ln_w, ln_b, w_out, rel_bias):
    B, S, D = x.shape
    ts = SEQ_TILE
    assert D == D_MODEL and S % ts == 0 and ts % BLOCK == 0
    n_tiles = B * S // ts

    idx = np.arange(256) // HEAD_DIM
    bd = jnp.asarray((idx[:, None] == idx[None, :]).astype(np.float32), dtype=_BF16)
    qw = (jnp.tile(q_norm_w.astype(_F32), N_HEADS) * (HEAD_DIM ** -0.5 * LOG2E)).reshape(1, D_ATTN)
    kw = jnp.tile(k_norm_w.astype(_F32), N_KV_HEADS).reshape(1, 128)
    xt = x.reshape(n_tiles, ts, D)
    wb = jnp.broadcast_to(dw_w.astype(_BF16)[:, None, :], (CONV_WIDTH, BF16_SUBLANES, D_CONV))
    wb = wb.reshape(CONV_WIDTH * BF16_SUBLANES, D_CONV)

    const2 = lambda t: (0, 0)
    resident = pl.Buffered(1)
    in_specs = [
        pl.BlockSpec(memory_space=pltpu.SMEM),
        pl.BlockSpec(memory_space=pltpu.SMEM),
        pl.BlockSpec((None, ts, D), lambda t: (jnp.minimum(t + 1, n_tiles - 1), 0, 0)),
        pl.BlockSpec((None, ts, D), lambda t: (jnp.maximum(t - 2, 0), 0, 0)),
        pl.BlockSpec((1, D), const2),
        pl.BlockSpec((D, D_IN), const2, pipeline_mode=resident),
        pl.BlockSpec((1, D_ATTN), const2),
        pl.BlockSpec((1, 128), const2),
        pl.BlockSpec((256, 256), const2),
        pl.BlockSpec((CONV_WIDTH * BF16_SUBLANES, D_CONV), const2),
        pl.BlockSpec((1, D_CONV), const2),
        pl.BlockSpec((1, D_CONV), const2),
        pl.BlockSpec((1, D_CONV), const2),
        pl.BlockSpec((D, D), const2, pipeline_mode=resident),
        pl.BlockSpec((BLOCK, 2 * BLOCK), const2),
    ]
    scratch = [
        pltpu.VMEM((ts, D), _BF16),
        pltpu.VMEM((ts, UA0 - Q0), _F32),
        pltpu.VMEM((ts, ZC0 - UA0), _F32),
        pltpu.VMEM((ts, D_IN - ZC0), _F32),
        pltpu.VMEM((ts, D_ATTN + 128), _F32),
        pltpu.VMEM((2, ts, D_ATTN), _BF16),
        pltpu.VMEM((2, 4, ts + KV_HALO, 128), _BF16),
        pltpu.VMEM((2, 4, ts + KV_HALO, 256), _BF16),
        pltpu.VMEM((2 * N_SLABS, U_ROWS, 128), _F32),
        pltpu.VMEM((N_SLABS, N_UTILES * BF16_SUBLANES, 128), _BF16),
        pltpu.VMEM((2, ts, D_ATTN), _F32),
        pltpu.VMEM((2, ts, D_CONV), _F32),
        pltpu.VMEM((ts, D_CONV), _F32),
        pltpu.VMEM((N_SLABS, C_ROWS, 128), _F32),
        pltpu.VMEM((2, ts, D), _BF16),
        pltpu.VMEM((2, 4, 2 * BLOCK, 2 * BLOCK), _F32),
        pltpu.VMEM((ATTN_SLOTS, 2 * BLOCK, 2 * BLOCK), _F32),
        pltpu.VMEM((ATTN_SLOTS, 2, BLOCK, 128), _F32),
    ]
    out = pl.pallas_call(
        functools.partial(_layer_kernel, tiles_per_seq=S // ts, n_tiles=n_tiles),
        grid=(n_tiles + 2,),
        in_specs=in_specs,
        out_specs=pl.BlockSpec((None, ts, D), lambda t: (jnp.maximum(t - 2, 0), 0, 0)),
        out_shape=jax.ShapeDtypeStruct((n_tiles, ts, D), x.dtype),
        scratch_shapes=scratch,
        compiler_params=pltpu.CompilerParams(
            dimension_semantics=("arbitrary",),
            vmem_limit_bytes=VMEM_LIMIT_BYTES),
        name="hybrid_layer",
    )(sinks.astype(_F32), rel_bias.astype(_F32).reshape(NUM_BUCKETS * N_HEADS), xt, xt,
      norm_w.astype(_F32).reshape(1, D), w_in.astype(_BF16), qw, kw, bd,
      wb, dw_b.astype(_F32).reshape(1, D_CONV), ln_w.astype(_F32).reshape(1, D_CONV),
      ln_b.astype(_F32).reshape(1, D_CONV), w_out.astype(_BF16), jnp.asarray(_banded_buckets()))
    return out.reshape(B, S, D)


def kernel(x, norm_w, w_in, q_norm_w, k_norm_w, sinks, dw_w, dw_b, ln_w, ln_b, w_out, rel_bias):
    depth = norm_w.shape[0]
    for l in range(depth):
        x = _layer(x, norm_w[l], w_in[l], q_norm_w[l], k_norm_w[l], sinks[l], dw_w[l], dw_b[l],
                   ln_w[l], ln_b[l], w_out[l], rel_bias)
    return x
```

```python
import functools
import math

import numpy as np
import jax
import jax.numpy as jnp
from jax import lax
from jax.experimental import pallas as pl
from jax.experimental.pallas import tpu as pltpu

D_MODEL = 1024
D_ATTN = 512
D_CONV = 512
HEAD_DIM = 64
N_HEADS = 8
N_KV_HEADS = 2
BLOCK = 128
NUM_BUCKETS = 32
MAX_DISTANCE = 128
CONV_WIDTH = 31
D_IN = 2816
EPS = 1e-6
LN_EPS = 1e-5
NEG_INF = -1e30
LOG2E = math.log2(math.e)

SEQ_TILE = 512
KV_HALO = BLOCK
ROW_CHUNK = 64
SUBLANES = 8
N_SLABS = D_CONV // 128
BF16_SUBLANES = 16
N_VACC = 4
N_TBLK = BF16_SUBLANES
T_BLK = SEQ_TILE // N_TBLK
U_PITCH = T_BLK + 1
U_ROWS = ((N_TBLK + 1) * U_PITCH + 7) // 8 * 8
N_UTILES = CONV_WIDTH - 1 + T_BLK
C_PITCH = T_BLK + 8
C_ROWS = N_TBLK * C_PITCH
N_CHUNK = 256
W_ROWS = 128
ATTN_SLOTS = 8
VMEM_LIMIT_BYTES = 56 * 1024 * 1024

Q0, K0, V0, ZA0, UA0, UG0, ZC0 = 0, 512, 640, 768, 1280, 1792, 2304

_F32 = jnp.float32
_BF16 = jnp.bfloat16


def _t5_bucket_table():
    qi = np.arange(BLOCK)[:, None]
    sj = np.arange(2 * BLOCK)[None, :]
    dist = qi + BLOCK - sj
    n = np.maximum(dist, 0)
    max_exact = NUM_BUCKETS // 2
    nf = np.maximum(n, 1).astype(np.float32)
    large = max_exact + (np.log(nf / max_exact) / math.log(MAX_DISTANCE / max_exact)
                         * (NUM_BUCKETS - max_exact)).astype(np.int32)
    large = np.minimum(large, NUM_BUCKETS - 1)
    bucket = np.where(n < max_exact, n, large).astype(np.int32)
    band = (dist >= 0) & (dist < BLOCK)
    return bucket, band


def _banded_buckets():
    bucket, band = _t5_bucket_table()
    return np.where(band, bucket, -1).astype(np.int32)


def _sigmoid(z):
    return 0.5 * jnp.tanh(0.5 * z) + 0.5


def _silu(z):
    h = 0.5 * z
    return h + h * jnp.tanh(h)


def _build_bias_tables(relb_ref, bucket_ref, bias_s):
    cur_keys = lax.broadcasted_iota(jnp.int32, (8, 2 * BLOCK), 1) >= BLOCK

    def body(c, carry):
        r0 = pl.multiple_of(c * 8, 8)
        bk = bucket_ref[pl.ds(r0, 8), :]
        accs = [jnp.full((8, 2 * BLOCK), NEG_INF, _F32) for _ in range(N_HEADS)]
        for b in range(NUM_BUCKETS):
            hit = bk == b
            for h in range(N_HEADS):
                accs[h] = jnp.where(hit, relb_ref[b * N_HEADS + h] * LOG2E, accs[h])
        for h in range(N_HEADS):
            g, rem = divmod(h, 4)
            j, e = divmod(rem, 2)
            rows = pl.ds(BLOCK * j + r0, 8)
            bias_s[0, 2 * g + e, rows, :] = accs[h]
            bias_s[1, 2 * g + e, rows, :] = jnp.where(cur_keys, accs[h], NEG_INF)
        return carry

    lax.fori_loop(0, BLOCK // 8, body, 0)


class _RawProj:
    bounds = (Q0, UA0, ZC0, D_IN)

    def __init__(self, *refs):
        self.refs = refs

    def _find(self, c0, c1):
        for ref, lo, hi in zip(self.refs, self.bounds[:-1], self.bounds[1:]):
            if lo <= c0 and c1 <= hi:
                return ref, c0 - lo, c1 - lo
        raise ValueError((c0, c1))

    def load(self, rows, c0, c1):
        ref, a, b = self._find(c0, c1)
        return ref[rows, a:b]

    def store(self, c0, c1, val):
        ref, a, b = self._find(c0, c1)
        ref[:, a:b] = val


def _row_chunks():
    return [slice(r, r + ROW_CHUNK) for r in range(0, SEQ_TILE, ROW_CHUNK)]


def _stream_a(sa, x_ref, x0_ref, nw_ref, win_ref, qw_ref, kw_ref, bd_ref,
              h_s, p_s, ssq_s, q_s, k_s, v_s, u_s, ga_s, gc_s):
    ts = SEQ_TILE

    def rms(src_ref, r):
        def run():
            xc = src_ref[r:r + ROW_CHUNK, :]
            ss = jnp.sum(xc * xc, axis=-1, keepdims=True)
            rs = lax.rsqrt(ss * (1.0 / D_MODEL) + EPS)
            h_s[r:r + ROW_CHUNK, :] = ((xc * rs) * nw_ref[...]).astype(_BF16)
        return run

    def dot(c0, c1):
        def run():
            p_s.store(c0, c1, jnp.dot(h_s[...], win_ref[:, c0:c1], preferred_element_type=_F32))
        return run

    def q_sumsq(half):
        def run():
            cs = slice(256 * half, 256 * half + 256)
            q = p_s.load(slice(None), Q0 + cs.start, Q0 + cs.stop)
            ssq_s[:, cs] = jnp.dot((q * q).astype(_BF16), bd_ref[...], preferred_element_type=_F32)
        return run

    def q_norm(half):
        def run():
            cs = slice(256 * half, 256 * half + 256)
            for rr in _row_chunks():
                q = p_s.load(rr, Q0 + cs.start, Q0 + cs.stop)
                qn =(q * lax.rsqrt(ssq_s[rr, cs] * (1.0 / HEAD_DIM) + EPS)) * qw_ref[:, cs]
                q_s[sa, rr, cs] = qn.astype(_BF16)
        return run

    def k_sumsq():
        k = p_s.load(slice(None), K0, K0 + 128)
        ssq_s[:, D_ATTN:D_ATTN + 128] = jnp.dot((k * k).astype(_BF16), bd_ref[0:128, 0:128],
                                                preferred_element_type=_F32)

    def kv_store():
        lo = lax.broadcasted_iota(jnp.int32, (ROW_CHUNK, 128), 1) < HEAD_DIM
        zero = jnp.zeros((ROW_CHUNK, 128), _F32)
        one = jnp.ones((ROW_CHUNK, 128), _F32)
        ones_lo = jnp.where(lo, one, zero).astype(_BF16)
        ones_hi = jnp.where(lo, zero, one).astype(_BF16)
        for rr in _row_chunks():
            cur = slice(KV_HALO + rr.start, KV_HALO + rr.stop)
            k = p_s.load(rr, K0, K0 + 128)
            v = p_s.load(rr, V0, V0 + 128)
            kn = (k * lax.rsqrt(ssq_s[rr, D_ATTN:D_ATTN + 128] * (1.0 / HEAD_DIM) + EPS)) * kw_ref[...]
            kn_sw = pltpu.roll(kn, 64, axis=1)
            v_sw = pltpu.roll(v, 64, axis=1)
            k_s[sa, 0, cur, :] = jnp.where(lo, kn, zero).astype(_BF16)
            k_s[sa, 1, cur, :] = jnp.where(lo, zero, kn_sw).astype(_BF16)
            k_s[sa, 2, cur, :] = jnp.where(lo, kn_sw, zero).astype(_BF16)
            k_s[sa, 3, cur, :] = jnp.where(lo, zero, kn).astype(_BF16)
            v_s[sa, 0, cur, 0:128] = jnp.where(lo, v, zero).astype(_BF16)
            v_s[sa, 1, cur, 0:128] = jnp.where(lo, zero, v_sw).astype(_BF16)
            v_s[sa, 2, cur, 0:128] = jnp.where(lo, v_sw, zero).astype(_BF16)
            v_s[sa, 3, cur, 0:128] = jnp.where(lo, zero, v).astype(_BF16)
            v_s[sa, 0, cur, 128:256] = ones_lo
            v_s[sa, 1, cur, 128:256] = ones_hi
            v_s[sa, 2, cur, 128:256] = ones_lo
            v_s[sa, 3, cur, 128:256] = ones_hi

    def gate(c0, dst, half):
        def run():
            cs = slice(256 * half, 256 * half + 256)
            for rr in _row_chunks():
                z = p_s.load(rr, c0 + 256 * half, c0 + 256 * half + 256)
                dst[sa, rr, cs] = _silu(z)
        return run

    def glu(half):
        def run():
            for blk in range(N_TBLK):
                tr = slice(T_BLK * blk, T_BLK * (blk + 1))
                a = p_s.load(tr, UA0 + 256 * half, UA0 + 256 * half + 256)
                g = p_s.load(tr, UG0 + 256 * half, UG0 + 256 * half + 256)
                u = a * _sigmoid(g)
                p0 = U_PITCH * (blk + 1)
                for sl in range(2):
                    u_s[sa * N_SLABS + 2 * half + sl, p0:p0 + T_BLK, :] = u[:, 128 * sl:128 * sl + 128]
        return run

    return dict(
        rms_next=[rms(x_ref, r) for r in range(0, ts, ROW_CHUNK)],
        rms_first=[rms(x0_ref, r) for r in range(0, ts, ROW_CHUNK)],
        dot_qkvz=dot(Q0, UA0), dot_ag=dot(UA0, ZC0), dot_zc=dot(ZC0, D_IN),
        q_sumsq=[q_sumsq(0), q_sumsq(1)], q_norm=[q_norm(0), q_norm(1)],
        k_sumsq=k_sumsq, kv_store=kv_store,
        gate_a=[gate(ZA0, ga_s, 0), gate(ZA0, ga_s, 1)],
        gate_c=[gate(ZC0, gc_s, 0), gate(ZC0, gc_s, 1)],
        glu=[glu(0), glu(1)])


def _stream_b(sb, first_b, sink_ref, wb_ref, dwb_ref, lnw_ref, lnb_ref,
              q_s, k_s, v_s, u_s, u2_s, ga_s, gc_s, c_s, yc_s, y_s, bias_s, o_s, dn_s):
    ts = SEQ_TILE
    taps = CONV_WIDTH - 1

    def pack(cb):
        def run():
            for k in range(N_UTILES):
                tau = k - taps
                base = U_PITCH + tau if tau >= 0 else T_BLK + tau
                halves = [u_s[sb * N_SLABS + cb, pl.ds(base + h * SUBLANES * U_PITCH, SUBLANES, stride=U_PITCH), :]
                          for h in range(N_TBLK // SUBLANES)]
                u2_s[cb, BF16_SUBLANES * k:BF16_SUBLANES * (k + 1), :] = (
                    jnp.concatenate(halves, axis=0).astype(_BF16))
        return run

    def conv(cb):
        def run():
            ls = slice(128 * cb, 128 * cb + 128)
            bv = jnp.broadcast_to(dwb_ref[:, ls], (BF16_SUBLANES, 128))
            wv = [wb_ref[BF16_SUBLANES * j:BF16_SUBLANES * (j + 1), ls].astype(_F32) for j in range(CONV_WIDTH)]
            for r0 in range(0, T_BLK, N_VACC):
                accs = [None] * N_VACC
                for k in range(r0, r0 + N_VACC + taps):
                    win = u2_s[cb, BF16_SUBLANES * k:BF16_SUBLANES * (k + 1), :].astype(_F32)
                    for n in range(N_VACC):
                        j = k - (r0 + n)
                        if 0 <= j < CONV_WIDTH:
                            term = win * wv[j]
                            accs[n] = term if accs[n] is None else accs[n] + term
                for n in range(N_VACC):
                    r = r0 + n
                    c_s[BF16_SUBLANES * r:BF16_SUBLANES * (r + 1), ls] = accs[n] + bv
        return run

    def ln(r):
        def run():
            y = c_s[r:r + ROW_CHUNK, :]
            mu = jnp.sum(y, axis=-1, keepdims=True) * (1.0 / D_CONV)
            d = y - mu
            var = jnp.sum(d * d, axis=-1, keepdims=True) * (1.0 / D_CONV)
            yn = (d * lax.rsqrt(var + LN_EPS)) * lnw_ref[...] + lnb_ref[...]
            act = _silu(yn)
            for n in range(ROW_CHUNK // SUBLANES):
                row = r + SUBLANES * n
                tau, blk0 = divmod(row, BF16_SUBLANES)
                for cb in range(N_SLABS):
                    yc_s[cb, pl.ds(C_PITCH * blk0 + tau, SUBLANES, stride=C_PITCH), :] = (
                        act[SUBLANES * n:SUBLANES * (n + 1), 128 * cb:128 * cb + 128])
        return run

    def regate(blk):
        def run():
            tr = slice(T_BLK * blk, T_BLK * (blk + 1))
            for cb in range(N_SLABS):
                ls = slice(128 * cb, 128 * cb + 128)
                val = yc_s[cb, C_PITCH * blk:C_PITCH * blk + T_BLK, :] * gc_s[sb, tr, ls]
                y_s[sb, tr, D_ATTN + 128 * cb:D_ATTN + 128 * cb + 128] = val.astype(_BF16)
        return run

    def attn_front(b, g):
        def run():
            slot = (2 * b + g) % ATTN_SLOTS
            rows = slice(BLOCK * b, BLOCK * b + BLOCK)
            krows = slice(BLOCK * b, BLOCK * b + 2 * BLOCK)
            tbl = first_b if b == 0 else 0
            lo_q = lax.broadcasted_iota(jnp.int32, (BLOCK, 128), 1) < HEAD_DIM
            c0 = 256 * g
            qg = jnp.concatenate([q_s[sb, rows, c0:c0 + 128], q_s[sb, rows, c0 + 128:c0 + 256]], axis=0)
            kk = jnp.concatenate([k_s[sb, 2 * g, krows, :], k_s[sb, 2 * g + 1, krows, :]], axis=0)
            s = lax.dot_general(qg, kk, (((1,), (1,)), ((), ())),
                                preferred_element_type=_F32)
            ps = [[None, None], [None, None]]
            ex = [[None, None], [None, None]]
            for e in range(2):
                se = s[:, 2 * BLOCK * e:2 * BLOCK * (e + 1)] + bias_s[tbl, 2 * g + e]
                for j in range(2):
                    sink = sink_ref[4 * g + 2 * j + e] * LOG2E
                    sj = se[BLOCK * j:BLOCK * j + BLOCK, :]
                    m = jnp.maximum(jnp.max(sj, axis=-1, keepdims=True), sink)
                    ps[j][e] = jnp.exp2(sj - m).astype(_BF16)
                    ex[j][e] = jnp.exp2(sink - m)
            p = jnp.concatenate([jnp.concatenate(ps[0], axis=1), jnp.concatenate(ps[1], axis=1)], axis=0)
            vv = jnp.concatenate([v_s[sb, 2 * g, krows, :], v_s[sb, 2 * g + 1, krows, :]], axis=0)
            o_s[slot] = jnp.dot(p, vv, preferred_element_type=_F32)
            for j in range(2):
                dn_s[slot, j] = jnp.where(lo_q, ex[j][0], ex[j][1])
        return run

    def attn_norm(b, g):
        def run():
            slot = (2 * b + g) % ATTN_SLOTS
            rows = slice(BLOCK * b, BLOCK * b + BLOCK)
            for j in range(2):
                oj = o_s[slot, BLOCK * j:BLOCK * j + BLOCK, :]
                den = oj[:, 128:256] + dn_s[slot, j]
                cs = slice(256 * g + 128 * j, 256 * g + 128 * j + 128)
                y_s[sb, rows, cs] = ((oj[:, 0:128] / den) * ga_s[sb, rows, cs]).astype(_BF16)
        return run

    pairs = [(b, g) for b in range(ts // BLOCK) for g in range(N_KV_HEADS)]
    return dict(attn_front=[attn_front(b, g) for b, g in pairs],
                attn_norm=[attn_norm(b, g) for b, g in pairs],
                pack=[pack(cb) for cb in range(N_SLABS)],
                conv=[conv(cb) for cb in range(N_SLABS)],
                ln=[ln(r) for r in range(0, ts, ROW_CHUNK)],
                regate=[regate(blk) for blk in range(N_TBLK)])


def _stream_c(sy, xres_ref, wout_ref, o_ref, y_s):
    def outproj(n):
        def run():
            o_ref[:, n:n + N_CHUNK] = xres_ref[:, n:n + N_CHUNK] + jnp.dot(
                y_s[sy], wout_ref[:, n:n + N_CHUNK], preferred_element_type=_F32)
        return run

    return [outproj(n) for n in range(0, D_MODEL, N_CHUNK)]


def _load_weight_bf16(w_hbm, w_s, stage, sem):
    rows, cols = w_s.shape
    n_chunks = rows // W_ROWS

    def copy(c):
        slot = c % 2
        return pltpu.make_async_copy(w_hbm.at[pl.ds(c * W_ROWS, W_ROWS), :],
                                     stage.at[slot, :, pl.ds(0, cols)], sem.at[slot])

    copy(0).start()
    for c in range(n_chunks):
        if c + 1 < n_chunks:
            copy(c + 1).start()
        copy(c).wait()
        w_s[c * W_ROWS:(c + 1) * W_ROWS, :] = stage[c % 2, :, 0:cols].astype(_BF16)


def _layer_kernel(sink_ref, relb_ref, x_ref, xres_ref, nw_ref, win_hbm, qw_ref, kw_ref, bd_ref, wb_ref,
                  dwb_ref, lnw_ref, lnb_ref, wout_hbm, bucket_ref,
                  o_ref,
                  win_ref, wout_ref, wstage_s, wsem,
                  h_s, p1_s, p2_s, p3_s, ssq_s, q_s, k_s, v_s, u_s, u2_s, ga_s, gc_s, c_s, yc_s, y_s,
                  bias_s, o_s, dn_s, *, tiles_per_seq, n_tiles):
    p_s = _RawProj(p1_s, p2_s, p3_s)
    ts = SEQ_TILE
    t = pl.program_id(0)
    sa = lax.rem(t, 2)
    sb = 1 - sa
    first_a = lax.rem(t, tiles_per_seq) == 0
    first_b = jnp.where(lax.rem(t + tiles_per_seq - 1, tiles_per_seq) == 0, 1, 0)

    @pl.when(t == 0)
    def _():
        _load_weight_bf16(win_hbm, win_ref, wstage_s, wsem)
        _load_weight_bf16(wout_hbm, wout_ref, wstage_s, wsem)
        _build_bias_tables(relb_ref, bucket_ref, bias_s)
        y_s[1] = jnp.zeros(y_s.shape[1:], _BF16)

    @pl.when(first_a)
    def _():
        k_s[sa, :, 0:KV_HALO, :] = jnp.zeros((4, KV_HALO, 128), _BF16)
        v_s[sa, :, 0:KV_HALO, :] = jnp.zeros((4, KV_HALO, 256), _BF16)
        for cb in range(N_SLABS):
            u_s[sa * N_SLABS + cb, 0:T_BLK, :] = jnp.zeros((T_BLK, 128), _F32)

    @pl.when(jnp.logical_not(first_a))
    def _():
        k_s[sa, :, 0:KV_HALO, :] = k_s[sb, :, ts:ts + KV_HALO, :]
        v_s[sa, :, 0:KV_HALO, :] = v_s[sb, :, ts:ts + KV_HALO, :]
        for cb in range(N_SLABS):
            u_s[sa * N_SLABS + cb, 0:T_BLK, :] = u_s[sb * N_SLABS + cb,
                                                     N_TBLK * U_PITCH:N_TBLK * U_PITCH + T_BLK, :]

    A = _stream_a(sa, x_ref, xres_ref, nw_ref, win_ref, qw_ref, kw_ref, bd_ref,
                  h_s, p_s, ssq_s, q_s, k_s, v_s, u_s, ga_s, gc_s)
    Bm = _stream_b(sb, first_b, sink_ref, wb_ref, dwb_ref, lnw_ref, lnb_ref,
                   q_s, k_s, v_s, u_s, u2_s, ga_s, gc_s, c_s, yc_s, y_s, bias_s, o_s, dn_s)
    C = _stream_c(sa, xres_ref, wout_ref, o_ref, y_s)
    af, an = Bm["attn_front"], Bm["attn_norm"]

    regions = [
        Bm["pack"] + af + Bm["conv"] + an,
        Bm["ln"] + Bm["regate"] + [A["dot_qkvz"], A["dot_zc"]],
        A["q_sumsq"] + [A["k_sumsq"]] + A["gate_a"] + A["gate_c"] + A["q_norm"]
        + [A["kv_store"]] + [A["dot_ag"]],
        A["glu"] + C + A["rms_next"],
    ]
    fill_regions = [
        A["rms_first"],
        [A["dot_qkvz"], A["dot_zc"]],
        A["q_sumsq"] + [A["k_sumsq"]] + A["gate_a"] + A["gate_c"] + A["q_norm"]
        + [A["kv_store"]] + [A["dot_ag"]],
        A["glu"] + A["rms_next"],
    ]
    drain_regions = [
        Bm["pack"] + af + Bm["conv"] + an,
        Bm["ln"] + Bm["regate"],
        C,
    ]
    last_regions = [C]
    schedule = [(regions, (t >= 1) & (t < n_tiles)),
                (fill_regions, t == 0),
                (drain_regions, t == n_tiles),
                (last_regions, t == n_tiles + 1)]
    for group, active in schedule:
        trips = jnp.where(active, 1, 0)
        for region in group:
            def body(_, carry, region=region):
                for item in region:
                    item()
                return carry
            lax.fori_loop(0, trips, body, 0)


def _layer(x, norm_w, w_in, q_norm_w, k_norm_w, sinks, dw_w, dw_b, ln_w, ln_b, w_out, rel_bias):
    B, S, D = x.shape
    ts = SEQ_TILE
    assert D == D_MODEL and S % ts == 0 and ts % BLOCK == 0
    n_tiles = B * S // ts

    idx = np.arange(256) // HEAD_DIM
    bd = jnp.asarray((idx[:, None] == idx[None, :]).astype(np.float32), dtype=_BF16)
    qw = (jnp.tile(q_norm_w.astype(_F32), N_HEADS) * (HEAD_DIM ** -0.5 * LOG2E)).reshape(1, D_ATTN)
    kw = jnp.tile(k_norm_w.astype(_F32), N_KV_HEADS).reshape(1, 128)
    xt = x.reshape(n_tiles, ts, D)
    wb = jnp.broadcast_to(dw_w.astype(_BF16)[:, None, :], (CONV_WIDTH, BF16_SUBLANES, D_CONV))
    wb = wb.reshape(CONV_WIDTH * BF16_SUBLANES, D_CONV)

    const2 = lambda t: (0, 0)
    in_specs = [
        pl.BlockSpec(memory_space=pltpu.SMEM),
        pl.BlockSpec(memory_space=pltpu.SMEM),
        pl.BlockSpec((None, ts, D), lambda t: (jnp.minimum(t + 1, n_tiles - 1), 0, 0)),
        pl.BlockSpec((None, ts, D), lambda t: (jnp.maximum(t - 2, 0), 0, 0)),
        pl.BlockSpec((1, D), const2),
        pl.BlockSpec(memory_space=pl.ANY),
        pl.BlockSpec((1, D_ATTN), const2),
        pl.BlockSpec((1, 128), const2),
        pl.BlockSpec((256, 256), const2),
        pl.BlockSpec((CONV_WIDTH * BF16_SUBLANES, D_CONV), const2),
        pl.BlockSpec((1, D_CONV), const2),
        pl.BlockSpec((1, D_CONV), const2),
        pl.BlockSpec((1, D_CONV), const2),
        pl.BlockSpec(memory_space=pl.ANY),
        pl.BlockSpec((BLOCK, 2 * BLOCK), const2),
    ]
    scratch = [
        pltpu.VMEM((D, D_IN), _BF16),
        pltpu.VMEM((D, D), _BF16),
        pltpu.VMEM((2, W_ROWS, D_IN), _F32),
        pltpu.SemaphoreType.DMA((2,)),
        pltpu.VMEM((ts, D), _BF16),
        pltpu.VMEM((ts, UA0 - Q0), _F32),
        pltpu.VMEM((ts, ZC0 - UA0), _F32),
        pltpu.VMEM((ts, D_IN - ZC0), _F32),
        pltpu.VMEM((ts, D_ATTN + 128), _F32),
        pltpu.VMEM((2, ts, D_ATTN), _BF16),
        pltpu.VMEM((2, 4, ts + KV_HALO, 128), _BF16),
        pltpu.VMEM((2, 4, ts + KV_HALO, 256), _BF16),
        pltpu.VMEM((2 * N_SLABS, U_ROWS, 128), _F32),
        pltpu.VMEM((N_SLABS, N_UTILES * BF16_SUBLANES, 128), _BF16),
        pltpu.VMEM((2, ts, D_ATTN), _F32),
        pltpu.VMEM((2, ts, D_CONV), _F32),
        pltpu.VMEM((ts, D_CONV), _F32),
        pltpu.VMEM((N_SLABS, C_ROWS, 128), _F32),
        pltpu.VMEM((2, ts, D), _BF16),
        pltpu.VMEM((2, 4, 2 * BLOCK, 2 * BLOCK), _F32),
        pltpu.VMEM((ATTN_SLOTS, 2 * BLOCK, 2 * BLOCK), _F32),
        pltpu.VMEM((ATTN_SLOTS, 2, BLOCK, 128), _F32),
    ]
    out = pl.pallas_call(
        functools.partial(_layer_kernel, tiles_per_seq=S // ts, n_tiles=n_tiles),
        grid=(n_tiles + 2,),
        in_specs=in_specs,
        out_specs=pl.BlockSpec((None, ts, D), lambda t: (jnp.maximum(t - 2, 0), 0, 0)),
        out_shape=jax.ShapeDtypeStruct((n_tiles, ts, D), x.dtype),
        scratch_shapes=scratch,
        compiler_params=pltpu.CompilerParams(
            dimension_semantics=("arbitrary",),
            vmem_limit_bytes=VMEM_LIMIT_BYTES),
        name="hybrid_layer",
    )(sinks.astype(_F32), rel_bias.astype(_F32).reshape(NUM_BUCKETS * N_HEADS), xt, xt,
      norm_w.astype(_F32).reshape(1, D), w_in.astype(_F32), qw, kw, bd,
      wb, dw_b.astype(_F32).reshape(1, D_CONV), ln_w.astype(_F32).reshape(1, D_CONV),
      ln_b.astype(_F32).reshape(1, D_CONV), w_out.astype(_F32), jnp.asarray(_banded_buckets()))
    return out.reshape(B, S, D)


def kernel(x, norm_w, w_in, q_norm_w, k_norm_w, sinks, dw_w, dw_b, ln_w, ln_b, w_out, rel_bias):
    depth = norm_w.shape[0]
    for l in range(depth):
        x = _layer(x, norm_w[l], w_in[l], q_norm_w[l], k_norm_w[l], sinks[l], dw_w[l], dw_b[l],
                   ln_w[l], ln_b[l], w_out[l], rel_bias)
    return x
```

```python
import functools
import math

import numpy as np
import jax
import jax.numpy as jnp
from jax import lax
from jax.experimental import pallas as pl
from jax.experimental.pallas import tpu as pltpu

D_MODEL = 1024
D_ATTN = 512
D_CONV = 512
HEAD_DIM = 64
N_HEADS = 8
N_KV_HEADS = 2
BLOCK = 128
NUM_BUCKETS = 32
MAX_DISTANCE = 128
CONV_WIDTH = 31
D_IN = 2816
EPS = 1e-6
LN_EPS = 1e-5
NEG_INF = -1e30
LOG2E = math.log2(math.e)

SEQ_TILE = 512
KV_HALO = BLOCK
ROW_CHUNK = 64
SUBLANES = 8
N_SLABS = D_CONV // 128
BF16_SUBLANES = 16
N_VACC = 4
N_TBLK = BF16_SUBLANES
T_BLK = SEQ_TILE // N_TBLK
U_PITCH = T_BLK + 1
U_ROWS = ((N_TBLK + 1) * U_PITCH + 7) // 8 * 8
N_UTILES = CONV_WIDTH - 1 + T_BLK
C_PITCH = T_BLK + 8
C_ROWS = N_TBLK * C_PITCH
N_CHUNK = 256
W_ROWS = 128
W_SLOTS = 4
ATTN_SLOTS = 8
VMEM_LIMIT_BYTES = 56 * 1024 * 1024

Q0, K0, V0, ZA0, UA0, UG0, ZC0 = 0, 512, 640, 768, 1280, 1792, 2304

_F32 = jnp.float32
_BF16 = jnp.bfloat16


def _t5_bucket_table():
    qi = np.arange(BLOCK)[:, None]
    sj = np.arange(2 * BLOCK)[None, :]
    dist = qi + BLOCK - sj
    n = np.maximum(dist, 0)
    max_exact = NUM_BUCKETS // 2
    nf = np.maximum(n, 1).astype(np.float32)
    large = max_exact + (np.log(nf / max_exact) / math.log(MAX_DISTANCE / max_exact)
                         * (NUM_BUCKETS - max_exact)).astype(np.int32)
    large = np.minimum(large, NUM_BUCKETS - 1)
    bucket = np.where(n < max_exact, n, large).astype(np.int32)
    band = (dist >= 0) & (dist < BLOCK)
    return bucket, band


def _banded_buckets():
    bucket, band = _t5_bucket_table()
    return np.where(band, bucket, -1).astype(np.int32)


def _sigmoid(z):
    return 0.5 * jnp.tanh(0.5 * z) + 0.5


def _silu(z):
    h = 0.5 * z
    return h + h * jnp.tanh(h)


def _build_bias_tables(relb_ref, bucket_ref, bias_s):
    cur_keys = lax.broadcasted_iota(jnp.int32, (8, 2 * BLOCK), 1) >= BLOCK

    def body(c, carry):
        r0 = pl.multiple_of(c * 8, 8)
        bk = bucket_ref[pl.ds(r0, 8), :]
        accs = [jnp.full((8, 2 * BLOCK), NEG_INF, _F32) for _ in range(N_HEADS)]
        for b in range(NUM_BUCKETS):
            hit = bk == b
            for h in range(N_HEADS):
                accs[h] = jnp.where(hit, relb_ref[b * N_HEADS + h] * LOG2E, accs[h])
        for h in range(N_HEADS):
            g, rem = divmod(h, 4)
            j, e = divmod(rem, 2)
            rows = pl.ds(BLOCK * j + r0, 8)
            bias_s[0, 2 * g + e, rows, :] = accs[h]
            bias_s[1, 2 * g + e, rows, :] = jnp.where(cur_keys, accs[h], NEG_INF)
        return carry

    lax.fori_loop(0, BLOCK // 8, body, 0)


class _RawProj:
    bounds = (Q0, UA0, ZC0, D_IN)

    def __init__(self, *refs):
        self.refs = refs

    def _find(self, c0, c1):
        for ref, lo, hi in zip(self.refs, self.bounds[:-1], self.bounds[1:]):
            if lo <= c0 and c1 <= hi:
                return ref, c0 - lo, c1 - lo
        raise ValueError((c0, c1))

    def load(self, rows, c0, c1):
        ref, a, b = self._find(c0, c1)
        return ref[rows, a:b]

    def store(self, c0, c1, val):
        ref, a, b = self._find(c0, c1)
        ref[:, a:b] = val


def _row_chunks():
    return [slice(r, r + ROW_CHUNK) for r in range(0, SEQ_TILE, ROW_CHUNK)]


def _stream_a(sa, x_ref, x0_ref, nw_ref, win_ref, qw_ref, kw_ref, bd_ref,
              h_s, p_s, ssq_s, q_s, k_s, v_s, u_s, ga_s, gc_s):
    ts = SEQ_TILE

    def rms(src_ref, r):
        def run():
            xc = src_ref[r:r + ROW_CHUNK, :]
            ss = jnp.sum(xc * xc, axis=-1, keepdims=True)
            rs = lax.rsqrt(ss * (1.0 / D_MODEL) + EPS)
            h_s[r:r + ROW_CHUNK, :] = ((xc * rs) * nw_ref[...]).astype(_BF16)
        return run

    def dot(c0, c1):
        def run():
            p_s.store(c0, c1, jnp.dot(h_s[...], win_ref[:, c0:c1], preferred_element_type=_F32))
        return run

    def q_sumsq(half):
        def run():
            cs = slice(256 * half, 256 * half + 256)
            q = p_s.load(slice(None), Q0 + cs.start, Q0 + cs.stop)
            ssq_s[:, cs] = jnp.dot((q * q).astype(_BF16), bd_ref[...], preferred_element_type=_F32)
        return run

    def q_norm(half):
        def run():
            cs = slice(256 * half, 256 * half + 256)
            for rr in _row_chunks():
                q = p_s.load(rr, Q0 + cs.start, Q0 + cs.stop)
                qn =(q * lax.rsqrt(ssq_s[rr, cs] * (1.0 / HEAD_DIM) + EPS)) * qw_ref[:, cs]
                q_s[sa, rr, cs] = qn.astype(_BF16)
        return run

    def k_sumsq():
        k = p_s.load(slice(None), K0, K0 + 128)
        ssq_s[:, D_ATTN:D_ATTN + 128] = jnp.dot((k * k).astype(_BF16), bd_ref[0:128, 0:128],
                                                preferred_element_type=_F32)

    def kv_store():
        lo = lax.broadcasted_iota(jnp.int32, (ROW_CHUNK, 128), 1) < HEAD_DIM
        zero = jnp.zeros((ROW_CHUNK, 128), _F32)
        one = jnp.ones((ROW_CHUNK, 128), _F32)
        ones_lo = jnp.where(lo, one, zero).astype(_BF16)
        ones_hi = jnp.where(lo, zero, one).astype(_BF16)
        for rr in _row_chunks():
            cur = slice(KV_HALO + rr.start, KV_HALO + rr.stop)
            k = p_s.load(rr, K0, K0 + 128)
            v = p_s.load(rr, V0, V0 + 128)
            kn = (k * lax.rsqrt(ssq_s[rr, D_ATTN:D_ATTN + 128] * (1.0 / HEAD_DIM) + EPS)) * kw_ref[...]
            kn_sw = pltpu.roll(kn, 64, axis=1)
            v_sw = pltpu.roll(v, 64, axis=1)
            k_s[sa, 0, cur, :] = jnp.where(lo, kn, zero).astype(_BF16)
            k_s[sa, 1, cur, :] = jnp.where(lo, zero, kn_sw).astype(_BF16)
            k_s[sa, 2, cur, :] = jnp.where(lo, kn_sw, zero).astype(_BF16)
            k_s[sa, 3, cur, :] = jnp.where(lo, zero, kn).astype(_BF16)
            v_s[sa, 0, cur, 0:128] = jnp.where(lo, v, zero).astype(_BF16)
            v_s[sa, 1, cur, 0:128] = jnp.where(lo, zero, v_sw).astype(_BF16)
            v_s[sa, 2, cur, 0:128] = jnp.where(lo, v_sw, zero).astype(_BF16)
            v_s[sa, 3, cur, 0:128] = jnp.where(lo, zero, v).astype(_BF16)
            v_s[sa, 0, cur, 128:256] = ones_lo
            v_s[sa, 1, cur, 128:256] = ones_hi
            v_s[sa, 2, cur, 128:256] = ones_lo
            v_s[sa, 3, cur, 128:256] = ones_hi

    def gate(c0, dst, half):
        def run():
            cs = slice(256 * half, 256 * half + 256)
            for rr in _row_chunks():
                z = p_s.load(rr, c0 + 256 * half, c0 + 256 * half + 256)
                dst[sa, rr, cs] = _silu(z)
        return run

    def glu(half):
        def run():
            for blk in range(N_TBLK):
                tr = slice(T_BLK * blk, T_BLK * (blk + 1))
                a = p_s.load(tr, UA0 + 256 * half, UA0 + 256 * half + 256)
                g = p_s.load(tr, UG0 + 256 * half, UG0 + 256 * half + 256)
                u = a * _sigmoid(g)
                p0 = U_PITCH * (blk + 1)
                for sl in range(2):
                    u_s[sa * N_SLABS + 2 * half + sl, p0:p0 + T_BLK, :] = u[:, 128 * sl:128 * sl + 128]
        return run

    return dict(
        rms_next=[rms(x_ref, r) for r in range(0, ts, ROW_CHUNK)],
        rms_first=[rms(x0_ref, r) for r in range(0, ts, ROW_CHUNK)],
        dot_qkvz=dot(Q0, UA0), dot_ag=dot(UA0, ZC0), dot_zc=dot(ZC0, D_IN),
        q_sumsq=[q_sumsq(0), q_sumsq(1)], q_norm=[q_norm(0), q_norm(1)],
        k_sumsq=k_sumsq, kv_store=kv_store,
        gate_a=[gate(ZA0, ga_s, 0), gate(ZA0, ga_s, 1)],
        gate_c=[gate(ZC0, gc_s, 0), gate(ZC0, gc_s, 1)],
        glu=[glu(0), glu(1)])


def _stream_b(sb, first_b, sink_ref, wb_ref, dwb_ref, lnw_ref, lnb_ref,
              q_s, k_s, v_s, u_s, u2_s, ga_s, gc_s, c_s, yc_s, y_s, bias_s, o_s, dn_s):
    ts = SEQ_TILE
    taps = CONV_WIDTH - 1

    def pack(cb):
        def run():
            for k in range(N_UTILES):
                tau = k - taps
                base = U_PITCH + tau if tau >= 0 else T_BLK + tau
                halves = [u_s[sb * N_SLABS + cb, pl.ds(base + h * SUBLANES * U_PITCH, SUBLANES, stride=U_PITCH), :]
                          for h in range(N_TBLK // SUBLANES)]
                u2_s[cb, BF16_SUBLANES * k:BF16_SUBLANES * (k + 1), :] = (
                    jnp.concatenate(halves, axis=0).astype(_BF16))
        return run

    def conv(cb):
        def run():
            ls = slice(128 * cb, 128 * cb + 128)
            bv = jnp.broadcast_to(dwb_ref[:, ls], (BF16_SUBLANES, 128))
            wv = [wb_ref[BF16_SUBLANES * j:BF16_SUBLANES * (j + 1), ls].astype(_F32) for j in range(CONV_WIDTH)]
            for r0 in range(0, T_BLK, N_VACC):
                accs = [None] * N_VACC
                for k in range(r0, r0 + N_VACC + taps):
                    win = u2_s[cb, BF16_SUBLANES * k:BF16_SUBLANES * (k + 1), :].astype(_F32)
                    for n in range(N_VACC):
                        j = k - (r0 + n)
                        if 0 <= j < CONV_WIDTH:
                            term = win * wv[j]
                            accs[n] = term if accs[n] is None else accs[n] + term
                for n in range(N_VACC):
                    r = r0 + n
                    c_s[BF16_SUBLANES * r:BF16_SUBLANES * (r + 1), ls] = accs[n] + bv
        return run

    def ln(r):
        def run():
            y = c_s[r:r + ROW_CHUNK, :]
            mu = jnp.sum(y, axis=-1, keepdims=True) * (1.0 / D_CONV)
            d = y - mu
            var = jnp.sum(d * d, axis=-1, keepdims=True) * (1.0 / D_CONV)
            yn = (d * lax.rsqrt(var + LN_EPS)) * lnw_ref[...] + lnb_ref[...]
            act = _silu(yn)
            for n in range(ROW_CHUNK // SUBLANES):
                row = r + SUBLANES * n
                tau, blk0 = divmod(row, BF16_SUBLANES)
                for cb in range(N_SLABS):
                    yc_s[cb, pl.ds(C_PITCH * blk0 + tau, SUBLANES, stride=C_PITCH), :] = (
                        act[SUBLANES * n:SUBLANES * (n + 1), 128 * cb:128 * cb + 128])
        return run

    def regate(blk):
        def run():
            tr = slice(T_BLK * blk, T_BLK * (blk + 1))
            for cb in range(N_SLABS):
                ls = slice(128 * cb, 128 * cb + 128)
                val = yc_s[cb, C_PITCH * blk:C_PITCH * blk + T_BLK, :] * gc_s[sb, tr, ls]
                y_s[sb, tr, D_ATTN + 128 * cb:D_ATTN + 128 * cb + 128] = val.astype(_BF16)
        return run

    def attn_front(b, g):
        def run():
            slot = (2 * b + g) % ATTN_SLOTS
            rows = slice(BLOCK * b, BLOCK * b + BLOCK)
            krows = slice(BLOCK * b, BLOCK * b + 2 * BLOCK)
            tbl = first_b if b == 0 else 0
            lo_q = lax.broadcasted_iota(jnp.int32, (BLOCK, 128), 1) < HEAD_DIM
            c0 = 256 * g
            qg = jnp.concatenate([q_s[sb, rows, c0:c0 + 128], q_s[sb, rows, c0 + 128:c0 + 256]], axis=0)
            kk = jnp.concatenate([k_s[sb, 2 * g, krows, :], k_s[sb, 2 * g + 1, krows, :]], axis=0)
            s = lax.dot_general(qg, kk, (((1,), (1,)), ((), ())),
                                preferred_element_type=_F32)
            ps = [[None, None], [None, None]]
            ex = [[None, None], [None, None]]
            for e in range(2):
                se = s[:, 2 * BLOCK * e:2 * BLOCK * (e + 1)] + bias_s[tbl, 2 * g + e]
                for j in range(2):
                    sink = sink_ref[4 * g + 2 * j + e] * LOG2E
                    sj = se[BLOCK * j:BLOCK * j + BLOCK, :]
                    m = jnp.maximum(jnp.max(sj, axis=-1, keepdims=True), sink)
                    ps[j][e] = jnp.exp2(sj - m).astype(_BF16)
                    ex[j][e] = jnp.exp2(sink - m)
            p = jnp.concatenate([jnp.concatenate(ps[0], axis=1), jnp.concatenate(ps[1], axis=1)], axis=0)
            vv = jnp.concatenate([v_s[sb, 2 * g, krows, :], v_s[sb, 2 * g + 1, krows, :]], axis=0)
            o_s[slot] = jnp.dot(p, vv, preferred_element_type=_F32)
            for j in range(2):
                dn_s[slot, j] = jnp.where(lo_q, ex[j][0], ex[j][1])
        return run

    def attn_norm(b, g):
        def run():
            slot = (2 * b + g) % ATTN_SLOTS
            rows = slice(BLOCK * b, BLOCK * b + BLOCK)
            for j in range(2):
                oj = o_s[slot, BLOCK * j:BLOCK * j + BLOCK, :]
                den = oj[:, 128:256] + dn_s[slot, j]
                cs = slice(256 * g + 128 * j, 256 * g + 128 * j + 128)
                y_s[sb, rows, cs] = ((oj[:, 0:128] / den) * ga_s[sb, rows, cs]).astype(_BF16)
        return run

    pairs = [(b, g) for b in range(ts // BLOCK) for g in range(N_KV_HEADS)]
    return dict(attn_front=[attn_front(b, g) for b, g in pairs],
                attn_norm=[attn_norm(b, g) for b, g in pairs],
                pack=[pack(cb) for cb in range(N_SLABS)],
                conv=[conv(cb) for cb in range(N_SLABS)],
                ln=[ln(r) for r in range(0, ts, ROW_CHUNK)],
                regate=[regate(blk) for blk in range(N_TBLK)])


def _stream_c(sy, xres_ref, wout_ref, o_ref, y_s):
    def outproj(n):
        def run():
            o_ref[:, n:n + N_CHUNK] = xres_ref[:, n:n + N_CHUNK] + jnp.dot(
                y_s[sy], wout_ref[:, n:n + N_CHUNK], preferred_element_type=_F32)
        return run

    return [outproj(n) for n in range(0, D_MODEL, N_CHUNK)]


def _load_weight_bf16(w_hbm, w_s, stage, sem):
    rows, cols = w_s.shape
    n_chunks = rows // W_ROWS
    n_slots = stage.shape[0]

    def copy(c):
        slot = c % n_slots
        return pltpu.make_async_copy(w_hbm.at[pl.ds(c * W_ROWS, W_ROWS), :],
                                     stage.at[slot, :, pl.ds(0, cols)], sem.at[slot])

    for c in range(min(n_slots - 1, n_chunks)):
        copy(c).start()
    for c in range(n_chunks):
        nxt = c + n_slots - 1
        if nxt < n_chunks:
            copy(nxt).start()
        copy(c).wait()
        w_s[c * W_ROWS:(c + 1) * W_ROWS, :] = stage[c % n_slots, :, 0:cols].astype(_BF16)


def _layer_kernel(sink_ref, relb_ref, x_ref, xres_ref, nw_ref, win_hbm, qw_ref, kw_ref, bd_ref, wb_ref,
                  dwb_ref, lnw_ref, lnb_ref, wout_hbm, bucket_ref,
                  o_ref,
                  win_ref, wout_ref, wstage_s, wsem,
                  h_s, p1_s, p2_s, p3_s, ssq_s, q_s, k_s, v_s, u_s, u2_s, ga_s, gc_s, c_s, yc_s, y_s,
                  bias_s, o_s, dn_s, *, tiles_per_seq, n_tiles):
    p_s = _RawProj(p1_s, p2_s, p3_s)
    ts = SEQ_TILE
    t = pl.program_id(0)
    sa = lax.rem(t, 2)
    sb = 1 - sa
    first_a = lax.rem(t, tiles_per_seq) == 0
    first_b = jnp.where(lax.rem(t + tiles_per_seq - 1, tiles_per_seq) == 0, 1, 0)

    @pl.when(t == 0)
    def _():
        _load_weight_bf16(win_hbm, win_ref, wstage_s, wsem)
        _load_weight_bf16(wout_hbm, wout_ref, wstage_s, wsem)
        _build_bias_tables(relb_ref, bucket_ref, bias_s)
        y_s[1] = jnp.zeros(y_s.shape[1:], _BF16)

    @pl.when(first_a)
    def _():
        k_s[sa, :, 0:KV_HALO, :] = jnp.zeros((4, KV_HALO, 128), _BF16)
        v_s[sa, :, 0:KV_HALO, :] = jnp.zeros((4, KV_HALO, 256), _BF16)
        for cb in range(N_SLABS):
            u_s[sa * N_SLABS + cb, 0:T_BLK, :] = jnp.zeros((T_BLK, 128), _F32)

    @pl.when(jnp.logical_not(first_a))
    def _():
        k_s[sa, :, 0:KV_HALO, :] = k_s[sb, :, ts:ts + KV_HALO, :]
        v_s[sa, :, 0:KV_HALO, :] = v_s[sb, :, ts:ts + KV_HALO, :]
        for cb in range(N_SLABS):
            u_s[sa * N_SLABS + cb, 0:T_BLK, :] = u_s[sb * N_SLABS + cb,
                                                     N_TBLK * U_PITCH:N_TBLK * U_PITCH + T_BLK, :]

    A = _stream_a(sa, x_ref, xres_ref, nw_ref, win_ref, qw_ref, kw_ref, bd_ref,
                  h_s, p_s, ssq_s, q_s, k_s, v_s, u_s, ga_s, gc_s)
    Bm = _stream_b(sb, first_b, sink_ref, wb_ref, dwb_ref, lnw_ref, lnb_ref,
                   q_s, k_s, v_s, u_s, u2_s, ga_s, gc_s, c_s, yc_s, y_s, bias_s, o_s, dn_s)
    C = _stream_c(sa, xres_ref, wout_ref, o_ref, y_s)
    af, an = Bm["attn_front"], Bm["attn_norm"]

    regions = [
        Bm["pack"] + af + Bm["conv"] + an,
        Bm["ln"] + Bm["regate"] + [A["dot_qkvz"], A["dot_zc"]],
        A["q_sumsq"] + [A["k_sumsq"]] + A["gate_a"] + A["gate_c"] + A["q_norm"]
        + [A["kv_store"]] + [A["dot_ag"]],
        A["glu"] + C + A["rms_next"],
    ]
    fill_regions = [
        A["rms_first"],
        [A["dot_qkvz"], A["dot_zc"]],
        A["q_sumsq"] + [A["k_sumsq"]] + A["gate_a"] + A["gate_c"] + A["q_norm"]
        + [A["kv_store"]] + [A["dot_ag"]],
        A["glu"] + A["rms_next"],
    ]
    drain_regions = [
        Bm["pack"] + af + Bm["conv"] + an,
        Bm["ln"] + Bm["regate"],
        C,
    ]
    last_regions = [C]
    schedule = [(regions, (t >= 1) & (t < n_tiles)),
                (fill_regions, t == 0),
                (drain_regions, t == n_tiles),
                (last_regions, t == n_tiles + 1)]
    for group, active in schedule:
        trips = jnp.where(active, 1, 0)
        for region in group:
            def body(_, carry, region=region):
                for item in region:
                    item()
                return carry
            lax.fori_loop(0, trips, body, 0)


def _layer(x, norm_w, w_in, q_norm_w, k_norm_w, sinks, dw_w, dw_b, ln_w, ln_b, w_out, rel_bias):
    B, S, D = x.shape
    ts = SEQ_TILE
    assert D == D_MODEL and S % ts == 0 and ts % BLOCK == 0
    n_tiles = B * S // ts

    idx = np.arange(256) // HEAD_DIM
    bd = jnp.asarray((idx[:, None] == idx[None, :]).astype(np.float32), dtype=_BF16)
    qw = (jnp.tile(q_norm_w.astype(_F32), N_HEADS) * (HEAD_DIM ** -0.5 * LOG2E)).reshape(1, D_ATTN)
    kw = jnp.tile(k_norm_w.astype(_F32), N_KV_HEADS).reshape(1, 128)
    xt = x.reshape(n_tiles, ts, D)
    wb = jnp.broadcast_to(dw_w.astype(_BF16)[:, None, :], (CONV_WIDTH, BF16_SUBLANES, D_CONV))
    wb = wb.reshape(CONV_WIDTH * BF16_SUBLANES, D_CONV)

    const2 = lambda t: (0, 0)
    in_specs = [
        pl.BlockSpec(memory_space=pltpu.SMEM),
        pl.BlockSpec(memory_space=pltpu.SMEM),
        pl.BlockSpec((None, ts, D), lambda t: (jnp.minimum(t + 1, n_tiles - 1), 0, 0)),
        pl.BlockSpec((None, ts, D), lambda t: (jnp.maximum(t - 2, 0), 0, 0)),
        pl.BlockSpec((1, D), const2),
        pl.BlockSpec(memory_space=pl.ANY),
        pl.BlockSpec((1, D_ATTN), const2),
        pl.BlockSpec((1, 128), const2),
        pl.BlockSpec((256, 256), const2),
        pl.BlockSpec((CONV_WIDTH * BF16_SUBLANES, D_CONV), const2),
        pl.BlockSpec((1, D_CONV), const2),
        pl.BlockSpec((1, D_CONV), const2),
        pl.BlockSpec((1, D_CONV), const2),
        pl.BlockSpec(memory_space=pl.ANY),
        pl.BlockSpec((BLOCK, 2 * BLOCK), const2),
    ]
    scratch = [
        pltpu.VMEM((D, D_IN), _BF16),
        pltpu.VMEM((D, D), _BF16),
        pltpu.VMEM((W_SLOTS, W_ROWS, D_IN), _F32),
        pltpu.SemaphoreType.DMA((W_SLOTS,)),
        pltpu.VMEM((ts, D), _BF16),
        pltpu.VMEM((ts, UA0 - Q0), _F32),
        pltpu.VMEM((ts, ZC0 - UA0), _F32),
        pltpu.VMEM((ts, D_IN - ZC0), _F32),
        pltpu.VMEM((ts, D_ATTN + 128), _F32),
        pltpu.VMEM((2, ts, D_ATTN), _BF16),
        pltpu.VMEM((2, 4, ts + KV_HALO, 128), _BF16),
        pltpu.VMEM((2, 4, ts + KV_HALO, 256), _BF16),
        pltpu.VMEM((2 * N_SLABS, U_ROWS, 128), _F32),
        pltpu.VMEM((N_SLABS, N_UTILES * BF16_SUBLANES, 128), _BF16),
        pltpu.VMEM((2, ts, D_ATTN), _F32),
        pltpu.VMEM((2, ts, D_CONV), _F32),
        pltpu.VMEM((ts, D_CONV), _F32),
        pltpu.VMEM((N_SLABS, C_ROWS, 128), _F32),
        pltpu.VMEM((2, ts, D), _BF16),
        pltpu.VMEM((2, 4, 2 * BLOCK, 2 * BLOCK), _F32),
        pltpu.VMEM((ATTN_SLOTS, 2 * BLOCK, 2 * BLOCK), _F32),
        pltpu.VMEM((ATTN_SLOTS, 2, BLOCK, 128), _F32),
    ]
    out = pl.pallas_call(
        functools.partial(_layer_kernel, tiles_per_seq=S // ts, n_tiles=n_tiles),
        grid=(n_tiles + 2,),
        in_specs=in_specs,
        out_specs=pl.BlockSpec((None, ts, D), lambda t: (jnp.maximum(t - 2, 0), 0, 0)),
        out_shape=jax.ShapeDtypeStruct((n_tiles, ts, D), x.dtype),
        scratch_shapes=scratch,
        compiler_params=pltpu.CompilerParams(
            dimension_semantics=("arbitrary",),
            vmem_limit_bytes=VMEM_LIMIT_BYTES),
        name="hybrid_layer",
    )(sinks.astype(_F32), rel_bias.astype(_F32).reshape(NUM_BUCKETS * N_HEADS), xt, xt,
      norm_w.astype(_F32).reshape(1, D), w_in.astype(_F32), qw, kw, bd,
      wb, dw_b.astype(_F32).reshape(1, D_CONV), ln_w.astype(_F32).reshape(1, D_CONV),
      ln_b.astype(_F32).reshape(1, D_CONV), w_out.astype(_F32), jnp.asarray(_banded_buckets()))
    return out.reshape(B, S, D)


def kernel(x, norm_w, w_in, q_norm_w, k_norm_w, sinks, dw_w, dw_b, ln_w, ln_b, w_out, rel_bias):
    depth = norm_w.shape[0]
    for l in range(depth):
        x = _layer(x, norm_w[l], w_in[l], q_norm_w[l], k_norm_w[l], sinks[l], dw_w[l], dw_b[l],
                   ln_w[l], ln_b[l], w_out[l], rel_bias)
    return x
```

```python
import functools
import math

import numpy as np
import jax
import jax.numpy as jnp
from jax import lax
from jax.experimental import pallas as pl
from jax.experimental.pallas import tpu as pltpu

D_MODEL = 1024
D_ATTN = 512
D_CONV = 512
HEAD_DIM = 64
N_HEADS = 8
N_KV_HEADS = 2
BLOCK = 128
NUM_BUCKETS = 32
MAX_DISTANCE = 128
CONV_WIDTH = 31
D_IN = 2816
EPS = 1e-6
LN_EPS = 1e-5
NEG_INF = -1e30
LOG2E = math.log2(math.e)

SEQ_TILE = 512
KV_HALO = BLOCK
ROW_CHUNK = 64
SUBLANES = 8
N_SLABS = D_CONV // 128
BF16_SUBLANES = 16
N_VACC = 4
N_TBLK = BF16_SUBLANES
T_BLK = SEQ_TILE // N_TBLK
U_PITCH = T_BLK + 1
U_ROWS = ((N_TBLK + 1) * U_PITCH + 7) // 8 * 8
N_UTILES = CONV_WIDTH - 1 + T_BLK
C_PITCH = T_BLK + 8
C_ROWS = N_TBLK * C_PITCH
N_CHUNK = 256
W_ROWS = 128
W_SLOTS = 4
ATTN_SLOTS = 8
VMEM_LIMIT_BYTES = 56 * 1024 * 1024

Q0, K0, V0, ZA0, UA0, UG0, ZC0 = 0, 512, 640, 768, 1280, 1792, 2304

_F32 = jnp.float32
_BF16 = jnp.bfloat16


def _t5_bucket_table():
    qi = np.arange(BLOCK)[:, None]
    sj = np.arange(2 * BLOCK)[None, :]
    dist = qi + BLOCK - sj
    n = np.maximum(dist, 0)
    max_exact = NUM_BUCKETS // 2
    nf = np.maximum(n, 1).astype(np.float32)
    large = max_exact + (np.log(nf / max_exact) / math.log(MAX_DISTANCE / max_exact)
                         * (NUM_BUCKETS - max_exact)).astype(np.int32)
    large = np.minimum(large, NUM_BUCKETS - 1)
    bucket = np.where(n < max_exact, n, large).astype(np.int32)
    band = (dist >= 0) & (dist < BLOCK)
    return bucket, band


def _banded_buckets():
    bucket, band = _t5_bucket_table()
    return np.where(band, bucket, -1).astype(np.int32)


def _sigmoid(z):
    return 0.5 * jnp.tanh(0.5 * z) + 0.5


def _silu(z):
    h = 0.5 * z
    return h + h * jnp.tanh(h)


def _build_bias_tables(relb_ref, bucket_ref, bias_s):
    cur_keys = lax.broadcasted_iota(jnp.int32, (8, 2 * BLOCK), 1) >= BLOCK

    def body(c, carry):
        r0 = pl.multiple_of(c * 8, 8)
        bk = bucket_ref[pl.ds(r0, 8), :]
        accs = [jnp.full((8, 2 * BLOCK), NEG_INF, _F32) for _ in range(N_HEADS)]
        for b in range(NUM_BUCKETS):
            hit = bk == b
            for h in range(N_HEADS):
                accs[h] = jnp.where(hit, relb_ref[b * N_HEADS + h] * LOG2E, accs[h])
        for h in range(N_HEADS):
            g, rem = divmod(h, 4)
            j, e = divmod(rem, 2)
            rows = pl.ds(BLOCK * j + r0, 8)
            bias_s[0, 2 * g + e, rows, :] = accs[h]
            bias_s[1, 2 * g + e, rows, :] = jnp.where(cur_keys, accs[h], NEG_INF)
        return carry

    lax.fori_loop(0, BLOCK // 8, body, 0)


class _RawProj:
    bounds = (Q0, UA0, ZC0, D_IN)

    def __init__(self, *refs):
        self.refs = refs

    def _find(self, c0, c1):
        for ref, lo, hi in zip(self.refs, self.bounds[:-1], self.bounds[1:]):
            if lo <= c0 and c1 <= hi:
                return ref, c0 - lo, c1 - lo
        raise ValueError((c0, c1))

    def load(self, rows, c0, c1):
        ref, a, b = self._find(c0, c1)
        return ref[rows, a:b]

    def store(self, c0, c1, val):
        ref, a, b = self._find(c0, c1)
        ref[:, a:b] = val


def _row_chunks():
    return [slice(r, r + ROW_CHUNK) for r in range(0, SEQ_TILE, ROW_CHUNK)]


def _stream_a(sa, x_ref, x0_ref, nw_ref, win_ref, qw_ref, kw_ref, bd_ref,
              h_s, p_s, ssq_s, q_s, k_s, v_s, u_s, ga_s, gc_s):
    ts = SEQ_TILE

    def rms(src_ref, r):
        def run():
            xc = src_ref[r:r + ROW_CHUNK, :]
            ss = jnp.sum(xc * xc, axis=-1, keepdims=True)
            rs = lax.rsqrt(ss * (1.0 / D_MODEL) + EPS)
            h_s[r:r + ROW_CHUNK, :] = ((xc * rs) * nw_ref[...]).astype(_BF16)
        return run

    def dot(c0, c1):
        def run():
            p_s.store(c0, c1, jnp.dot(h_s[...], win_ref[:, c0:c1], preferred_element_type=_F32))
        return run

    def q_sumsq(half):
        def run():
            cs = slice(256 * half, 256 * half + 256)
            q = p_s.load(slice(None), Q0 + cs.start, Q0 + cs.stop)
            ssq_s[:, cs] = jnp.dot((q * q).astype(_BF16), bd_ref[...], preferred_element_type=_F32)
        return run

    def q_norm(half):
        def run():
            cs = slice(256 * half, 256 * half + 256)
            for rr in _row_chunks():
                q = p_s.load(rr, Q0 + cs.start, Q0 + cs.stop)
                qn =(q * lax.rsqrt(ssq_s[rr, cs] * (1.0 / HEAD_DIM) + EPS)) * qw_ref[:, cs]
                q_s[sa, rr, cs] = qn.astype(_BF16)
        return run

    def k_sumsq():
        k = p_s.load(slice(None), K0, K0 + 128)
        ssq_s[:, D_ATTN:D_ATTN + 128] = jnp.dot((k * k).astype(_BF16), bd_ref[0:128, 0:128],
                                                preferred_element_type=_F32)

    def kv_store():
        lo = lax.broadcasted_iota(jnp.int32, (ROW_CHUNK, 128), 1) < HEAD_DIM
        zero = jnp.zeros((ROW_CHUNK, 128), _F32)
        one = jnp.ones((ROW_CHUNK, 128), _F32)
        ones_lo = jnp.where(lo, one, zero).astype(_BF16)
        ones_hi = jnp.where(lo, zero, one).astype(_BF16)
        for rr in _row_chunks():
            cur = slice(KV_HALO + rr.start, KV_HALO + rr.stop)
            k = p_s.load(rr, K0, K0 + 128)
            v = p_s.load(rr, V0, V0 + 128)
            kn = (k * lax.rsqrt(ssq_s[rr, D_ATTN:D_ATTN + 128] * (1.0 / HEAD_DIM) + EPS)) * kw_ref[...]
            kn_sw = pltpu.roll(kn, 64, axis=1)
            v_sw = pltpu.roll(v, 64, axis=1)
            k_s[sa, 0, cur, :] = jnp.where(lo, kn, zero).astype(_BF16)
            k_s[sa, 1, cur, :] = jnp.where(lo, zero, kn_sw).astype(_BF16)
            k_s[sa, 2, cur, :] = jnp.where(lo, kn_sw, zero).astype(_BF16)
            k_s[sa, 3, cur, :] = jnp.where(lo, zero, kn).astype(_BF16)
            v_s[sa, 0, cur, 0:128] = jnp.where(lo, v, zero).astype(_BF16)
            v_s[sa, 1, cur, 0:128] = jnp.where(lo, zero, v_sw).astype(_BF16)
            v_s[sa, 2, cur, 0:128] = jnp.where(lo, v_sw, zero).astype(_BF16)
            v_s[sa, 3, cur, 0:128] = jnp.where(lo, zero, v).astype(_BF16)
            v_s[sa, 0, cur, 128:256] = ones_lo
            v_s[sa, 1, cur, 128:256] = ones_hi
            v_s[sa, 2, cur, 128:256] = ones_lo
            v_s[sa, 3, cur, 128:256] = ones_hi

    def gate(c0, dst, half):
        def run():
            cs = slice(256 * half, 256 * half + 256)
            for rr in _row_chunks():
                z = p_s.load(rr, c0 + 256 * half, c0 + 256 * half + 256)
                dst[sa, rr, cs] = _silu(z)
        return run

    def glu(half):
        def run():
            for blk in range(N_TBLK):
                tr = slice(T_BLK * blk, T_BLK * (blk + 1))
                a = p_s.load(tr, UA0 + 256 * half, UA0 + 256 * half + 256)
                g = p_s.load(tr, UG0 + 256 * half, UG0 + 256 * half + 256)
                u = a * _sigmoid(g)
                p0 = U_PITCH * (blk + 1)
                for sl in range(2):
                    u_s[sa * N_SLABS + 2 * half + sl, p0:p0 + T_BLK, :] = u[:, 128 * sl:128 * sl + 128]
        return run

    return dict(
        rms_next=[rms(x_ref, r) for r in range(0, ts, ROW_CHUNK)],
        rms_first=[rms(x0_ref, r) for r in range(0, ts, ROW_CHUNK)],
        dot_qkvz=dot(Q0, UA0), dot_ag=dot(UA0, ZC0), dot_zc=dot(ZC0, D_IN),
        q_sumsq=[q_sumsq(0), q_sumsq(1)], q_norm=[q_norm(0), q_norm(1)],
        k_sumsq=k_sumsq, kv_store=kv_store,
        gate_a=[gate(ZA0, ga_s, 0), gate(ZA0, ga_s, 1)],
        gate_c=[gate(ZC0, gc_s, 0), gate(ZC0, gc_s, 1)],
        glu=[glu(0), glu(1)])


def _stream_b(sb, first_b, sink_ref, wb_ref, dwb_ref, lnw_ref, lnb_ref,
              q_s, k_s, v_s, u_s, u2_s, ga_s, gc_s, c_s, yc_s, y_s, bias_s, o_s, dn_s):
    ts = SEQ_TILE
    taps = CONV_WIDTH - 1

    def pack(cb):
        def run():
            for k in range(N_UTILES):
                tau = k - taps
                base = U_PITCH + tau if tau >= 0 else T_BLK + tau
                halves = [u_s[sb * N_SLABS + cb, pl.ds(base + h * SUBLANES * U_PITCH, SUBLANES, stride=U_PITCH), :]
                          for h in range(N_TBLK // SUBLANES)]
                u2_s[cb, BF16_SUBLANES * k:BF16_SUBLANES * (k + 1), :] = (
                    jnp.concatenate(halves, axis=0).astype(_BF16))
        return run

    def conv(cb):
        def run():
            ls = slice(128 * cb, 128 * cb + 128)
            bv = jnp.broadcast_to(dwb_ref[:, ls], (BF16_SUBLANES, 128))
            wv = [wb_ref[BF16_SUBLANES * j:BF16_SUBLANES * (j + 1), ls].astype(_F32) for j in range(CONV_WIDTH)]
            for r0 in range(0, T_BLK, N_VACC):
                accs = [None] * N_VACC
                for k in range(r0, r0 + N_VACC + taps):
                    win = u2_s[cb, BF16_SUBLANES * k:BF16_SUBLANES * (k + 1), :].astype(_F32)
                    for n in range(N_VACC):
                        j = k - (r0 + n)
                        if 0 <= j < CONV_WIDTH:
                            term = win * wv[j]
                            accs[n] = term if accs[n] is None else accs[n] + term
                for n in range(N_VACC):
                    r = r0 + n
                    c_s[BF16_SUBLANES * r:BF16_SUBLANES * (r + 1), ls] = accs[n] + bv
        return run

    def ln(r):
        def run():
            y = c_s[r:r + ROW_CHUNK, :]
            mu = jnp.sum(y, axis=-1, keepdims=True) * (1.0 / D_CONV)
            d = y - mu
            var = jnp.sum(d * d, axis=-1, keepdims=True) * (1.0 / D_CONV)
            yn = (d * lax.rsqrt(var + LN_EPS)) * lnw_ref[...] + lnb_ref[...]
            act = _silu(yn)
            for n in range(ROW_CHUNK // SUBLANES):
                row = r + SUBLANES * n
                tau, blk0 = divmod(row, BF16_SUBLANES)
                for cb in range(N_SLABS):
                    yc_s[cb, pl.ds(C_PITCH * blk0 + tau, SUBLANES, stride=C_PITCH), :] = (
                        act[SUBLANES * n:SUBLANES * (n + 1), 128 * cb:128 * cb + 128])
        return run

    def regate(blk):
        def run():
            tr = slice(T_BLK * blk, T_BLK * (blk + 1))
            for cb in range(N_SLABS):
                ls = slice(128 * cb, 128 * cb + 128)
                val = yc_s[cb, C_PITCH * blk:C_PITCH * blk + T_BLK, :] * gc_s[sb, tr, ls]
                y_s[sb, tr, D_ATTN + 128 * cb:D_ATTN + 128 * cb + 128] = val.astype(_BF16)
        return run

    def attn_front(b, g):
        def run():
            slot = (2 * b + g) % ATTN_SLOTS
            rows = slice(BLOCK * b, BLOCK * b + BLOCK)
            krows = slice(BLOCK * b, BLOCK * b + 2 * BLOCK)
            tbl = first_b if b == 0 else 0
            lo_q = lax.broadcasted_iota(jnp.int32, (BLOCK, 128), 1) < HEAD_DIM
            c0 = 256 * g
            qg = jnp.concatenate([q_s[sb, rows, c0:c0 + 128], q_s[sb, rows, c0 + 128:c0 + 256]], axis=0)
            kk = jnp.concatenate([k_s[sb, 2 * g, krows, :], k_s[sb, 2 * g + 1, krows, :]], axis=0)
            s = lax.dot_general(qg, kk, (((1,), (1,)), ((), ())),
                                preferred_element_type=_F32)
            ps = [[None, None], [None, None]]
            ex = [[None, None], [None, None]]
            for e in range(2):
                se = s[:, 2 * BLOCK * e:2 * BLOCK * (e + 1)] + bias_s[tbl, 2 * g + e]
                for j in range(2):
                    sink = sink_ref[4 * g + 2 * j + e] * LOG2E
                    sj = se[BLOCK * j:BLOCK * j + BLOCK, :]
                    m = jnp.maximum(jnp.max(sj, axis=-1, keepdims=True), sink)
                    ps[j][e] = jnp.exp2(sj - m).astype(_BF16)
                    ex[j][e] = jnp.exp2(sink - m)
            p = jnp.concatenate([jnp.concatenate(ps[0], axis=1), jnp.concatenate(ps[1], axis=1)], axis=0)
            vv = jnp.concatenate([v_s[sb, 2 * g, krows, :], v_s[sb, 2 * g + 1, krows, :]], axis=0)
            o_s[slot] = jnp.dot(p, vv, preferred_element_type=_F32)
            for j in range(2):
                dn_s[slot, j] = jnp.where(lo_q, ex[j][0], ex[j][1])
        return run

    def attn_norm(b, g):
        def run():
            slot = (2 * b + g) % ATTN_SLOTS
            rows = slice(BLOCK * b, BLOCK * b + BLOCK)
            for j in range(2):
                oj = o_s[slot, BLOCK * j:BLOCK * j + BLOCK, :]
                den = oj[:, 128:256] + dn_s[slot, j]
                cs = slice(256 * g + 128 * j, 256 * g + 128 * j + 128)
                y_s[sb, rows, cs] = ((oj[:, 0:128] / den) * ga_s[sb, rows, cs]).astype(_BF16)
        return run

    pairs = [(b, g) for b in range(ts // BLOCK) for g in range(N_KV_HEADS)]
    return dict(attn_front=[attn_front(b, g) for b, g in pairs],
                attn_norm=[attn_norm(b, g) for b, g in pairs],
                pack=[pack(cb) for cb in range(N_SLABS)],
                conv=[conv(cb) for cb in range(N_SLABS)],
                ln=[ln(r) for r in range(0, ts, ROW_CHUNK)],
                regate=[regate(blk) for blk in range(N_TBLK)])


def _stream_c(sy, xres_ref, wout_ref, o_ref, y_s):
    def outproj(n):
        def run():
            o_ref[:, n:n + N_CHUNK] = xres_ref[:, n:n + N_CHUNK] + jnp.dot(
                y_s[sy], wout_ref[:, n:n + N_CHUNK], preferred_element_type=_F32)
        return run

    return [outproj(n) for n in range(0, D_MODEL, N_CHUNK)]


def _load_weight_bf16(w_hbm, w_s, stage, sem):
    rows, cols = w_s.shape
    n_chunks = rows // W_ROWS
    n_slots = stage.shape[0]

    def copy(c):
        slot = c % n_slots
        return pltpu.make_async_copy(w_hbm.at[pl.ds(c * W_ROWS, W_ROWS), :],
                                     stage.at[slot, :, pl.ds(0, cols)], sem.at[slot])

    for c in range(min(n_slots - 1, n_chunks)):
        copy(c).start()
    for c in range(n_chunks):
        nxt = c + n_slots - 1
        if nxt < n_chunks:
            copy(nxt).start()
        copy(c).wait()
        w_s[c * W_ROWS:(c + 1) * W_ROWS, :] = stage[c % n_slots, :, 0:cols].astype(_BF16)


def _layer_kernel(sink_ref, relb_ref, x_ref, xres_ref, nw_ref, win_hbm, qw_ref, kw_ref, bd_ref, wb_ref,
                  dwb_ref, lnw_ref, lnb_ref, wout_hbm, bucket_ref,
                  o_ref,
                  win_ref, wout_ref, wstage_s, wsem,
                  h_s, p1_s, p2_s, p3_s, ssq_s, q_s, k_s, v_s, u_s, u2_s, ga_s, gc_s, c_s, yc_s, y_s,
                  bias_s, o_s, dn_s, *, tiles_per_seq, n_tiles):
    p_s = _RawProj(p1_s, p2_s, p3_s)
    ts = SEQ_TILE
    t = pl.program_id(0)
    sa = lax.rem(t, 2)
    sb = 1 - sa
    first_a = lax.rem(t, tiles_per_seq) == 0
    first_b = jnp.where(lax.rem(t + tiles_per_seq - 1, tiles_per_seq) == 0, 1, 0)

    @pl.when(t == 0)
    def _():
        _load_weight_bf16(win_hbm, win_ref, wstage_s, wsem)
        _load_weight_bf16(wout_hbm, wout_ref, wstage_s, wsem)
        _build_bias_tables(relb_ref, bucket_ref, bias_s)
        y_s[1] = jnp.zeros(y_s.shape[1:], _BF16)

    @pl.when(first_a)
    def _():
        k_s[sa, :, 0:KV_HALO, :] = jnp.zeros((4, KV_HALO, 128), _BF16)
        v_s[sa, :, 0:KV_HALO, :] = jnp.zeros((4, KV_HALO, 256), _BF16)
        for cb in range(N_SLABS):
            u_s[sa * N_SLABS + cb, 0:T_BLK, :] = jnp.zeros((T_BLK, 128), _F32)

    @pl.when(jnp.logical_not(first_a))
    def _():
        k_s[sa, :, 0:KV_HALO, :] = k_s[sb, :, ts:ts + KV_HALO, :]
        v_s[sa, :, 0:KV_HALO, :] = v_s[sb, :, ts:ts + KV_HALO, :]
        for cb in range(N_SLABS):
            u_s[sa * N_SLABS + cb, 0:T_BLK, :] = u_s[sb * N_SLABS + cb,
                                                     N_TBLK * U_PITCH:N_TBLK * U_PITCH + T_BLK, :]

    A = _stream_a(sa, x_ref, xres_ref, nw_ref, win_ref, qw_ref, kw_ref, bd_ref,
                  h_s, p_s, ssq_s, q_s, k_s, v_s, u_s, ga_s, gc_s)
    Bm = _stream_b(sb, first_b, sink_ref, wb_ref, dwb_ref, lnw_ref, lnb_ref,
                   q_s, k_s, v_s, u_s, u2_s, ga_s, gc_s, c_s, yc_s, y_s, bias_s, o_s, dn_s)
    C = _stream_c(sa, xres_ref, wout_ref, o_ref, y_s)
    af, an = Bm["attn_front"], Bm["attn_norm"]

    regions = [
        Bm["pack"] + af + Bm["conv"] + an,
        Bm["ln"] + Bm["regate"] + [A["dot_qkvz"], A["dot_zc"]]
        + A["q_sumsq"] + [A["k_sumsq"]] + A["gate_a"] + A["gate_c"] + A["q_norm"]
        + [A["kv_store"]] + [A["dot_ag"]] + C + A["glu"] + A["rms_next"],
    ]
    fill_regions = [
        A["rms_first"],
        [A["dot_qkvz"], A["dot_zc"]],
        A["q_sumsq"] + [A["k_sumsq"]] + A["gate_a"] + A["gate_c"] + A["q_norm"]
        + [A["kv_store"]] + [A["dot_ag"]],
        A["glu"] + A["rms_next"],
    ]
    drain_regions = [
        Bm["pack"] + af + Bm["conv"] + an,
        Bm["ln"] + Bm["regate"],
        C,
    ]
    last_regions = [C]
    schedule = [(regions, (t >= 1) & (t < n_tiles)),
                (fill_regions, t == 0),
                (drain_regions, t == n_tiles),
                (last_regions, t == n_tiles + 1)]
    for group, active in schedule:
        trips = jnp.where(active, 1, 0)
        for region in group:
            def body(_, carry, region=region):
                for item in region:
                    item()
                return carry
            lax.fori_loop(0, trips, body, 0)


def _layer(x, norm_w, w_in, q_norm_w, k_norm_w, sinks, dw_w, dw_b, ln_w, ln_b, w_out, rel_bias):
    B, S, D = x.shape
    ts = SEQ_TILE
    assert D == D_MODEL and S % ts == 0 and ts % BLOCK == 0
    n_tiles = B * S // ts

    idx = np.arange(256) // HEAD_DIM
    bd = jnp.asarray((idx[:, None] == idx[None, :]).astype(np.float32), dtype=_BF16)
    qw = (jnp.tile(q_norm_w.astype(_F32), N_HEADS) * (HEAD_DIM ** -0.5 * LOG2E)).reshape(1, D_ATTN)
    kw = jnp.tile(k_norm_w.astype(_F32), N_KV_HEADS).reshape(1, 128)
    xt = x.reshape(n_tiles, ts, D)
    wb = jnp.broadcast_to(dw_w.astype(_BF16)[:, None, :], (CONV_WIDTH, BF16_SUBLANES, D_CONV))
    wb = wb.reshape(CONV_WIDTH * BF16_SUBLANES, D_CONV)

    const2 = lambda t: (0, 0)
    in_specs = [
        pl.BlockSpec(memory_space=pltpu.SMEM),
        pl.BlockSpec(memory_space=pltpu.SMEM),
        pl.BlockSpec((None, ts, D), lambda t: (jnp.minimum(t + 1, n_tiles - 1), 0, 0)),
        pl.BlockSpec((None, ts, D), lambda t: (jnp.maximum(t - 2, 0), 0, 0)),
        pl.BlockSpec((1, D), const2),
        pl.BlockSpec(memory_space=pl.ANY),
        pl.BlockSpec((1, D_ATTN), const2),
        pl.BlockSpec((1, 128), const2),
        pl.BlockSpec((256, 256), const2),
        pl.BlockSpec((CONV_WIDTH * BF16_SUBLANES, D_CONV), const2),
        pl.BlockSpec((1, D_CONV), const2),
        pl.BlockSpec((1, D_CONV), const2),
        pl.BlockSpec((1, D_CONV), const2),
        pl.BlockSpec(memory_space=pl.ANY),
        pl.BlockSpec((BLOCK, 2 * BLOCK), const2),
    ]
    scratch = [
        pltpu.VMEM((D, D_IN), _BF16),
        pltpu.VMEM((D, D), _BF16),
        pltpu.VMEM((W_SLOTS, W_ROWS, D_IN), _F32),
        pltpu.SemaphoreType.DMA((W_SLOTS,)),
        pltpu.VMEM((ts, D), _BF16),
        pltpu.VMEM((ts, UA0 - Q0), _F32),
        pltpu.VMEM((ts, ZC0 - UA0), _F32),
        pltpu.VMEM((ts, D_IN - ZC0), _F32),
        pltpu.VMEM((ts, D_ATTN + 128), _F32),
        pltpu.VMEM((2, ts, D_ATTN), _BF16),
        pltpu.VMEM((2, 4, ts + KV_HALO, 128), _BF16),
        pltpu.VMEM((2, 4, ts + KV_HALO, 256), _BF16),
        pltpu.VMEM((2 * N_SLABS, U_ROWS, 128), _F32),
        pltpu.VMEM((N_SLABS, N_UTILES * BF16_SUBLANES, 128), _BF16),
        pltpu.VMEM((2, ts, D_ATTN), _F32),
        pltpu.VMEM((2, ts, D_CONV), _F32),
        pltpu.VMEM((ts, D_CONV), _F32),
        pltpu.VMEM((N_SLABS, C_ROWS, 128), _F32),
        pltpu.VMEM((2, ts, D), _BF16),
        pltpu.VMEM((2, 4, 2 * BLOCK, 2 * BLOCK), _F32),
        pltpu.VMEM((ATTN_SLOTS, 2 * BLOCK, 2 * BLOCK), _F32),
        pltpu.VMEM((ATTN_SLOTS, 2, BLOCK, 128), _F32),
    ]
    out = pl.pallas_call(
        functools.partial(_layer_kernel, tiles_per_seq=S // ts, n_tiles=n_tiles),
        grid=(n_tiles + 2,),
        in_specs=in_specs,
        out_specs=pl.BlockSpec((None, ts, D), lambda t: (jnp.maximum(t - 2, 0), 0, 0)),
        out_shape=jax.ShapeDtypeStruct((n_tiles, ts, D), x.dtype),
        scratch_shapes=scratch,
        compiler_params=pltpu.CompilerParams(
            dimension_semantics=("arbitrary",),
            vmem_limit_bytes=VMEM_LIMIT_BYTES),
        name="hybrid_layer",
    )(sinks.astype(_F32), rel_bias.astype(_F32).reshape(NUM_BUCKETS * N_HEADS), xt, xt,
      norm_w.astype(_F32).reshape(1, D), w_in.astype(_F32), qw, kw, bd,
      wb, dw_b.astype(_F32).reshape(1, D_CONV), ln_w.astype(_F32).reshape(1, D_CONV),
      ln_b.astype(_F32).reshape(1, D_CONV), w_out.astype(_F32), jnp.asarray(_banded_buckets()))
    return out.reshape(B, S, D)


def kernel(x, norm_w, w_in, q_norm_w, k_norm_w, sinks, dw_w, dw_b, ln_w, ln_b, w_out, rel_bias):
    depth = norm_w.shape[0]
    for l in range(depth):
        x = _layer(x, norm_w[l], w_in[l], q_norm_w[l], k_norm_w[l], sinks[l], dw_w[l], dw_b[l],
                   ln_w[l], ln_b[l], w_out[l], rel_bias)
    return x
```

```python
import functools
import math

import numpy as np
import jax
import jax.numpy as jnp
from jax import lax
from jax.experimental import pallas as pl
from jax.experimental.pallas import tpu as pltpu

D_MODEL = 1024
D_ATTN = 512
D_CONV = 512
HEAD_DIM = 64
N_HEADS = 8
N_KV_HEADS = 2
BLOCK = 128
NUM_BUCKETS = 32
MAX_DISTANCE = 128
CONV_WIDTH = 31
D_IN = 2816
EPS = 1e-6
LN_EPS = 1e-5
NEG_INF = -1e30
LOG2E = math.log2(math.e)

SEQ_TILE = 512
KV_HALO = BLOCK
ROW_CHUNK = 64
SUBLANES = 8
N_SLABS = D_CONV // 128
BF16_SUBLANES = 16
N_VACC = 4
N_TBLK = BF16_SUBLANES
T_BLK = SEQ_TILE // N_TBLK
U_PITCH = T_BLK + 1
U_ROWS = ((N_TBLK + 1) * U_PITCH + 7) // 8 * 8
N_UTILES = CONV_WIDTH - 1 + T_BLK
C_PITCH = T_BLK + 8
C_ROWS = N_TBLK * C_PITCH
N_CHUNK = 256
W_ROWS = 128
W_SLOTS = 4
ATTN_SLOTS = 8
VMEM_LIMIT_BYTES = 56 * 1024 * 1024

Q0, K0, V0, ZA0, UA0, UG0, ZC0 = 0, 512, 640, 768, 1280, 1792, 2304

_F32 = jnp.float32
_BF16 = jnp.bfloat16


def _t5_bucket_table():
    qi = np.arange(BLOCK)[:, None]
    sj = np.arange(2 * BLOCK)[None, :]
    dist = qi + BLOCK - sj
    n = np.maximum(dist, 0)
    max_exact = NUM_BUCKETS // 2
    nf = np.maximum(n, 1).astype(np.float32)
    large = max_exact + (np.log(nf / max_exact) / math.log(MAX_DISTANCE / max_exact)
                         * (NUM_BUCKETS - max_exact)).astype(np.int32)
    large = np.minimum(large, NUM_BUCKETS - 1)
    bucket = np.where(n < max_exact, n, large).astype(np.int32)
    band = (dist >= 0) & (dist < BLOCK)
    return bucket, band


def _banded_buckets():
    bucket, band = _t5_bucket_table()
    return np.where(band, bucket, -1).astype(np.int32)


def _sigmoid(z):
    return 0.5 * jnp.tanh(0.5 * z) + 0.5


def _silu(z):
    h = 0.5 * z
    return h + h * jnp.tanh(h)


def _build_bias_tables(relb_ref, bucket_ref, bias_s):
    cur_keys = lax.broadcasted_iota(jnp.int32, (8, 2 * BLOCK), 1) >= BLOCK

    def body(c, carry):
        r0 = pl.multiple_of(c * 8, 8)
        bk = bucket_ref[pl.ds(r0, 8), :]
        accs = [jnp.full((8, 2 * BLOCK), NEG_INF, _F32) for _ in range(N_HEADS)]
        for b in range(NUM_BUCKETS):
            hit = bk == b
            for h in range(N_HEADS):
                accs[h] = jnp.where(hit, relb_ref[b, h] * LOG2E, accs[h])
        for h in range(N_HEADS):
            g, rem = divmod(h, 4)
            j, e = divmod(rem, 2)
            rows = pl.ds(BLOCK * j + r0, 8)
            bias_s[0, 2 * g + e, rows, :] = accs[h]
            bias_s[1, 2 * g + e, rows, :] = jnp.where(cur_keys, accs[h], NEG_INF)
        return carry

    lax.fori_loop(0, BLOCK // 8, body, 0)


class _RawProj:
    bounds = (Q0, UA0, ZC0, D_IN)

    def __init__(self, *refs):
        self.refs = refs

    def _find(self, c0, c1):
        for ref, lo, hi in zip(self.refs, self.bounds[:-1], self.bounds[1:]):
            if lo <= c0 and c1 <= hi:
                return ref, c0 - lo, c1 - lo
        raise ValueError((c0, c1))

    def load(self, rows, c0, c1):
        ref, a, b = self._find(c0, c1)
        return ref[rows, a:b]

    def store(self, c0, c1, val):
        ref, a, b = self._find(c0, c1)
        ref[:, a:b] = val


def _sink(sink_ref, layer, head):
    return sink_ref[layer, head]


def _row_chunks():
    return [slice(r, r + ROW_CHUNK) for r in range(0, SEQ_TILE, ROW_CHUNK)]


def _stream_a(sa, x_ref, x0_ref, nw_ref, win_ref, qw_ref, kw_ref, bd_ref,
              h_s, p_s, ssq_s, q_s, k_s, v_s, u_s, ga_s, gc_s):
    ts = SEQ_TILE

    def rms(src_ref, r):
        def run():
            xc = src_ref[r:r + ROW_CHUNK, :]
            ss = jnp.sum(xc * xc, axis=-1, keepdims=True)
            rs = lax.rsqrt(ss * (1.0 / D_MODEL) + EPS)
            h_s[r:r + ROW_CHUNK, :] = ((xc * rs) * nw_ref[...]).astype(_BF16)
        return run

    def dot(c0, c1):
        def run():
            p_s.store(c0, c1, jnp.dot(h_s[...], win_ref[:, c0:c1], preferred_element_type=_F32))
        return run

    def q_sumsq(half):
        def run():
            cs = slice(256 * half, 256 * half + 256)
            q = p_s.load(slice(None), Q0 + cs.start, Q0 + cs.stop)
            ssq_s[:, cs] = jnp.dot((q * q).astype(_BF16), bd_ref[...], preferred_element_type=_F32)
        return run

    def q_norm(half):
        def run():
            cs = slice(256 * half, 256 * half + 256)
            for rr in _row_chunks():
                q = p_s.load(rr, Q0 + cs.start, Q0 + cs.stop)
                qn =(q * lax.rsqrt(ssq_s[rr, cs] * (1.0 / HEAD_DIM) + EPS)) * qw_ref[:, cs]
                q_s[sa, rr, cs] = qn.astype(_BF16)
        return run

    def k_sumsq():
        k = p_s.load(slice(None), K0, K0 + 128)
        ssq_s[:, D_ATTN:D_ATTN + 128] = jnp.dot((k * k).astype(_BF16), bd_ref[0:128, 0:128],
                                                preferred_element_type=_F32)

    def kv_store():
        lo = lax.broadcasted_iota(jnp.int32, (ROW_CHUNK, 128), 1) < HEAD_DIM
        zero = jnp.zeros((ROW_CHUNK, 128), _F32)
        one = jnp.ones((ROW_CHUNK, 128), _F32)
        ones_lo = jnp.where(lo, one, zero).astype(_BF16)
        ones_hi = jnp.where(lo, zero, one).astype(_BF16)
        for rr in _row_chunks():
            cur = slice(KV_HALO + rr.start, KV_HALO + rr.stop)
            k = p_s.load(rr, K0, K0 + 128)
            v = p_s.load(rr, V0, V0 + 128)
            kn = (k * lax.rsqrt(ssq_s[rr, D_ATTN:D_ATTN + 128] * (1.0 / HEAD_DIM) + EPS)) * kw_ref[...]
            kn_sw = pltpu.roll(kn, 64, axis=1)
            v_sw = pltpu.roll(v, 64, axis=1)
            k_s[sa, 0, cur, :] = jnp.where(lo, kn, zero).astype(_BF16)
            k_s[sa, 1, cur, :] = jnp.where(lo, zero, kn_sw).astype(_BF16)
            k_s[sa, 2, cur, :] = jnp.where(lo, kn_sw, zero).astype(_BF16)
            k_s[sa, 3, cur, :] = jnp.where(lo, zero, kn).astype(_BF16)
            v_s[sa, 0, cur, 0:128] = jnp.where(lo, v, zero).astype(_BF16)
            v_s[sa, 1, cur, 0:128] = jnp.where(lo, zero, v_sw).astype(_BF16)
            v_s[sa, 2, cur, 0:128] = jnp.where(lo, v_sw, zero).astype(_BF16)
            v_s[sa, 3, cur, 0:128] = jnp.where(lo, zero, v).astype(_BF16)
            v_s[sa, 0, cur, 128:256] = ones_lo
            v_s[sa, 1, cur, 128:256] = ones_hi
            v_s[sa, 2, cur, 128:256] = ones_lo
            v_s[sa, 3, cur, 128:256] = ones_hi

    def gate(c0, dst, half):
        def run():
            cs = slice(256 * half, 256 * half + 256)
            for rr in _row_chunks():
                z = p_s.load(rr, c0 + 256 * half, c0 + 256 * half + 256)
                dst[sa, rr, cs] = _silu(z)
        return run

    def glu(half):
        def run():
            for blk in range(N_TBLK):
                tr = slice(T_BLK * blk, T_BLK * (blk + 1))
                a = p_s.load(tr, UA0 + 256 * half, UA0 + 256 * half + 256)
                g = p_s.load(tr, UG0 + 256 * half, UG0 + 256 * half + 256)
                u = a * _sigmoid(g)
                p0 = U_PITCH * (blk + 1)
                for sl in range(2):
                    u_s[sa * N_SLABS + 2 * half + sl, p0:p0 + T_BLK, :] = u[:, 128 * sl:128 * sl + 128]
        return run

    return dict(
        rms_next=[rms(x_ref, r) for r in range(0, ts, ROW_CHUNK)],
        rms_first=[rms(x0_ref, r) for r in range(0, ts, ROW_CHUNK)],
        dot_qkvz=dot(Q0, UA0), dot_ag=dot(UA0, ZC0), dot_zc=dot(ZC0, D_IN),
        q_sumsq=[q_sumsq(0), q_sumsq(1)], q_norm=[q_norm(0), q_norm(1)],
        k_sumsq=k_sumsq, kv_store=kv_store,
        gate_a=[gate(ZA0, ga_s, 0), gate(ZA0, ga_s, 1)],
        gate_c=[gate(ZC0, gc_s, 0), gate(ZC0, gc_s, 1)],
        glu=[glu(0), glu(1)])


def _stream_b(sb, first_b, sink_of, wb_ref, dwb_ref, lnw_ref, lnb_ref,
              q_s, k_s, v_s, u_s, u2_s, ga_s, gc_s, c_s, yc_s, y_s, bias_s, o_s, dn_s):
    ts = SEQ_TILE
    taps = CONV_WIDTH - 1

    def pack(cb):
        def run():
            for k in range(N_UTILES):
                tau = k - taps
                base = U_PITCH + tau if tau >= 0 else T_BLK + tau
                halves = [u_s[sb * N_SLABS + cb,
                              pl.ds(base + h * SUBLANES * U_PITCH, SUBLANES, stride=U_PITCH), :]
                          for h in range(N_TBLK // SUBLANES)]
                u2_s[cb, BF16_SUBLANES * k:BF16_SUBLANES * (k + 1), :] = (
                    jnp.concatenate(halves, axis=0).astype(_BF16))
        return run

    def conv(cb):
        def run():
            ls = slice(128 * cb, 128 * cb + 128)
            bv = jnp.broadcast_to(dwb_ref[:, ls], (BF16_SUBLANES, 128))
            wv = [wb_ref[BF16_SUBLANES * j:BF16_SUBLANES * (j + 1), ls].astype(_F32) for j in range(CONV_WIDTH)]
            for r0 in range(0, T_BLK, N_VACC):
                accs = [None] * N_VACC
                for k in range(r0, r0 + N_VACC + taps):
                    win = u2_s[cb, BF16_SUBLANES * k:BF16_SUBLANES * (k + 1), :].astype(_F32)
                    for n in range(N_VACC):
                        j = k - (r0 + n)
                        if 0 <= j < CONV_WIDTH:
                            term = win * wv[j]
                            accs[n] = term if accs[n] is None else accs[n] + term
                for n in range(N_VACC):
                    r = r0 + n
                    c_s[BF16_SUBLANES * r:BF16_SUBLANES * (r + 1), ls] = accs[n] + bv
        return run

    def ln(r):
        def run():
            y = c_s[r:r + ROW_CHUNK, :]
            mu = jnp.sum(y, axis=-1, keepdims=True) * (1.0 / D_CONV)
            d = y - mu
            var = jnp.sum(d * d, axis=-1, keepdims=True) * (1.0 / D_CONV)
            yn = (d * lax.rsqrt(var + LN_EPS)) * lnw_ref[...] + lnb_ref[...]
            act = _silu(yn)
            for n in range(ROW_CHUNK // SUBLANES):
                row = r + SUBLANES * n
                tau, blk0 = divmod(row, BF16_SUBLANES)
                for cb in range(N_SLABS):
                    yc_s[cb, pl.ds(C_PITCH * blk0 + tau, SUBLANES, stride=C_PITCH), :] = (
                        act[SUBLANES * n:SUBLANES * (n + 1), 128 * cb:128 * cb + 128])
        return run

    def regate(blk):
        def run():
            tr = slice(T_BLK * blk, T_BLK * (blk + 1))
            for cb in range(N_SLABS):
                ls = slice(128 * cb, 128 * cb + 128)
                val = yc_s[cb, C_PITCH * blk:C_PITCH * blk + T_BLK, :] * gc_s[sb, tr, ls]
                y_s[sb, tr, D_ATTN + 128 * cb:D_ATTN + 128 * cb + 128] = val.astype(_BF16)
        return run

    def attn_front(b, g):
        def run():
            slot = (2 * b + g) % ATTN_SLOTS
            rows = slice(BLOCK * b, BLOCK * b + BLOCK)
            krows = slice(BLOCK * b, BLOCK * b + 2 * BLOCK)
            tbl = first_b if b == 0 else 0
            lo_q = lax.broadcasted_iota(jnp.int32, (BLOCK, 128), 1) < HEAD_DIM
            c0 = 256 * g
            qg = jnp.concatenate([q_s[sb, rows, c0:c0 + 128], q_s[sb, rows, c0 + 128:c0 + 256]], axis=0)
            kk = jnp.concatenate([k_s[sb, 2 * g, krows, :], k_s[sb, 2 * g + 1, krows, :]], axis=0)
            s = lax.dot_general(qg, kk, (((1,), (1,)), ((), ())),
                                preferred_element_type=_F32)
            ps = [[None, None], [None, None]]
            ex = [[None, None], [None, None]]
            for e in range(2):
                se = s[:, 2 * BLOCK * e:2 * BLOCK * (e + 1)] + bias_s[tbl, 2 * g + e]
                for j in range(2):
                    sink = sink_of(4 * g + 2 * j + e) * LOG2E
                    sj = se[BLOCK * j:BLOCK * j + BLOCK, :]
                    m = jnp.maximum(jnp.max(sj, axis=-1, keepdims=True), sink)
                    ps[j][e] = jnp.exp2(sj - m).astype(_BF16)
                    ex[j][e] = jnp.exp2(sink - m)
            p = jnp.concatenate([jnp.concatenate(ps[0], axis=1), jnp.concatenate(ps[1], axis=1)], axis=0)
            vv = jnp.concatenate([v_s[sb, 2 * g, krows, :], v_s[sb, 2 * g + 1, krows, :]], axis=0)
            o_s[slot] = jnp.dot(p, vv, preferred_element_type=_F32)
            for j in range(2):
                dn_s[slot, j] = jnp.where(lo_q, ex[j][0], ex[j][1])
        return run

    def attn_norm(b, g):
        def run():
            slot = (2 * b + g) % ATTN_SLOTS
            rows = slice(BLOCK * b, BLOCK * b + BLOCK)
            for j in range(2):
                oj = o_s[slot, BLOCK * j:BLOCK * j + BLOCK, :]
                den = oj[:, 128:256] + dn_s[slot, j]
                cs = slice(256 * g + 128 * j, 256 * g + 128 * j + 128)
                y_s[sb, rows, cs] = ((oj[:, 0:128] / den) * ga_s[sb, rows, cs]).astype(_BF16)
        return run

    pairs = [(b, g) for b in range(ts // BLOCK) for g in range(N_KV_HEADS)]
    return dict(attn_front=[attn_front(b, g) for b, g in pairs],
                attn_norm=[attn_norm(b, g) for b, g in pairs],
                pack=[pack(cb) for cb in range(N_SLABS)],
                conv=[conv(cb) for cb in range(N_SLABS)],
                ln=[ln(r) for r in range(0, ts, ROW_CHUNK)],
                regate=[regate(blk) for blk in range(N_TBLK)])


def _stream_c(sy, xres_ref, wout_ref, o_ref, y_s):
    def outproj(n):
        def run():
            o_ref[:, n:n + N_CHUNK] = xres_ref[:, n:n + N_CHUNK] + jnp.dot(
                y_s[sy], wout_ref[:, n:n + N_CHUNK], preferred_element_type=_F32)
        return run

    return [outproj(n) for n in range(0, D_MODEL, N_CHUNK)]


def _load_weight_bf16(w_hbm, w_s, stage, sem):
    rows, cols = w_s.shape
    n_chunks = rows // W_ROWS
    n_slots = stage.shape[0]

    def copy(c):
        slot = c % n_slots
        return pltpu.make_async_copy(w_hbm.at[pl.ds(c * W_ROWS, W_ROWS), :],
                                     stage.at[slot, :, pl.ds(0, cols)], sem.at[slot])

    for c in range(min(n_slots - 1, n_chunks)):
        copy(c).start()
    for c in range(n_chunks):
        nxt = c + n_slots - 1
        if nxt < n_chunks:
            copy(nxt).start()
        copy(c).wait()
        w_s[c * W_ROWS:(c + 1) * W_ROWS, :] = stage[c % n_slots, :, 0:cols].astype(_BF16)


def _layer_kernel(sink_ref, relb_ref, x_ref, xres_ref, nw_ref, win_hbm, qw_ref, kw_ref, bd_ref, dww_ref,
                  dwb_ref, lnw_ref, lnb_ref, wout_hbm, bucket_ref,
                  o_ref,
                  win_ref, wout_ref, wstage_s, wsem, wb_ref,
                  h_s, p1_s, p2_s, p3_s, ssq_s, q_s, k_s, v_s, u_s, u2_s, ga_s, gc_s, c_s, yc_s, y_s,
                  bias_s, o_s, dn_s, *, layer, tiles_per_seq, n_tiles):
    p_s = _RawProj(p1_s, p2_s, p3_s)
    ts = SEQ_TILE
    t = pl.program_id(0)
    sa = lax.rem(t, 2)
    sb = 1 - sa
    first_a = lax.rem(t, tiles_per_seq) == 0
    first_b = jnp.where(lax.rem(t + tiles_per_seq - 1, tiles_per_seq) == 0, 1, 0)

    @pl.when(t == 0)
    def _():
        _load_weight_bf16(win_hbm.at[layer], win_ref, wstage_s, wsem)
        _load_weight_bf16(wout_hbm.at[layer], wout_ref, wstage_s, wsem)
        _build_bias_tables(relb_ref, bucket_ref, bias_s)
        for j in range(CONV_WIDTH):
            wb_ref[BF16_SUBLANES * j:BF16_SUBLANES * (j + 1), :] = jnp.broadcast_to(
                dww_ref[j:j + 1, :], (BF16_SUBLANES, D_CONV)).astype(_BF16)
        y_s[1] = jnp.zeros(y_s.shape[1:], _BF16)

    @pl.when(first_a)
    def _():
        k_s[sa, :, 0:KV_HALO, :] = jnp.zeros((4, KV_HALO, 128), _BF16)
        v_s[sa, :, 0:KV_HALO, :] = jnp.zeros((4, KV_HALO, 256), _BF16)
        for cb in range(N_SLABS):
            u_s[sa * N_SLABS + cb, 0:T_BLK, :] = jnp.zeros((T_BLK, 128), _F32)

    @pl.when(jnp.logical_not(first_a))
    def _():
        k_s[sa, :, 0:KV_HALO, :] = k_s[sb, :, ts:ts + KV_HALO, :]
        v_s[sa, :, 0:KV_HALO, :] = v_s[sb, :, ts:ts + KV_HALO, :]
        for cb in range(N_SLABS):
            u_s[sa * N_SLABS + cb, 0:T_BLK, :] = u_s[sb * N_SLABS + cb,
                                                     N_TBLK * U_PITCH:N_TBLK * U_PITCH + T_BLK, :]

    A = _stream_a(sa, x_ref, xres_ref, nw_ref, win_ref, qw_ref, kw_ref, bd_ref,
                  h_s, p_s, ssq_s, q_s, k_s, v_s, u_s, ga_s, gc_s)
    Bm = _stream_b(sb, first_b, functools.partial(_sink, sink_ref, layer), wb_ref, dwb_ref, lnw_ref, lnb_ref,
                   q_s, k_s, v_s, u_s, u2_s, ga_s, gc_s, c_s, yc_s, y_s, bias_s, o_s, dn_s)
    C = _stream_c(sa, xres_ref, wout_ref, o_ref, y_s)
    af, an = Bm["attn_front"], Bm["attn_norm"]

    regions = [
        Bm["pack"] + af + Bm["conv"] + an,
        Bm["ln"] + Bm["regate"] + [A["dot_qkvz"], A["dot_zc"]]
        + A["q_sumsq"] + [A["k_sumsq"]] + A["gate_a"] + A["gate_c"] + A["q_norm"]
        + [A["kv_store"]] + [A["dot_ag"]] + C + A["glu"] + A["rms_next"],
    ]
    fill_regions = [
        A["rms_first"],
        [A["dot_qkvz"], A["dot_zc"]],
        A["q_sumsq"] + [A["k_sumsq"]] + A["gate_a"] + A["gate_c"] + A["q_norm"]
        + [A["kv_store"]] + [A["dot_ag"]],
        A["glu"] + A["rms_next"],
    ]
    drain_regions = [
        Bm["pack"] + af + Bm["conv"] + an,
        Bm["ln"] + Bm["regate"],
        C,
    ]
    last_regions = [C]
    schedule = [(regions, (t >= 1) & (t < n_tiles)),
                (fill_regions, t == 0),
                (drain_regions, t == n_tiles),
                (last_regions, t == n_tiles + 1)]
    for group, active in schedule:
        trips = jnp.where(active, 1, 0)
        for region in group:
            def body(_, carry, region=region):
                for item in region:
                    item()
                return carry
            lax.fori_loop(0, trips, body, 0)


def _layer(x, layer, norm_w, w_in, q_norm_w, k_norm_w, sinks, dw_w, dw_b, ln_w, ln_b, w_out, rel_bias):
    B, S, D = x.shape
    ts = SEQ_TILE
    assert D == D_MODEL and S % ts == 0 and ts % BLOCK == 0
    n_tiles = B * S // ts

    idx = np.arange(256) // HEAD_DIM
    bd = jnp.asarray((idx[:, None] == idx[None, :]).astype(np.float32), dtype=_BF16)
    qw = (jnp.tile(q_norm_w[layer].astype(_F32), N_HEADS) * (HEAD_DIM ** -0.5 * LOG2E)).reshape(1, D_ATTN)
    kw = jnp.tile(k_norm_w[layer].astype(_F32), N_KV_HEADS).reshape(1, 128)
    xt = x.reshape(n_tiles, ts, D)

    const2 = lambda t: (0, 0)
    layer_row = lambda t: (layer, 0)
    in_specs = [
        pl.BlockSpec(memory_space=pltpu.SMEM),
        pl.BlockSpec(memory_space=pltpu.SMEM),
        pl.BlockSpec((None, ts, D), lambda t: (jnp.minimum(t + 1, n_tiles - 1), 0, 0)),
        pl.BlockSpec((None, ts, D), lambda t: (jnp.maximum(t - 2, 0), 0, 0)),
        pl.BlockSpec((1, D), layer_row),
        pl.BlockSpec(memory_space=pl.ANY),
        pl.BlockSpec((1, D_ATTN), const2),
        pl.BlockSpec((1, 128), const2),
        pl.BlockSpec((256, 256), const2),
        pl.BlockSpec((None, CONV_WIDTH, D_CONV), lambda t: (layer, 0, 0)),
        pl.BlockSpec((1, D_CONV), layer_row),
        pl.BlockSpec((1, D_CONV), layer_row),
        pl.BlockSpec((1, D_CONV), layer_row),
        pl.BlockSpec(memory_space=pl.ANY),
        pl.BlockSpec((BLOCK, 2 * BLOCK), const2),
    ]
    scratch = [
        pltpu.VMEM((D, D_IN), _BF16),
        pltpu.VMEM((D, D), _BF16),
        pltpu.VMEM((W_SLOTS, W_ROWS, D_IN), _F32),
        pltpu.SemaphoreType.DMA((W_SLOTS,)),
        pltpu.VMEM((CONV_WIDTH * BF16_SUBLANES, D_CONV), _BF16),
        pltpu.VMEM((ts, D), _BF16),
        pltpu.VMEM((ts, UA0 - Q0), _F32),
        pltpu.VMEM((ts, ZC0 - UA0), _F32),
        pltpu.VMEM((ts, D_IN - ZC0), _F32),
        pltpu.VMEM((ts, D_ATTN + 128), _F32),
        pltpu.VMEM((2, ts, D_ATTN), _BF16),
        pltpu.VMEM((2, 4, ts + KV_HALO, 128), _BF16),
        pltpu.VMEM((2, 4, ts + KV_HALO, 256), _BF16),
        pltpu.VMEM((2 * N_SLABS, U_ROWS, 128), _F32),
        pltpu.VMEM((N_SLABS, N_UTILES * BF16_SUBLANES, 128), _BF16),
        pltpu.VMEM((2, ts, D_ATTN), _F32),
        pltpu.VMEM((2, ts, D_CONV), _F32),
        pltpu.VMEM((ts, D_CONV), _F32),
        pltpu.VMEM((N_SLABS, C_ROWS, 128), _F32),
        pltpu.VMEM((2, ts, D), _BF16),
        pltpu.VMEM((2, 4, 2 * BLOCK, 2 * BLOCK), _F32),
        pltpu.VMEM((ATTN_SLOTS, 2 * BLOCK, 2 * BLOCK), _F32),
        pltpu.VMEM((ATTN_SLOTS, 2, BLOCK, 128), _F32),
    ]
    out = pl.pallas_call(
        functools.partial(_layer_kernel, layer=layer, tiles_per_seq=S // ts, n_tiles=n_tiles),
        grid=(n_tiles + 2,),
        in_specs=in_specs,
        out_specs=pl.BlockSpec((None, ts, D), lambda t: (jnp.maximum(t - 2, 0), 0, 0)),
        out_shape=jax.ShapeDtypeStruct((n_tiles, ts, D), x.dtype),
        scratch_shapes=scratch,
        compiler_params=pltpu.CompilerParams(
            dimension_semantics=("arbitrary",),
            vmem_limit_bytes=VMEM_LIMIT_BYTES),
        name="hybrid_layer",
    )(sinks.astype(_F32), rel_bias.astype(_F32), xt, xt,
      norm_w.astype(_F32), w_in.astype(_F32), qw, kw, bd,
      dw_w.astype(_F32), dw_b.astype(_F32), ln_w.astype(_F32),
      ln_b.astype(_F32), w_out.astype(_F32), jnp.asarray(_banded_buckets()))
    return out.reshape(B, S, D)


def kernel(x, norm_w, w_in, q_norm_w, k_norm_w, sinks, dw_w, dw_b, ln_w, ln_b, w_out, rel_bias):
    for layer in range(norm_w.shape[0]):
        x = _layer(x, layer, norm_w, w_in, q_norm_w, k_norm_w, sinks, dw_w, dw_b, ln_w, ln_b, w_out, rel_bias)
    return x
```

```python
import functools
import math

import numpy as np
import jax
import jax.numpy as jnp
from jax import lax
from jax.experimental import pallas as pl
from jax.experimental.pallas import tpu as pltpu

D_MODEL = 1024
D_ATTN = 512
D_CONV = 512
HEAD_DIM = 64
N_HEADS = 8
N_KV_HEADS = 2
BLOCK = 128
NUM_BUCKETS = 32
MAX_DISTANCE = 128
CONV_WIDTH = 31
D_IN = 2816
EPS = 1e-6
LN_EPS = 1e-5
NEG_INF = -1e30
LOG2E = math.log2(math.e)

SEQ_TILE = 512
KV_HALO = BLOCK
ROW_CHUNK = 64
SUBLANES = 8
N_SLABS = D_CONV // 128
BF16_SUBLANES = 16
N_VACC = 4
N_TBLK = BF16_SUBLANES
T_BLK = SEQ_TILE // N_TBLK
U_PITCH = T_BLK + 1
U_ROWS = ((N_TBLK + 1) * U_PITCH + 7) // 8 * 8
N_UTILES = CONV_WIDTH - 1 + T_BLK
C_PITCH = T_BLK + 8
C_ROWS = N_TBLK * C_PITCH
N_CHUNK = 256
W_ROWS = 128
W_SLOTS = 4
ATTN_SLOTS = 8
VMEM_LIMIT_BYTES = 56 * 1024 * 1024

Q0, K0, V0, ZA0, UA0, UG0, ZC0 = 0, 512, 640, 768, 1280, 1792, 2304

_F32 = jnp.float32
_BF16 = jnp.bfloat16


def _t5_bucket_table():
    qi = np.arange(BLOCK)[:, None]
    sj = np.arange(2 * BLOCK)[None, :]
    dist = qi + BLOCK - sj
    n = np.maximum(dist, 0)
    max_exact = NUM_BUCKETS // 2
    nf = np.maximum(n, 1).astype(np.float32)
    large = max_exact + (np.log(nf / max_exact) / math.log(MAX_DISTANCE / max_exact)
                         * (NUM_BUCKETS - max_exact)).astype(np.int32)
    large = np.minimum(large, NUM_BUCKETS - 1)
    bucket = np.where(n < max_exact, n, large).astype(np.int32)
    band = (dist >= 0) & (dist < BLOCK)
    return bucket, band


def _banded_buckets():
    bucket, band = _t5_bucket_table()
    return np.where(band, bucket, -1).astype(np.int32)


def _sigmoid(z):
    return 0.5 * jnp.tanh(0.5 * z) + 0.5


def _silu(z):
    h = 0.5 * z
    return h + h * jnp.tanh(h)


def _build_bias_tables(relb_ref, bucket_ref, bias_s):
    cur_keys = lax.broadcasted_iota(jnp.int32, (8, 2 * BLOCK), 1) >= BLOCK

    def body(c, carry):
        r0 = pl.multiple_of(c * 8, 8)
        bk = bucket_ref[pl.ds(r0, 8), :]
        accs = [jnp.full((8, 2 * BLOCK), NEG_INF, _F32) for _ in range(N_HEADS)]
        for b in range(NUM_BUCKETS):
            hit = bk == b
            for h in range(N_HEADS):
                accs[h] = jnp.where(hit, relb_ref[b, h] * LOG2E, accs[h])
        for h in range(N_HEADS):
            g, rem = divmod(h, 4)
            j, e = divmod(rem, 2)
            rows = pl.ds(BLOCK * j + r0, 8)
            bias_s[0, 2 * g + e, rows, :] = accs[h]
            bias_s[1, 2 * g + e, rows, :] = jnp.where(cur_keys, accs[h], NEG_INF)
        return carry

    lax.fori_loop(0, BLOCK // 8, body, 0)


class _RawProj:
    bounds = (Q0, UA0, ZC0, D_IN)

    def __init__(self, *refs):
        self.refs = refs

    def _find(self, c0, c1):
        for ref, lo, hi in zip(self.refs, self.bounds[:-1], self.bounds[1:]):
            if lo <= c0 and c1 <= hi:
                return ref, c0 - lo, c1 - lo
        raise ValueError((c0, c1))

    def load(self, rows, c0, c1):
        ref, a, b = self._find(c0, c1)
        return ref[rows, a:b]

    def store(self, c0, c1, val):
        ref, a, b = self._find(c0, c1)
        ref[:, a:b] = val


def _sink(sink_ref, layer, head):
    return sink_ref[layer, head]


def _row_chunks():
    return [slice(r, r + ROW_CHUNK) for r in range(0, SEQ_TILE, ROW_CHUNK)]


def _stream_a(sa, x_ref, x0_ref, nw_ref, win_ref, qw_ref, kw_ref, bd_ref,
              h_s, p_s, ssq_s, q_s, k_s, v_s, u_s, ga_s, gc_s):
    ts = SEQ_TILE

    def rms(src_ref, r):
        def run():
            xc = src_ref[r:r + ROW_CHUNK, :]
            ss = jnp.sum(xc * xc, axis=-1, keepdims=True)
            rs = lax.rsqrt(ss * (1.0 / D_MODEL) + EPS)
            h_s[r:r + ROW_CHUNK, :] = ((xc * rs) * nw_ref[...]).astype(_BF16)
        return run

    def dot(c0, c1):
        def run():
            p_s.store(c0, c1, jnp.dot(h_s[...], win_ref[:, c0:c1], preferred_element_type=_F32))
        return run

    def q_sumsq(half):
        def run():
            cs = slice(256 * half, 256 * half + 256)
            q = p_s.load(slice(None), Q0 + cs.start, Q0 + cs.stop)
            ssq_s[:, cs] = jnp.dot((q * q).astype(_BF16), bd_ref[...], preferred_element_type=_F32)
        return run

    def q_norm(half):
        def run():
            cs = slice(256 * half, 256 * half + 256)
            for rr in _row_chunks():
                q = p_s.load(rr, Q0 + cs.start, Q0 + cs.stop)
                qn =(q * lax.rsqrt(ssq_s[rr, cs] * (1.0 / HEAD_DIM) + EPS)) * qw_ref[:, cs]
                q_s[sa, rr, cs] = qn.astype(_BF16)
        return run

    def k_sumsq():
        k = p_s.load(slice(None), K0, K0 + 128)
        ssq_s[:, D_ATTN:D_ATTN + 128] = jnp.dot((k * k).astype(_BF16), bd_ref[0:128, 0:128],
                                                preferred_element_type=_F32)

    def kv_store():
        lo = lax.broadcasted_iota(jnp.int32, (ROW_CHUNK, 128), 1) < HEAD_DIM
        zero = jnp.zeros((ROW_CHUNK, 128), _F32)
        one = jnp.ones((ROW_CHUNK, 128), _F32)
        ones_lo = jnp.where(lo, one, zero).astype(_BF16)
        ones_hi = jnp.where(lo, zero, one).astype(_BF16)
        for rr in _row_chunks():
            cur = slice(KV_HALO + rr.start, KV_HALO + rr.stop)
            k = p_s.load(rr, K0, K0 + 128)
            v = p_s.load(rr, V0, V0 + 128)
            kn = (k * lax.rsqrt(ssq_s[rr, D_ATTN:D_ATTN + 128] * (1.0 / HEAD_DIM) + EPS)) * kw_ref[...]
            kn_sw = pltpu.roll(kn, 64, axis=1)
            v_sw = pltpu.roll(v, 64, axis=1)
            k_s[sa, 0, cur, :] = jnp.where(lo, kn, zero).astype(_BF16)
            k_s[sa, 1, cur, :] = jnp.where(lo, zero, kn_sw).astype(_BF16)
            k_s[sa, 2, cur, :] = jnp.where(lo, kn_sw, zero).astype(_BF16)
            k_s[sa, 3, cur, :] = jnp.where(lo, zero, kn).astype(_BF16)
            v_s[sa, 0, cur, 0:128] = jnp.where(lo, v, zero).astype(_BF16)
            v_s[sa, 1, cur, 0:128] = jnp.where(lo, zero, v_sw).astype(_BF16)
            v_s[sa, 2, cur, 0:128] = jnp.where(lo, v_sw, zero).astype(_BF16)
            v_s[sa, 3, cur, 0:128] = jnp.where(lo, zero, v).astype(_BF16)
            v_s[sa, 0, cur, 128:256] = ones_lo
            v_s[sa, 1, cur, 128:256] = ones_hi
            v_s[sa, 2, cur, 128:256] = ones_lo
            v_s[sa, 3, cur, 128:256] = ones_hi

    def gate(c0, dst, half):
        def run():
            cs = slice(256 * half, 256 * half + 256)
            for rr in _row_chunks():
                z = p_s.load(rr, c0 + 256 * half, c0 + 256 * half + 256)
                dst[sa, rr, cs] = _silu(z)
        return run

    def glu(half):
        def run():
            for blk in range(N_TBLK):
                tr = slice(T_BLK * blk, T_BLK * (blk + 1))
                a = p_s.load(tr, UA0 + 256 * half, UA0 + 256 * half + 256)
                g = p_s.load(tr, UG0 + 256 * half, UG0 + 256 * half + 256)
                u = a * _sigmoid(g)
                p0 = U_PITCH * (blk + 1)
                for sl in range(2):
                    u_s[sa * N_SLABS + 2 * half + sl, p0:p0 + T_BLK, :] = u[:, 128 * sl:128 * sl + 128]
        return run

    return dict(
        rms_next=[rms(x_ref, r) for r in range(0, ts, ROW_CHUNK)],
        rms_first=[rms(x0_ref, r) for r in range(0, ts, ROW_CHUNK)],
        dot_qkvz=dot(Q0, UA0), dot_ag=dot(UA0, ZC0), dot_zc=dot(ZC0, D_IN),
        q_sumsq=[q_sumsq(0), q_sumsq(1)], q_norm=[q_norm(0), q_norm(1)],
        k_sumsq=k_sumsq, kv_store=kv_store,
        gate_a=[gate(ZA0, ga_s, 0), gate(ZA0, ga_s, 1)],
        gate_c=[gate(ZC0, gc_s, 0), gate(ZC0, gc_s, 1)],
        glu=[glu(0), glu(1)])


def _stream_b(sb, first_b, sink_of, wb_ref, dwb_ref, lnw_ref, lnb_ref,
              q_s, k_s, v_s, u_s, u2_s, ga_s, gc_s, c_s, yc_s, y_s, bias_s, o_s, dn_s):
    ts = SEQ_TILE
    taps = CONV_WIDTH - 1

    def pack(cb):
        def run():
            for k in range(N_UTILES):
                tau = k - taps
                base = U_PITCH + tau if tau >= 0 else T_BLK + tau
                halves = [u_s[sb * N_SLABS + cb,
                              pl.ds(base + h * SUBLANES * U_PITCH, SUBLANES, stride=U_PITCH), :]
                          for h in range(N_TBLK // SUBLANES)]
                u2_s[cb, BF16_SUBLANES * k:BF16_SUBLANES * (k + 1), :] = (
                    jnp.concatenate(halves, axis=0).astype(_BF16))
        return run

    def conv(cb):
        def run():
            ls = slice(128 * cb, 128 * cb + 128)
            bv = jnp.broadcast_to(dwb_ref[:, ls], (BF16_SUBLANES, 128))
            wv = [wb_ref[BF16_SUBLANES * j:BF16_SUBLANES * (j + 1), ls].astype(_F32) for j in range(CONV_WIDTH)]
            for r0 in range(0, T_BLK, N_VACC):
                accs = [None] * N_VACC
                for k in range(r0, r0 + N_VACC + taps):
                    win = u2_s[cb, BF16_SUBLANES * k:BF16_SUBLANES * (k + 1), :].astype(_F32)
                    for n in range(N_VACC):
                        j = k - (r0 + n)
                        if 0 <= j < CONV_WIDTH:
                            term = win * wv[j]
                            accs[n] = term if accs[n] is None else accs[n] + term
                for n in range(N_VACC):
                    r = r0 + n
                    c_s[BF16_SUBLANES * r:BF16_SUBLANES * (r + 1), ls] = accs[n] + bv
        return run

    def ln(r):
        def run():
            y = c_s[r:r + ROW_CHUNK, :]
            mu = jnp.sum(y, axis=-1, keepdims=True) * (1.0 / D_CONV)
            d = y - mu
            var = jnp.sum(d * d, axis=-1, keepdims=True) * (1.0 / D_CONV)
            yn = (d * lax.rsqrt(var + LN_EPS)) * lnw_ref[...] + lnb_ref[...]
            act = _silu(yn)
            for n in range(ROW_CHUNK // SUBLANES):
                row = r + SUBLANES * n
                tau, blk0 = divmod(row, BF16_SUBLANES)
                for cb in range(N_SLABS):
                    yc_s[cb, pl.ds(C_PITCH * blk0 + tau, SUBLANES, stride=C_PITCH), :] = (
                        act[SUBLANES * n:SUBLANES * (n + 1), 128 * cb:128 * cb + 128])
        return run

    def regate(blk):
        def run():
            tr = slice(T_BLK * blk, T_BLK * (blk + 1))
            for cb in range(N_SLABS):
                ls = slice(128 * cb, 128 * cb + 128)
                val = yc_s[cb, C_PITCH * blk:C_PITCH * blk + T_BLK, :] * gc_s[sb, tr, ls]
                y_s[sb, tr, D_ATTN + 128 * cb:D_ATTN + 128 * cb + 128] = val.astype(_BF16)
        return run

    def attn_front(b, g):
        def run():
            slot = (2 * b + g) % ATTN_SLOTS
            rows = slice(BLOCK * b, BLOCK * b + BLOCK)
            krows = slice(BLOCK * b, BLOCK * b + 2 * BLOCK)
            tbl = first_b if b == 0 else 0
            lo_q = lax.broadcasted_iota(jnp.int32, (BLOCK, 128), 1) < HEAD_DIM
            c0 = 256 * g
            qg = jnp.concatenate([q_s[sb, rows, c0:c0 + 128], q_s[sb, rows, c0 + 128:c0 + 256]], axis=0)
            kk = jnp.concatenate([k_s[sb, 2 * g, krows, :], k_s[sb, 2 * g + 1, krows, :]], axis=0)
            s = lax.dot_general(qg, kk, (((1,), (1,)), ((), ())),
                                preferred_element_type=_F32)
            ps = [[None, None], [None, None]]
            ex = [[None, None], [None, None]]
            for e in range(2):
                se = s[:, 2 * BLOCK * e:2 * BLOCK * (e + 1)] + bias_s[tbl, 2 * g + e]
                for j in range(2):
                    sink = sink_of(4 * g + 2 * j + e) * LOG2E
                    sj = se[BLOCK * j:BLOCK * j + BLOCK, :]
                    m = jnp.maximum(jnp.max(sj, axis=-1, keepdims=True), sink)
                    ps[j][e] = jnp.exp2(sj - m).astype(_BF16)
                    ex[j][e] = jnp.exp2(sink - m)
            p = jnp.concatenate([jnp.concatenate(ps[0], axis=1), jnp.concatenate(ps[1], axis=1)], axis=0)
            vv = jnp.concatenate([v_s[sb, 2 * g, krows, :], v_s[sb, 2 * g + 1, krows, :]], axis=0)
            o_s[slot] = jnp.dot(p, vv, preferred_element_type=_F32)
            for j in range(2):
                dn_s[slot, j] = jnp.where(lo_q, ex[j][0], ex[j][1])
        return run

    def attn_norm(b, g):
        def run():
            slot = (2 * b + g) % ATTN_SLOTS
            rows = slice(BLOCK * b, BLOCK * b + BLOCK)
            for j in range(2):
                oj = o_s[slot, BLOCK * j:BLOCK * j + BLOCK, :]
                den = oj[:, 128:256] + dn_s[slot, j]
                cs = slice(256 * g + 128 * j, 256 * g + 128 * j + 128)
                y_s[sb, rows, cs] = ((oj[:, 0:128] / den) * ga_s[sb, rows, cs]).astype(_BF16)
        return run

    pairs = [(b, g) for b in range(ts // BLOCK) for g in range(N_KV_HEADS)]
    return dict(attn_front=[attn_front(b, g) for b, g in pairs],
                attn_norm=[attn_norm(b, g) for b, g in pairs],
                pack=[pack(cb) for cb in range(N_SLABS)],
                conv=[conv(cb) for cb in range(N_SLABS)],
                ln=[ln(r) for r in range(0, ts, ROW_CHUNK)],
                regate=[regate(blk) for blk in range(N_TBLK)])


def _stream_c(sy, xres_ref, wout_ref, o_ref, y_s):
    def outproj(n):
        def run():
            o_ref[:, n:n + N_CHUNK] = xres_ref[:, n:n + N_CHUNK] + jnp.dot(
                y_s[sy], wout_ref[:, n:n + N_CHUNK], preferred_element_type=_F32)
        return run

    return [outproj(n) for n in range(0, D_MODEL, N_CHUNK)]


def _load_weight_bf16(w_hbm, w_s, stage, sem):
    rows, cols = w_s.shape
    n_chunks = rows // W_ROWS
    n_slots = stage.shape[0]

    def copy(c):
        slot = c % n_slots
        return pltpu.make_async_copy(w_hbm.at[pl.ds(c * W_ROWS, W_ROWS), :],
                                     stage.at[slot, :, pl.ds(0, cols)], sem.at[slot])

    for c in range(min(n_slots - 1, n_chunks)):
        copy(c).start()
    for c in range(n_chunks):
        nxt = c + n_slots - 1
        if nxt < n_chunks:
            copy(nxt).start()
        copy(c).wait()
        w_s[c * W_ROWS:(c + 1) * W_ROWS, :] = stage[c % n_slots, :, 0:cols].astype(_BF16)


def _layer_kernel(sink_ref, relb_ref, x_ref, xres_ref, nw_ref, win_hbm, qw_ref, kw_ref, bd_ref, dww_ref,
                  dwb_ref, lnw_ref, lnb_ref, wout_hbm, bucket_ref,
                  o_ref,
                  win_ref, wout_ref, wstage_s, wsem, wb_ref,
                  h_s, p1_s, p2_s, p3_s, ssq_s, q_s, k_s, v_s, u_s, u2_s, ga_s, gc_s, c_s, yc_s, y_s,
                  bias_s, o_s, dn_s, *, layer, tiles_per_seq, n_tiles):
    p_s = _RawProj(p1_s, p2_s, p3_s)
    ts = SEQ_TILE
    t = pl.program_id(0)
    sa = lax.rem(t, 2)
    sb = 1 - sa
    first_a = lax.rem(t, tiles_per_seq) == 0
    first_b = jnp.where(lax.rem(t + tiles_per_seq - 1, tiles_per_seq) == 0, 1, 0)

    @pl.when(t == 0)
    def _():
        _load_weight_bf16(win_hbm.at[layer], win_ref, wstage_s, wsem)
        _load_weight_bf16(wout_hbm.at[layer], wout_ref, wstage_s, wsem)
        _build_bias_tables(relb_ref, bucket_ref, bias_s)
        for j in range(CONV_WIDTH):
            wb_ref[BF16_SUBLANES * j:BF16_SUBLANES * (j + 1), :] = jnp.broadcast_to(
                dww_ref[j:j + 1, :], (BF16_SUBLANES, D_CONV)).astype(_BF16)
        y_s[1] = jnp.zeros(y_s.shape[1:], _BF16)

    @pl.when(first_a)
    def _():
        k_s[sa, :, 0:KV_HALO, :] = jnp.zeros((4, KV_HALO, 128), _BF16)
        v_s[sa, :, 0:KV_HALO, :] = jnp.zeros((4, KV_HALO, 256), _BF16)
        for cb in range(N_SLABS):
            u_s[sa * N_SLABS + cb, 0:T_BLK, :] = jnp.zeros((T_BLK, 128), _F32)

    @pl.when(jnp.logical_not(first_a))
    def _():
        k_s[sa, :, 0:KV_HALO, :] = k_s[sb, :, ts:ts + KV_HALO, :]
        v_s[sa, :, 0:KV_HALO, :] = v_s[sb, :, ts:ts + KV_HALO, :]
        for cb in range(N_SLABS):
            u_s[sa * N_SLABS + cb, 0:T_BLK, :] = u_s[sb * N_SLABS + cb,
                                                     N_TBLK * U_PITCH:N_TBLK * U_PITCH + T_BLK, :]

    A = _stream_a(sa, x_ref, xres_ref, nw_ref, win_ref, qw_ref, kw_ref, bd_ref,
                  h_s, p_s, ssq_s, q_s, k_s, v_s, u_s, ga_s, gc_s)
    Bm = _stream_b(sb, first_b, functools.partial(_sink, sink_ref, layer), wb_ref, dwb_ref, lnw_ref, lnb_ref,
                   q_s, k_s, v_s, u_s, u2_s, ga_s, gc_s, c_s, yc_s, y_s, bias_s, o_s, dn_s)
    C = _stream_c(sa, xres_ref, wout_ref, o_ref, y_s)
    af, an = Bm["attn_front"], Bm["attn_norm"]

    regions = [
        Bm["pack"] + af + an,
        [item for pair in zip(C, Bm["conv"]) for item in pair],
        Bm["ln"] + Bm["regate"] + [A["dot_qkvz"], A["dot_zc"]]
        + A["q_sumsq"] + [A["k_sumsq"]] + A["gate_a"] + A["gate_c"] + A["q_norm"]
        + [A["kv_store"]] + [A["dot_ag"]] + A["glu"] + A["rms_next"],
    ]
    fill_regions = [
        A["rms_first"],
        [A["dot_qkvz"], A["dot_zc"]],
        A["q_sumsq"] + [A["k_sumsq"]] + A["gate_a"] + A["gate_c"] + A["q_norm"]
        + [A["kv_store"]] + [A["dot_ag"]],
        A["glu"] + A["rms_next"],
    ]
    drain_regions = [
        Bm["pack"] + af + Bm["conv"] + an,
        Bm["ln"] + Bm["regate"],
        C,
    ]
    last_regions = [C]
    schedule = [(regions, (t >= 1) & (t < n_tiles)),
                (fill_regions, t == 0),
                (drain_regions, t == n_tiles),
                (last_regions, t == n_tiles + 1)]
    for group, active in schedule:
        trips = jnp.where(active, 1, 0)
        for region in group:
            def body(_, carry, region=region):
                for item in region:
                    item()
                return carry
            lax.fori_loop(0, trips, body, 0)


def _layer(x, layer, norm_w, w_in, q_norm_w, k_norm_w, sinks, dw_w, dw_b, ln_w, ln_b, w_out, rel_bias):
    B, S, D = x.shape
    ts = SEQ_TILE
    assert D == D_MODEL and S % ts == 0 and ts % BLOCK == 0
    n_tiles = B * S // ts

    idx = np.arange(256) // HEAD_DIM
    bd = jnp.asarray((idx[:, None] == idx[None, :]).astype(np.float32), dtype=_BF16)
    qw = (jnp.tile(q_norm_w[layer].astype(_F32), N_HEADS) * (HEAD_DIM ** -0.5 * LOG2E)).reshape(1, D_ATTN)
    kw = jnp.tile(k_norm_w[layer].astype(_F32), N_KV_HEADS).reshape(1, 128)
    xt = x.reshape(n_tiles, ts, D)

    const2 = lambda t: (0, 0)
    layer_row = lambda t: (layer, 0)
    in_specs = [
        pl.BlockSpec(memory_space=pltpu.SMEM),
        pl.BlockSpec(memory_space=pltpu.SMEM),
        pl.BlockSpec((None, ts, D), lambda t: (jnp.minimum(t + 1, n_tiles - 1), 0, 0)),
        pl.BlockSpec((None, ts, D), lambda t: (jnp.maximum(t - 2, 0), 0, 0)),
        pl.BlockSpec((1, D), layer_row),
        pl.BlockSpec(memory_space=pl.ANY),
        pl.BlockSpec((1, D_ATTN), const2),
        pl.BlockSpec((1, 128), const2),
        pl.BlockSpec((256, 256), const2),
        pl.BlockSpec((None, CONV_WIDTH, D_CONV), lambda t: (layer, 0, 0)),
        pl.BlockSpec((1, D_CONV), layer_row),
        pl.BlockSpec((1, D_CONV), layer_row),
        pl.BlockSpec((1, D_CONV), layer_row),
        pl.BlockSpec(memory_space=pl.ANY),
        pl.BlockSpec((BLOCK, 2 * BLOCK), const2),
    ]
    scratch = [
        pltpu.VMEM((D, D_IN), _BF16),
        pltpu.VMEM((D, D), _BF16),
        pltpu.VMEM((W_SLOTS, W_ROWS, D_IN), _F32),
        pltpu.SemaphoreType.DMA((W_SLOTS,)),
        pltpu.VMEM((CONV_WIDTH * BF16_SUBLANES, D_CONV), _BF16),
        pltpu.VMEM((ts, D), _BF16),
        pltpu.VMEM((ts, UA0 - Q0), _F32),
        pltpu.VMEM((ts, ZC0 - UA0), _F32),
        pltpu.VMEM((ts, D_IN - ZC0), _F32),
        pltpu.VMEM((ts, D_ATTN + 128), _F32),
        pltpu.VMEM((2, ts, D_ATTN), _BF16),
        pltpu.VMEM((2, 4, ts + KV_HALO, 128), _BF16),
        pltpu.VMEM((2, 4, ts + KV_HALO, 256), _BF16),
        pltpu.VMEM((2 * N_SLABS, U_ROWS, 128), _F32),
        pltpu.VMEM((N_SLABS, N_UTILES * BF16_SUBLANES, 128), _BF16),
        pltpu.VMEM((2, ts, D_ATTN), _F32),
        pltpu.VMEM((2, ts, D_CONV), _F32),
        pltpu.VMEM((ts, D_CONV), _F32),
        pltpu.VMEM((N_SLABS, C_ROWS, 128), _F32),
        pltpu.VMEM((2, ts, D), _BF16),
        pltpu.VMEM((2, 4, 2 * BLOCK, 2 * BLOCK), _F32),
        pltpu.VMEM((ATTN_SLOTS, 2 * BLOCK, 2 * BLOCK), _F32),
        pltpu.VMEM((ATTN_SLOTS, 2, BLOCK, 128), _F32),
    ]
    out = pl.pallas_call(
        functools.partial(_layer_kernel, layer=layer, tiles_per_seq=S // ts, n_tiles=n_tiles),
        grid=(n_tiles + 2,),
        in_specs=in_specs,
        out_specs=pl.BlockSpec((None, ts, D), lambda t: (jnp.maximum(t - 2, 0), 0, 0)),
        out_shape=jax.ShapeDtypeStruct((n_tiles, ts, D), x.dtype),
        scratch_shapes=scratch,
        compiler_params=pltpu.CompilerParams(
            dimension_semantics=("arbitrary",),
            vmem_limit_bytes=VMEM_LIMIT_BYTES),
        name="hybrid_layer",
    )(sinks.astype(_F32), rel_bias.astype(_F32), xt, xt,
      norm_w.astype(_F32), w_in.astype(_F32), qw, kw, bd,
      dw_w.astype(_F32), dw_b.astype(_F32), ln_w.astype(_F32),
      ln_b.astype(_F32), w_out.astype(_F32), jnp.asarray(_banded_buckets()))
    return out.reshape(B, S, D)


def kernel(x, norm_w, w_in, q_norm_w, k_norm_w, sinks, dw_w, dw_b, ln_w, ln_b, w_out, rel_bias):
    for layer in range(norm_w.shape[0]):
        x = _layer(x, layer, norm_w, w_in, q_norm_w, k_norm_w, sinks, dw_w, dw_b, ln_w, ln_b, w_out, rel_bias)
    return x
```

```python
import functools
import math

import numpy as np
import jax
import jax.numpy as jnp
from jax import lax
from jax.experimental import pallas as pl
from jax.experimental.pallas import tpu as pltpu

D_MODEL = 1024
D_ATTN = 512
D_CONV = 512
HEAD_DIM = 64
N_HEADS = 8
N_KV_HEADS = 2
BLOCK = 128
NUM_BUCKETS = 32
MAX_DISTANCE = 128
CONV_WIDTH = 31
D_IN = 2816
EPS = 1e-6
LN_EPS = 1e-5
NEG_INF = -1e30
LOG2E = math.log2(math.e)

SEQ_TILE = 512
KV_HALO = BLOCK
ROW_CHUNK = 64
SUBLANES = 8
N_SLABS = D_CONV // 128
BF16_SUBLANES = 16
N_VACC = 4
N_TBLK = BF16_SUBLANES
T_BLK = SEQ_TILE // N_TBLK
U_PITCH = T_BLK + 1
U_ROWS = ((N_TBLK + 1) * U_PITCH + 7) // 8 * 8
N_UTILES = CONV_WIDTH - 1 + T_BLK
C_PITCH = T_BLK + 8
C_ROWS = N_TBLK * C_PITCH
N_CHUNK = 256
W_ROWS = 128
W_SLOTS = 4
ATTN_SLOTS = 8
VMEM_LIMIT_BYTES = 56 * 1024 * 1024

Q0, K0, V0, ZA0, UA0, UG0, ZC0 = 0, 512, 640, 768, 1280, 1792, 2304

_F32 = jnp.float32
_BF16 = jnp.bfloat16


def _t5_bucket_table():
    qi = np.arange(BLOCK)[:, None]
    sj = np.arange(2 * BLOCK)[None, :]
    dist = qi + BLOCK - sj
    n = np.maximum(dist, 0)
    max_exact = NUM_BUCKETS // 2
    nf = np.maximum(n, 1).astype(np.float32)
    large = max_exact + (np.log(nf / max_exact) / math.log(MAX_DISTANCE / max_exact)
                         * (NUM_BUCKETS - max_exact)).astype(np.int32)
    large = np.minimum(large, NUM_BUCKETS - 1)
    bucket = np.where(n < max_exact, n, large).astype(np.int32)
    band = (dist >= 0) & (dist < BLOCK)
    return bucket, band


def _banded_buckets():
    bucket, band = _t5_bucket_table()
    return np.where(band, bucket, -1).astype(np.int32)


def _sigmoid(z):
    return 0.5 * jnp.tanh(0.5 * z) + 0.5


def _silu(z):
    h = 0.5 * z
    return h + h * jnp.tanh(h)


def _build_bias_tables(relb_ref, bucket_ref, bias_s):
    cur_keys = lax.broadcasted_iota(jnp.int32, (8, 2 * BLOCK), 1) >= BLOCK

    def body(c, carry):
        r0 = pl.multiple_of(c * 8, 8)
        bk = bucket_ref[pl.ds(r0, 8), :]
        accs = [jnp.full((8, 2 * BLOCK), NEG_INF, _F32) for _ in range(N_HEADS)]
        for b in range(NUM_BUCKETS):
            hit = bk == b
            for h in range(N_HEADS):
                accs[h] = jnp.where(hit, relb_ref[b, h] * LOG2E, accs[h])
        for h in range(N_HEADS):
            g, rem = divmod(h, 4)
            j, e = divmod(rem, 2)
            rows = pl.ds(BLOCK * j + r0, 8)
            bias_s[0, 2 * g + e, rows, :] = accs[h]
            bias_s[1, 2 * g + e, rows, :] = jnp.where(cur_keys, accs[h], NEG_INF)
        return carry

    lax.fori_loop(0, BLOCK // 8, body, 0)


class _RawProj:
    bounds = (Q0, UA0, ZC0, D_IN)

    def __init__(self, *refs):
        self.refs = refs

    def _find(self, c0, c1):
        for ref, lo, hi in zip(self.refs, self.bounds[:-1], self.bounds[1:]):
            if lo <= c0 and c1 <= hi:
                return ref, c0 - lo, c1 - lo
        raise ValueError((c0, c1))

    def load(self, rows, c0, c1):
        ref, a, b = self._find(c0, c1)
        return ref[rows, a:b]

    def store(self, c0, c1, val):
        ref, a, b = self._find(c0, c1)
        ref[:, a:b] = val


def _sink(sink_ref, layer, head):
    return sink_ref[layer, head]


def _row_chunks():
    return [slice(r, r + ROW_CHUNK) for r in range(0, SEQ_TILE, ROW_CHUNK)]


def _stream_a(sa, x_ref, x0_ref, nw_ref, win_ref, qw_ref, kw_ref, bd_ref,
              h_s, p_s, ssq_s, q_s, k_s, v_s, u_s, ga_s, gc_s):
    ts = SEQ_TILE

    def rms(src_ref, r):
        def run():
            xc = src_ref[r:r + ROW_CHUNK, :]
            ss = jnp.sum(xc * xc, axis=-1, keepdims=True)
            rs = lax.rsqrt(ss * (1.0 / D_MODEL) + EPS)
            h_s[r:r + ROW_CHUNK, :] = ((xc * rs) * nw_ref[...]).astype(_BF16)
        return run

    def dot(c0, c1):
        def run():
            p_s.store(c0, c1, jnp.dot(h_s[...], win_ref[:, c0:c1], preferred_element_type=_F32))
        return run

    def q_sumsq(half):
        def run():
            cs = slice(256 * half, 256 * half + 256)
            q = p_s.load(slice(None), Q0 + cs.start, Q0 + cs.stop)
            ssq_s[:, cs] = jnp.dot((q * q).astype(_BF16), bd_ref[...], preferred_element_type=_F32)
        return run

    def q_norm(half):
        def run():
            cs = slice(256 * half, 256 * half + 256)
            for rr in _row_chunks():
                q = p_s.load(rr, Q0 + cs.start, Q0 + cs.stop)
                qn =(q * lax.rsqrt(ssq_s[rr, cs] * (1.0 / HEAD_DIM) + EPS)) * qw_ref[:, cs]
                q_s[sa, rr, cs] = qn.astype(_BF16)
        return run

    def k_sumsq():
        k = p_s.load(slice(None), K0, K0 + 128)
        ssq_s[:, D_ATTN:D_ATTN + 128] = jnp.dot((k * k).astype(_BF16), bd_ref[0:128, 0:128],
                                                preferred_element_type=_F32)

    def kv_store():
        lo = lax.broadcasted_iota(jnp.int32, (ROW_CHUNK, 128), 1) < HEAD_DIM
        zero = jnp.zeros((ROW_CHUNK, 128), _F32)
        one = jnp.ones((ROW_CHUNK, 128), _F32)
        ones_lo = jnp.where(lo, one, zero).astype(_BF16)
        ones_hi = jnp.where(lo, zero, one).astype(_BF16)
        for rr in _row_chunks():
            cur = slice(KV_HALO + rr.start, KV_HALO + rr.stop)
            k = p_s.load(rr, K0, K0 + 128)
            v = p_s.load(rr, V0, V0 + 128)
            kn = (k * lax.rsqrt(ssq_s[rr, D_ATTN:D_ATTN + 128] * (1.0 / HEAD_DIM) + EPS)) * kw_ref[...]
            kn_sw = pltpu.roll(kn, 64, axis=1)
            v_sw = pltpu.roll(v, 64, axis=1)
            k_s[sa, 0, cur, :] = jnp.where(lo, kn, zero).astype(_BF16)
            k_s[sa, 1, cur, :] = jnp.where(lo, zero, kn_sw).astype(_BF16)
            k_s[sa, 2, cur, :] = jnp.where(lo, kn_sw, zero).astype(_BF16)
            k_s[sa, 3, cur, :] = jnp.where(lo, zero, kn).astype(_BF16)
            v_s[sa, 0, cur, 0:128] = jnp.where(lo, v, zero).astype(_BF16)
            v_s[sa, 1, cur, 0:128] = jnp.where(lo, zero, v_sw).astype(_BF16)
            v_s[sa, 2, cur, 0:128] = jnp.where(lo, v_sw, zero).astype(_BF16)
            v_s[sa, 3, cur, 0:128] = jnp.where(lo, zero, v).astype(_BF16)
            v_s[sa, 0, cur, 128:256] = ones_lo
            v_s[sa, 1, cur, 128:256] = ones_hi
            v_s[sa, 2, cur, 128:256] = ones_lo
            v_s[sa, 3, cur, 128:256] = ones_hi

    def gate(c0, dst, half, slot):
        def run():
            cs = slice(256 * half, 256 * half + 256)
            for rr in _row_chunks():
                z = p_s.load(rr, c0 + 256 * half, c0 + 256 * half + 256)
                dst[slot, rr, cs] = _silu(z)
        return run

    def glu(half):
        def run():
            for blk in range(N_TBLK):
                tr = slice(T_BLK * blk, T_BLK * (blk + 1))
                a = p_s.load(tr, UA0 + 256 * half, UA0 + 256 * half + 256)
                g = p_s.load(tr, UG0 + 256 * half, UG0 + 256 * half + 256)
                u = a * _sigmoid(g)
                p0 = U_PITCH * (blk + 1)
                for sl in range(2):
                    u_s[sa * N_SLABS + 2 * half + sl, p0:p0 + T_BLK, :] = u[:, 128 * sl:128 * sl + 128]
        return run

    return dict(
        rms_next=[rms(x_ref, r) for r in range(0, ts, ROW_CHUNK)],
        rms_first=[rms(x0_ref, r) for r in range(0, ts, ROW_CHUNK)],
        dot_qkvz=dot(Q0, UA0), dot_zc=dot(ZC0, D_IN),
        dot_ag=[dot(c0 + 256 * half, c0 + 256 * half + 256) for half in range(2) for c0 in (UA0, UG0)],
        q_sumsq=[q_sumsq(0), q_sumsq(1)], q_norm=[q_norm(0), q_norm(1)],
        k_sumsq=k_sumsq, kv_store=kv_store,
        gate_a=[gate(ZA0, ga_s, 0, sa), gate(ZA0, ga_s, 1, sa)],
        gate_c_prev=[gate(ZC0, gc_s, 0, 1 - sa), gate(ZC0, gc_s, 1, 1 - sa)],
        glu=[glu(0), glu(1)])


def _stream_b(sb, first_b, sink_of, wb_ref, dwb_ref, lnw_ref, lnb_ref,
              q_s, k_s, v_s, u_s, u2_s, ga_s, gc_s, c_s, yc_s, y_s, bias_s, o_s, dn_s):
    ts = SEQ_TILE
    taps = CONV_WIDTH - 1

    def pack(cb):
        def run():
            for k in range(N_UTILES):
                tau = k - taps
                base = U_PITCH + tau if tau >= 0 else T_BLK + tau
                halves = [u_s[sb * N_SLABS + cb,
                              pl.ds(base + h * SUBLANES * U_PITCH, SUBLANES, stride=U_PITCH), :]
                          for h in range(N_TBLK // SUBLANES)]
                u2_s[cb, BF16_SUBLANES * k:BF16_SUBLANES * (k + 1), :] = (
                    jnp.concatenate(halves, axis=0).astype(_BF16))
        return run

    def conv(cb):
        def run():
            ls = slice(128 * cb, 128 * cb + 128)
            bv = jnp.broadcast_to(dwb_ref[:, ls], (BF16_SUBLANES, 128))
            wv = [wb_ref[BF16_SUBLANES * j:BF16_SUBLANES * (j + 1), ls].astype(_F32) for j in range(CONV_WIDTH)]
            for r0 in range(0, T_BLK, N_VACC):
                accs = [None] * N_VACC
                for k in range(r0, r0 + N_VACC + taps):
                    win = u2_s[cb, BF16_SUBLANES * k:BF16_SUBLANES * (k + 1), :].astype(_F32)
                    for n in range(N_VACC):
                        j = k - (r0 + n)
                        if 0 <= j < CONV_WIDTH:
                            term = win * wv[j]
                            accs[n] = term if accs[n] is None else accs[n] + term
                for n in range(N_VACC):
                    r = r0 + n
                    c_s[BF16_SUBLANES * r:BF16_SUBLANES * (r + 1), ls] = accs[n] + bv
        return run

    def ln(r):
        def run():
            y = c_s[r:r + ROW_CHUNK, :]
            mu = jnp.sum(y, axis=-1, keepdims=True) * (1.0 / D_CONV)
            d = y - mu
            var = jnp.sum(d * d, axis=-1, keepdims=True) * (1.0 / D_CONV)
            yn = (d * lax.rsqrt(var + LN_EPS)) * lnw_ref[...] + lnb_ref[...]
            act = _silu(yn)
            for n in range(ROW_CHUNK // SUBLANES):
                row = r + SUBLANES * n
                tau, blk0 = divmod(row, BF16_SUBLANES)
                for cb in range(N_SLABS):
                    yc_s[cb, pl.ds(C_PITCH * blk0 + tau, SUBLANES, stride=C_PITCH), :] = (
                        act[SUBLANES * n:SUBLANES * (n + 1), 128 * cb:128 * cb + 128])
        return run

    def regate(blk):
        def run():
            tr = slice(T_BLK * blk, T_BLK * (blk + 1))
            for cb in range(N_SLABS):
                ls = slice(128 * cb, 128 * cb + 128)
                val = yc_s[cb, C_PITCH * blk:C_PITCH * blk + T_BLK, :] * gc_s[sb, tr, ls]
                y_s[sb, tr, D_ATTN + 128 * cb:D_ATTN + 128 * cb + 128] = val.astype(_BF16)
        return run

    def attn_front(b, g):
        def run():
            slot = (2 * b + g) % ATTN_SLOTS
            rows = slice(BLOCK * b, BLOCK * b + BLOCK)
            krows = slice(BLOCK * b, BLOCK * b + 2 * BLOCK)
            tbl = first_b if b == 0 else 0
            lo_q = lax.broadcasted_iota(jnp.int32, (BLOCK, 128), 1) < HEAD_DIM
            c0 = 256 * g
            qg = jnp.concatenate([q_s[sb, rows, c0:c0 + 128], q_s[sb, rows, c0 + 128:c0 + 256]], axis=0)
            kk = jnp.concatenate([k_s[sb, 2 * g, krows, :], k_s[sb, 2 * g + 1, krows, :]], axis=0)
            s = lax.dot_general(qg, kk, (((1,), (1,)), ((), ())),
                                preferred_element_type=_F32)
            ps = [[None, None], [None, None]]
            ex = [[None, None], [None, None]]
            for e in range(2):
                se = s[:, 2 * BLOCK * e:2 * BLOCK * (e + 1)] + bias_s[tbl, 2 * g + e]
                for j in range(2):
                    sink = sink_of(4 * g + 2 * j + e) * LOG2E
                    sj = se[BLOCK * j:BLOCK * j + BLOCK, :]
                    m = jnp.maximum(jnp.max(sj, axis=-1, keepdims=True), sink)
                    ps[j][e] = jnp.exp2(sj - m).astype(_BF16)
                    ex[j][e] = jnp.exp2(sink - m)
            p = jnp.concatenate([jnp.concatenate(ps[0], axis=1), jnp.concatenate(ps[1], axis=1)], axis=0)
            vv = jnp.concatenate([v_s[sb, 2 * g, krows, :], v_s[sb, 2 * g + 1, krows, :]], axis=0)
            o_s[slot] = jnp.dot(p, vv, preferred_element_type=_F32)
            for j in range(2):
                dn_s[slot, j] = jnp.where(lo_q, ex[j][0], ex[j][1])
        return run

    def attn_norm(b, g):
        def run():
            slot = (2 * b + g) % ATTN_SLOTS
            rows = slice(BLOCK * b, BLOCK * b + BLOCK)
            for j in range(2):
                oj = o_s[slot, BLOCK * j:BLOCK * j + BLOCK, :]
                den = oj[:, 128:256] + dn_s[slot, j]
                cs = slice(256 * g + 128 * j, 256 * g + 128 * j + 128)
                y_s[sb, rows, cs] = ((oj[:, 0:128] / den) * ga_s[sb, rows, cs]).astype(_BF16)
        return run

    pairs = [(b, g) for b in range(ts // BLOCK) for g in range(N_KV_HEADS)]
    return dict(attn_front=[attn_front(b, g) for b, g in pairs],
                attn_norm=[attn_norm(b, g) for b, g in pairs],
                pack=[pack(cb) for cb in range(N_SLABS)],
                conv=[conv(cb) for cb in range(N_SLABS)],
                ln=[ln(r) for r in range(0, ts, ROW_CHUNK)],
                regate=[regate(blk) for blk in range(N_TBLK)])


def _stream_c(sy, xres_ref, wout_ref, o_ref, y_s):
    def outproj(n):
        def run():
            o_ref[:, n:n + N_CHUNK] = xres_ref[:, n:n + N_CHUNK] + jnp.dot(
                y_s[sy], wout_ref[:, n:n + N_CHUNK], preferred_element_type=_F32)
        return run

    return [outproj(n) for n in range(0, D_MODEL, N_CHUNK)]


def _load_weight_bf16(w_hbm, w_s, stage, sem):
    rows, cols = w_s.shape
    n_chunks = rows // W_ROWS
    n_slots = stage.shape[0]

    def copy(c):
        slot = c % n_slots
        return pltpu.make_async_copy(w_hbm.at[pl.ds(c * W_ROWS, W_ROWS), :],
                                     stage.at[slot, :, pl.ds(0, cols)], sem.at[slot])

    for c in range(min(n_slots - 1, n_chunks)):
        copy(c).start()
    for c in range(n_chunks):
        nxt = c + n_slots - 1
        if nxt < n_chunks:
            copy(nxt).start()
        copy(c).wait()
        w_s[c * W_ROWS:(c + 1) * W_ROWS, :] = stage[c % n_slots, :, 0:cols].astype(_BF16)


def _layer_kernel(sink_ref, relb_ref, x_ref, xres_ref, nw_ref, win_hbm, qw_ref, kw_ref, bd_ref, dww_ref,
                  dwb_ref, lnw_ref, lnb_ref, wout_hbm, bucket_ref,
                  o_ref,
                  win_ref, wout_ref, wstage_s, wsem, wb_ref,
                  h_s, p1_s, p2_s, p3_s, ssq_s, q_s, k_s, v_s, u_s, u2_s, ga_s, gc_s, c_s, yc_s, y_s,
                  bias_s, o_s, dn_s, *, layer, tiles_per_seq, n_tiles):
    p_s = _RawProj(p1_s, p2_s, p3_s)
    ts = SEQ_TILE
    t = pl.program_id(0)
    sa = lax.rem(t, 2)
    sb = 1 - sa
    first_a = lax.rem(t, tiles_per_seq) == 0
    first_b = jnp.where(lax.rem(t + tiles_per_seq - 1, tiles_per_seq) == 0, 1, 0)

    @pl.when(t == 0)
    def _():
        _load_weight_bf16(win_hbm.at[layer], win_ref, wstage_s, wsem)
        _load_weight_bf16(wout_hbm.at[layer], wout_ref, wstage_s, wsem)
        _build_bias_tables(relb_ref, bucket_ref, bias_s)
        for j in range(CONV_WIDTH):
            wb_ref[BF16_SUBLANES * j:BF16_SUBLANES * (j + 1), :] = jnp.broadcast_to(
                dww_ref[j:j + 1, :], (BF16_SUBLANES, D_CONV)).astype(_BF16)
        y_s[1] = jnp.zeros(y_s.shape[1:], _BF16)

    @pl.when(first_a)
    def _():
        k_s[sa, :, 0:KV_HALO, :] = jnp.zeros((4, KV_HALO, 128), _BF16)
        v_s[sa, :, 0:KV_HALO, :] = jnp.zeros((4, KV_HALO, 256), _BF16)
        for cb in range(N_SLABS):
            u_s[sa * N_SLABS + cb, 0:T_BLK, :] = jnp.zeros((T_BLK, 128), _F32)

    @pl.when(jnp.logical_not(first_a))
    def _():
        k_s[sa, :, 0:KV_HALO, :] = k_s[sb, :, ts:ts + KV_HALO, :]
        v_s[sa, :, 0:KV_HALO, :] = v_s[sb, :, ts:ts + KV_HALO, :]
        for cb in range(N_SLABS):
            u_s[sa * N_SLABS + cb, 0:T_BLK, :] = u_s[sb * N_SLABS + cb,
                                                     N_TBLK * U_PITCH:N_TBLK * U_PITCH + T_BLK, :]

    A = _stream_a(sa, x_ref, xres_ref, nw_ref, win_ref, qw_ref, kw_ref, bd_ref,
                  h_s, p_s, ssq_s, q_s, k_s, v_s, u_s, ga_s, gc_s)
    Bm = _stream_b(sb, first_b, functools.partial(_sink, sink_ref, layer), wb_ref, dwb_ref, lnw_ref, lnb_ref,
                   q_s, k_s, v_s, u_s, u2_s, ga_s, gc_s, c_s, yc_s, y_s, bias_s, o_s, dn_s)
    C = _stream_c(sa, xres_ref, wout_ref, o_ref, y_s)
    af, an = Bm["attn_front"], Bm["attn_norm"]

    regions = [
        Bm["pack"] + af + an,
        [item for pair in zip(C, Bm["conv"]) for item in pair],
        A["gate_c_prev"] + Bm["ln"] + Bm["regate"] + [A["dot_qkvz"]]
        + A["q_sumsq"] + [A["k_sumsq"]] + A["gate_a"] + A["q_norm"]
        + [A["kv_store"]] + A["dot_ag"] + A["glu"] + [A["dot_zc"]] + A["rms_next"],
    ]
    fill_regions = [
        A["rms_first"],
        [A["dot_qkvz"], A["dot_zc"]],
        A["q_sumsq"] + [A["k_sumsq"]] + A["gate_a"] + A["q_norm"]
        + [A["kv_store"]] + A["dot_ag"],
        A["glu"] + A["rms_next"],
    ]
    drain_regions = [
        Bm["pack"] + af + Bm["conv"] + an,
        A["gate_c_prev"] + Bm["ln"] + Bm["regate"],
        C,
    ]
    last_regions = [C]
    schedule = [(regions, (t >= 1) & (t < n_tiles)),
                (fill_regions, t == 0),
                (drain_regions, t == n_tiles),
                (last_regions, t == n_tiles + 1)]
    for group, active in schedule:
        trips = jnp.where(active, 1, 0)
        for region in group:
            def body(_, carry, region=region):
                for item in region:
                    item()
                return carry
            lax.fori_loop(0, trips, body, 0)


def _layer(x, layer, norm_w, w_in, q_norm_w, k_norm_w, sinks, dw_w, dw_b, ln_w, ln_b, w_out, rel_bias):
    B, S, D = x.shape
    ts = SEQ_TILE
    assert D == D_MODEL and S % ts == 0 and ts % BLOCK == 0
    n_tiles = B * S // ts

    idx = np.arange(256) // HEAD_DIM
    bd = jnp.asarray((idx[:, None] == idx[None, :]).astype(np.float32), dtype=_BF16)
    qw = (jnp.tile(q_norm_w[layer].astype(_F32), N_HEADS) * (HEAD_DIM ** -0.5 * LOG2E)).reshape(1, D_ATTN)
    kw = jnp.tile(k_norm_w[layer].astype(_F32), N_KV_HEADS).reshape(1, 128)
    xt = x.reshape(n_tiles, ts, D)

    const2 = lambda t: (0, 0)
    layer_row = lambda t: (layer, 0)
    in_specs = [
        pl.BlockSpec(memory_space=pltpu.SMEM),
        pl.BlockSpec(memory_space=pltpu.SMEM),
        pl.BlockSpec((None, ts, D), lambda t: (jnp.minimum(t + 1, n_tiles - 1), 0, 0)),
        pl.BlockSpec((None, ts, D), lambda t: (jnp.maximum(t - 2, 0), 0, 0)),
        pl.BlockSpec((1, D), layer_row),
        pl.BlockSpec(memory_space=pl.ANY),
        pl.BlockSpec((1, D_ATTN), const2),
        pl.BlockSpec((1, 128), const2),
        pl.BlockSpec((256, 256), const2),
        pl.BlockSpec((None, CONV_WIDTH, D_CONV), lambda t: (layer, 0, 0)),
        pl.BlockSpec((1, D_CONV), layer_row),
        pl.BlockSpec((1, D_CONV), layer_row),
        pl.BlockSpec((1, D_CONV), layer_row),
        pl.BlockSpec(memory_space=pl.ANY),
        pl.BlockSpec((BLOCK, 2 * BLOCK), const2),
    ]
    scratch = [
        pltpu.VMEM((D, D_IN), _BF16),
        pltpu.VMEM((D, D), _BF16),
        pltpu.VMEM((W_SLOTS, W_ROWS, D_IN), _F32),
        pltpu.SemaphoreType.DMA((W_SLOTS,)),
        pltpu.VMEM((CONV_WIDTH * BF16_SUBLANES, D_CONV), _BF16),
        pltpu.VMEM((ts, D), _BF16),
        pltpu.VMEM((ts, UA0 - Q0), _F32),
        pltpu.VMEM((ts, ZC0 - UA0), _F32),
        pltpu.VMEM((ts, D_IN - ZC0), _F32),
        pltpu.VMEM((ts, D_ATTN + 128), _F32),
        pltpu.VMEM((2, ts, D_ATTN), _BF16),
        pltpu.VMEM((2, 4, ts + KV_HALO, 128), _BF16),
        pltpu.VMEM((2, 4, ts + KV_HALO, 256), _BF16),
        pltpu.VMEM((2 * N_SLABS, U_ROWS, 128), _F32),
        pltpu.VMEM((N_SLABS, N_UTILES * BF16_SUBLANES, 128), _BF16),
        pltpu.VMEM((2, ts, D_ATTN), _F32),
        pltpu.VMEM((2, ts, D_CONV), _F32),
        pltpu.VMEM((ts, D_CONV), _F32),
        pltpu.VMEM((N_SLABS, C_ROWS, 128), _F32),
        pltpu.VMEM((2, ts, D), _BF16),
        pltpu.VMEM((2, 4, 2 * BLOCK, 2 * BLOCK), _F32),
        pltpu.VMEM((ATTN_SLOTS, 2 * BLOCK, 2 * BLOCK), _F32),
        pltpu.VMEM((ATTN_SLOTS, 2, BLOCK, 128), _F32),
    ]
    out = pl.pallas_call(
        functools.partial(_layer_kernel, layer=layer, tiles_per_seq=S // ts, n_tiles=n_tiles),
        grid=(n_tiles + 2,),
        in_specs=in_specs,
        out_specs=pl.BlockSpec((None, ts, D), lambda t: (jnp.maximum(t - 2, 0), 0, 0)),
        out_shape=jax.ShapeDtypeStruct((n_tiles, ts, D), x.dtype),
        scratch_shapes=scratch,
        compiler_params=pltpu.CompilerParams(
            dimension_semantics=("arbitrary",),
            vmem_limit_bytes=VMEM_LIMIT_BYTES),
        name="hybrid_layer",
    )(sinks.astype(_F32), rel_bias.astype(_F32), xt, xt,
      norm_w.astype(_F32), w_in.astype(_F32), qw, kw, bd,
      dw_w.astype(_F32), dw_b.astype(_F32), ln_w.astype(_F32),
      ln_b.astype(_F32), w_out.astype(_F32), jnp.asarray(_banded_buckets()))
    return out.reshape(B, S, D)


def kernel(x, norm_w, w_in, q_norm_w, k_norm_w, sinks, dw_w, dw_b, ln_w, ln_b, w_out, rel_bias):
    for layer in range(norm_w.shape[0]):
        x = _layer(x, layer, norm_w, w_in, q_norm_w, k_norm_w, sinks, dw_w, dw_b, ln_w, ln_b, w_out, rel_bias)
    return x
```

```python
import functools
import math

import numpy as np
import jax
import jax.numpy as jnp
from jax import lax
from jax.experimental import pallas as pl
from jax.experimental.pallas import tpu as pltpu

D_MODEL = 1024
D_ATTN = 512
D_CONV = 512
HEAD_DIM = 64
N_HEADS = 8
N_KV_HEADS = 2
BLOCK = 128
NUM_BUCKETS = 32
MAX_DISTANCE = 128
CONV_WIDTH = 31
D_IN = 2816
EPS = 1e-6
LN_EPS = 1e-5
NEG_INF = -1e30
LOG2E = math.log2(math.e)

SEQ_TILE = 512
KV_HALO = BLOCK
ROW_CHUNK = 64
SUBLANES = 8
N_SLABS = D_CONV // 128
BF16_SUBLANES = 16
N_VACC = 4
N_TBLK = BF16_SUBLANES
T_BLK = SEQ_TILE // N_TBLK
U_PITCH = T_BLK + 1
U_ROWS = ((N_TBLK + 1) * U_PITCH + 7) // 8 * 8
N_UTILES = CONV_WIDTH - 1 + T_BLK
C_PITCH = T_BLK + 8
C_ROWS = N_TBLK * C_PITCH
N_CHUNK = 256
W_ROWS = 128
W_SLOTS = 4
ATTN_SLOTS = 8
VMEM_LIMIT_BYTES = 56 * 1024 * 1024

Q0, K0, V0, ZA0, UA0, UG0, ZC0 = 0, 512, 640, 768, 1280, 1792, 2304

_F32 = jnp.float32
_BF16 = jnp.bfloat16


def _t5_bucket_table():
    qi = np.arange(BLOCK)[:, None]
    sj = np.arange(2 * BLOCK)[None, :]
    dist = qi + BLOCK - sj
    n = np.maximum(dist, 0)
    max_exact = NUM_BUCKETS // 2
    nf = np.maximum(n, 1).astype(np.float32)
    large = max_exact + (np.log(nf / max_exact) / math.log(MAX_DISTANCE / max_exact)
                         * (NUM_BUCKETS - max_exact)).astype(np.int32)
    large = np.minimum(large, NUM_BUCKETS - 1)
    bucket = np.where(n < max_exact, n, large).astype(np.int32)
    band = (dist >= 0) & (dist < BLOCK)
    return bucket, band


def _banded_buckets():
    bucket, band = _t5_bucket_table()
    return np.where(band, bucket, -1).astype(np.int32)


def _sigmoid(z):
    return 0.5 * jnp.tanh(0.5 * z) + 0.5


def _silu(z):
    h = 0.5 * z
    return h + h * jnp.tanh(h)


def _build_bias_tables(relb_ref, bucket_ref, bias_s):
    cur_keys = lax.broadcasted_iota(jnp.int32, (8, 2 * BLOCK), 1) >= BLOCK

    def body(c, carry):
        r0 = pl.multiple_of(c * 8, 8)
        bk = bucket_ref[pl.ds(r0, 8), :]
        accs = [jnp.full((8, 2 * BLOCK), NEG_INF, _F32) for _ in range(N_HEADS)]
        for b in range(NUM_BUCKETS):
            hit = bk == b
            for h in range(N_HEADS):
                accs[h] = jnp.where(hit, relb_ref[b, h] * LOG2E, accs[h])
        for h in range(N_HEADS):
            g, rem = divmod(h, 4)
            j, e = divmod(rem, 2)
            rows = pl.ds(BLOCK * j + r0, 8)
            bias_s[0, 2 * g + e, rows, :] = accs[h]
            bias_s[1, 2 * g + e, rows, :] = jnp.where(cur_keys, accs[h], NEG_INF)
        return carry

    lax.fori_loop(0, BLOCK // 8, body, 0)


class _RawProj:
    bounds = (Q0, UA0, ZC0, D_IN)

    def __init__(self, *refs):
        self.refs = refs

    def _find(self, c0, c1):
        for ref, lo, hi in zip(self.refs, self.bounds[:-1], self.bounds[1:]):
            if lo <= c0 and c1 <= hi:
                return ref, c0 - lo, c1 - lo
        raise ValueError((c0, c1))

    def load(self, rows, c0, c1):
        ref, a, b = self._find(c0, c1)
        return ref[rows, a:b]

    def store(self, c0, c1, val):
        ref, a, b = self._find(c0, c1)
        ref[:, a:b] = val


def _sink(sink_ref, layer, head):
    return sink_ref[layer, head]


def _row_chunks():
    return [slice(r, r + ROW_CHUNK) for r in range(0, SEQ_TILE, ROW_CHUNK)]


def _stream_a(sa, x_ref, x0_ref, nw_ref, win_ref, qw_ref, kw_ref, bd_ref,
              h_s, p_s, ssq_s, q_s, k_s, v_s, u_s, u2_s, ga_s, gc_s):
    ts = SEQ_TILE

    def rms(src_ref, r):
        def run():
            xc = src_ref[r:r + ROW_CHUNK, :]
            ss = jnp.sum(xc * xc, axis=-1, keepdims=True)
            rs = lax.rsqrt(ss * (1.0 / D_MODEL) + EPS)
            h_s[r:r + ROW_CHUNK, :] = ((xc * rs) * nw_ref[...]).astype(_BF16)
        return run

    def dot(c0, c1):
        def run():
            p_s.store(c0, c1, jnp.dot(h_s[...], win_ref[:, c0:c1], preferred_element_type=_F32))
        return run

    def q_sumsq(half):
        def run():
            cs = slice(256 * half, 256 * half + 256)
            q = p_s.load(slice(None), Q0 + cs.start, Q0 + cs.stop)
            ssq_s[:, cs] = jnp.dot((q * q).astype(_BF16), bd_ref[...], preferred_element_type=_F32)
        return run

    def q_norm(half):
        def run():
            cs = slice(256 * half, 256 * half + 256)
            for rr in _row_chunks():
                q = p_s.load(rr, Q0 + cs.start, Q0 + cs.stop)
                qn =(q * lax.rsqrt(ssq_s[rr, cs] * (1.0 / HEAD_DIM) + EPS)) * qw_ref[:, cs]
                q_s[sa, rr, cs] = qn.astype(_BF16)
        return run

    def k_sumsq():
        k = p_s.load(slice(None), K0, K0 + 128)
        ssq_s[:, D_ATTN:D_ATTN + 128] = jnp.dot((k * k).astype(_BF16), bd_ref[0:128, 0:128],
                                                preferred_element_type=_F32)

    def kv_store():
        lo = lax.broadcasted_iota(jnp.int32, (ROW_CHUNK, 128), 1) < HEAD_DIM
        zero = jnp.zeros((ROW_CHUNK, 128), _F32)
        one = jnp.ones((ROW_CHUNK, 128), _F32)
        ones_lo = jnp.where(lo, one, zero).astype(_BF16)
        ones_hi = jnp.where(lo, zero, one).astype(_BF16)
        for rr in _row_chunks():
            cur = slice(KV_HALO + rr.start, KV_HALO + rr.stop)
            k = p_s.load(rr, K0, K0 + 128)
            v = p_s.load(rr, V0, V0 + 128)
            kn = (k * lax.rsqrt(ssq_s[rr, D_ATTN:D_ATTN + 128] * (1.0 / HEAD_DIM) + EPS)) * kw_ref[...]
            kn_sw = pltpu.roll(kn, 64, axis=1)
            v_sw = pltpu.roll(v, 64, axis=1)
            k_s[sa, 0, cur, :] = jnp.where(lo, kn, zero).astype(_BF16)
            k_s[sa, 1, cur, :] = jnp.where(lo, zero, kn_sw).astype(_BF16)
            k_s[sa, 2, cur, :] = jnp.where(lo, kn_sw, zero).astype(_BF16)
            k_s[sa, 3, cur, :] = jnp.where(lo, zero, kn).astype(_BF16)
            v_s[sa, 0, cur, 0:128] = jnp.where(lo, v, zero).astype(_BF16)
            v_s[sa, 1, cur, 0:128] = jnp.where(lo, zero, v_sw).astype(_BF16)
            v_s[sa, 2, cur, 0:128] = jnp.where(lo, v_sw, zero).astype(_BF16)
            v_s[sa, 3, cur, 0:128] = jnp.where(lo, zero, v).astype(_BF16)
            v_s[sa, 0, cur, 128:256] = ones_lo
            v_s[sa, 1, cur, 128:256] = ones_hi
            v_s[sa, 2, cur, 128:256] = ones_lo
            v_s[sa, 3, cur, 128:256] = ones_hi

    def gate(c0, dst, half, slot):
        def run():
            cs = slice(256 * half, 256 * half + 256)
            for rr in _row_chunks():
                z = p_s.load(rr, c0 + 256 * half, c0 + 256 * half + 256)
                dst[slot, rr, cs] = _silu(z)
        return run

    def glu(half):
        def run():
            for blk in range(N_TBLK):
                tr = slice(T_BLK * blk, T_BLK * (blk + 1))
                a = p_s.load(tr, UA0 + 256 * half, UA0 + 256 * half + 256)
                g = p_s.load(tr, UG0 + 256 * half, UG0 + 256 * half + 256)
                u = a * _sigmoid(g)
                p0 = U_PITCH * (blk + 1)
                for sl in range(2):
                    u_s[sa * N_SLABS + 2 * half + sl, p0:p0 + T_BLK, :] = u[:, 128 * sl:128 * sl + 128]
        return run

    def pack(cb):
        taps = CONV_WIDTH - 1

        def run():
            for k in range(N_UTILES):
                tau = k - taps
                base = U_PITCH + tau if tau >= 0 else T_BLK + tau
                halves = [u_s[sa * N_SLABS + cb,
                              pl.ds(base + h * SUBLANES * U_PITCH, SUBLANES, stride=U_PITCH), :]
                          for h in range(N_TBLK // SUBLANES)]
                u2_s[cb, BF16_SUBLANES * k:BF16_SUBLANES * (k + 1), :] = (
                    jnp.concatenate(halves, axis=0).astype(_BF16))
        return run

    return dict(
        pack=[pack(cb) for cb in range(N_SLABS)],
        rms_next=[rms(x_ref, r) for r in range(0, ts, ROW_CHUNK)],
        rms_first=[rms(x0_ref, r) for r in range(0, ts, ROW_CHUNK)],
        dot_qkvz=dot(Q0, UA0), dot_zc=dot(ZC0, D_IN),
        dot_ag=[dot(c0 + 256 * half, c0 + 256 * half + 256) for half in range(2) for c0 in (UA0, UG0)],
        q_sumsq=[q_sumsq(0), q_sumsq(1)], q_norm=[q_norm(0), q_norm(1)],
        k_sumsq=k_sumsq, kv_store=kv_store,
        gate_a=[gate(ZA0, ga_s, 0, sa), gate(ZA0, ga_s, 1, sa)],
        gate_c_prev=[gate(ZC0, gc_s, 0, 1 - sa), gate(ZC0, gc_s, 1, 1 - sa)],
        glu=[glu(0), glu(1)])


def _stream_b(sb, first_b, sink_of, wb_ref, dwb_ref, lnw_ref, lnb_ref,
              q_s, k_s, v_s, u2_s, ga_s, gc_s, c_s, yc_s, y_s, bias_s, o_s, dn_s):
    ts = SEQ_TILE
    taps = CONV_WIDTH - 1

    def conv(cb):
        def run():
            ls = slice(128 * cb, 128 * cb + 128)
            bv = jnp.broadcast_to(dwb_ref[:, ls], (BF16_SUBLANES, 128))
            wv = [wb_ref[BF16_SUBLANES * j:BF16_SUBLANES * (j + 1), ls].astype(_F32) for j in range(CONV_WIDTH)]
            for r0 in range(0, T_BLK, N_VACC):
                accs = [None] * N_VACC
                for k in range(r0, r0 + N_VACC + taps):
                    win = u2_s[cb, BF16_SUBLANES * k:BF16_SUBLANES * (k + 1), :].astype(_F32)
                    for n in range(N_VACC):
                        j = k - (r0 + n)
                        if 0 <= j < CONV_WIDTH:
                            term = win * wv[j]
                            accs[n] = term if accs[n] is None else accs[n] + term
                for n in range(N_VACC):
                    r = r0 + n
                    c_s[BF16_SUBLANES * r:BF16_SUBLANES * (r + 1), ls] = accs[n] + bv
        return run

    def ln(r):
        def run():
            y = c_s[r:r + ROW_CHUNK, :]
            mu = jnp.sum(y, axis=-1, keepdims=True) * (1.0 / D_CONV)
            d = y - mu
            var = jnp.sum(d * d, axis=-1, keepdims=True) * (1.0 / D_CONV)
            yn = (d * lax.rsqrt(var + LN_EPS)) * lnw_ref[...] + lnb_ref[...]
            act = _silu(yn)
            for n in range(ROW_CHUNK // SUBLANES):
                row = r + SUBLANES * n
                tau, blk0 = divmod(row, BF16_SUBLANES)
                for cb in range(N_SLABS):
                    yc_s[cb, pl.ds(C_PITCH * blk0 + tau, SUBLANES, stride=C_PITCH), :] = (
                        act[SUBLANES * n:SUBLANES * (n + 1), 128 * cb:128 * cb + 128])
        return run

    def regate(blk):
        def run():
            tr = slice(T_BLK * blk, T_BLK * (blk + 1))
            for cb in range(N_SLABS):
                ls = slice(128 * cb, 128 * cb + 128)
                val = yc_s[cb, C_PITCH * blk:C_PITCH * blk + T_BLK, :] * gc_s[sb, tr, ls]
                y_s[sb, tr, D_ATTN + 128 * cb:D_ATTN + 128 * cb + 128] = val.astype(_BF16)
        return run

    def attn_front(b, g):
        def run():
            slot = (2 * b + g) % ATTN_SLOTS
            rows = slice(BLOCK * b, BLOCK * b + BLOCK)
            krows = slice(BLOCK * b, BLOCK * b + 2 * BLOCK)
            tbl = first_b if b == 0 else 0
            lo_q = lax.broadcasted_iota(jnp.int32, (BLOCK, 128), 1) < HEAD_DIM
            c0 = 256 * g
            qg = jnp.concatenate([q_s[sb, rows, c0:c0 + 128], q_s[sb, rows, c0 + 128:c0 + 256]], axis=0)
            kk = jnp.concatenate([k_s[sb, 2 * g, krows, :], k_s[sb, 2 * g + 1, krows, :]], axis=0)
            s = lax.dot_general(qg, kk, (((1,), (1,)), ((), ())),
                                preferred_element_type=_F32)
            ps = [[None, None], [None, None]]
            ex = [[None, None], [None, None]]
            for e in range(2):
                se = s[:, 2 * BLOCK * e:2 * BLOCK * (e + 1)] + bias_s[tbl, 2 * g + e]
                for j in range(2):
                    sink = sink_of(4 * g + 2 * j + e) * LOG2E
                    sj = se[BLOCK * j:BLOCK * j + BLOCK, :]
                    m = jnp.maximum(jnp.max(sj, axis=-1, keepdims=True), sink)
                    ps[j][e] = jnp.exp2(sj - m).astype(_BF16)
                    ex[j][e] = jnp.exp2(sink - m)
            p = jnp.concatenate([jnp.concatenate(ps[0], axis=1), jnp.concatenate(ps[1], axis=1)], axis=0)
            vv = jnp.concatenate([v_s[sb, 2 * g, krows, :], v_s[sb, 2 * g + 1, krows, :]], axis=0)
            o_s[slot] = jnp.dot(p, vv, preferred_element_type=_F32)
            for j in range(2):
                dn_s[slot, j] = jnp.where(lo_q, ex[j][0], ex[j][1])
        return run

    def attn_norm(b, g):
        def run():
            slot = (2 * b + g) % ATTN_SLOTS
            rows = slice(BLOCK * b, BLOCK * b + BLOCK)
            for j in range(2):
                oj = o_s[slot, BLOCK * j:BLOCK * j + BLOCK, :]
                den = oj[:, 128:256] + dn_s[slot, j]
                cs = slice(256 * g + 128 * j, 256 * g + 128 * j + 128)
                y_s[sb, rows, cs] = ((oj[:, 0:128] / den) * ga_s[sb, rows, cs]).astype(_BF16)
        return run

    pairs = [(b, g) for b in range(ts // BLOCK) for g in range(N_KV_HEADS)]
    return dict(attn_front=[attn_front(b, g) for b, g in pairs],
                attn_norm=[attn_norm(b, g) for b, g in pairs],
                conv=[conv(cb) for cb in range(N_SLABS)],
                ln=[ln(r) for r in range(0, ts, ROW_CHUNK)],
                regate=[regate(blk) for blk in range(N_TBLK)])


def _stream_c(sy, xres_ref, wout_ref, o_ref, y_s):
    def outproj(n):
        def run():
            o_ref[:, n:n + N_CHUNK] = xres_ref[:, n:n + N_CHUNK] + jnp.dot(
                y_s[sy], wout_ref[:, n:n + N_CHUNK], preferred_element_type=_F32)
        return run

    return [outproj(n) for n in range(0, D_MODEL, N_CHUNK)]


def _load_weight_bf16(w_hbm, w_s, stage, sem):
    rows, cols = w_s.shape
    n_chunks = rows // W_ROWS
    n_slots = stage.shape[0]

    def copy(c):
        slot = c % n_slots
        return pltpu.make_async_copy(w_hbm.at[pl.ds(c * W_ROWS, W_ROWS), :],
                                     stage.at[slot, :, pl.ds(0, cols)], sem.at[slot])

    for c in range(min(n_slots - 1, n_chunks)):
        copy(c).start()
    for c in range(n_chunks):
        nxt = c + n_slots - 1
        if nxt < n_chunks:
            copy(nxt).start()
        copy(c).wait()
        w_s[c * W_ROWS:(c + 1) * W_ROWS, :] = stage[c % n_slots, :, 0:cols].astype(_BF16)


def _layer_kernel(sink_ref, relb_ref, x_ref, xres_ref, nw_ref, win_hbm, qw_ref, kw_ref, bd_ref, dww_ref,
                  dwb_ref, lnw_ref, lnb_ref, wout_hbm, bucket_ref,
                  o_ref,
                  win_ref, wout_ref, wstage_s, wsem, wb_ref,
                  h_s, p1_s, p2_s, p3_s, ssq_s, q_s, k_s, v_s, u_s, u2_s, ga_s, gc_s, c_s, yc_s, y_s,
                  bias_s, o_s, dn_s, *, layer, tiles_per_seq, n_tiles):
    p_s = _RawProj(p1_s, p2_s, p3_s)
    ts = SEQ_TILE
    t = pl.program_id(0)
    sa = lax.rem(t, 2)
    sb = 1 - sa
    first_a = lax.rem(t, tiles_per_seq) == 0
    first_b = jnp.where(lax.rem(t + tiles_per_seq - 1, tiles_per_seq) == 0, 1, 0)

    @pl.when(t == 0)
    def _():
        _load_weight_bf16(win_hbm.at[layer], win_ref, wstage_s, wsem)
        _load_weight_bf16(wout_hbm.at[layer], wout_ref, wstage_s, wsem)
        _build_bias_tables(relb_ref, bucket_ref, bias_s)
        for j in range(CONV_WIDTH):
            wb_ref[BF16_SUBLANES * j:BF16_SUBLANES * (j + 1), :] = jnp.broadcast_to(
                dww_ref[j:j + 1, :], (BF16_SUBLANES, D_CONV)).astype(_BF16)
        y_s[1] = jnp.zeros(y_s.shape[1:], _BF16)

    @pl.when(first_a)
    def _():
        k_s[sa, :, 0:KV_HALO, :] = jnp.zeros((4, KV_HALO, 128), _BF16)
        v_s[sa, :, 0:KV_HALO, :] = jnp.zeros((4, KV_HALO, 256), _BF16)
        for cb in range(N_SLABS):
            u_s[sa * N_SLABS + cb, 0:T_BLK, :] = jnp.zeros((T_BLK, 128), _F32)

    @pl.when(jnp.logical_not(first_a))
    def _():
        k_s[sa, :, 0:KV_HALO, :] = k_s[sb, :, ts:ts + KV_HALO, :]
        v_s[sa, :, 0:KV_HALO, :] = v_s[sb, :, ts:ts + KV_HALO, :]
        for cb in range(N_SLABS):
            u_s[sa * N_SLABS + cb, 0:T_BLK, :] = u_s[sb * N_SLABS + cb,
                                                     N_TBLK * U_PITCH:N_TBLK * U_PITCH + T_BLK, :]

    A = _stream_a(sa, x_ref, xres_ref, nw_ref, win_ref, qw_ref, kw_ref, bd_ref,
                  h_s, p_s, ssq_s, q_s, k_s, v_s, u_s, u2_s, ga_s, gc_s)
    Bm = _stream_b(sb, first_b, functools.partial(_sink, sink_ref, layer), wb_ref, dwb_ref, lnw_ref, lnb_ref,
                   q_s, k_s, v_s, u2_s, ga_s, gc_s, c_s, yc_s, y_s, bias_s, o_s, dn_s)
    C = _stream_c(sa, xres_ref, wout_ref, o_ref, y_s)
    af, an = Bm["attn_front"], Bm["attn_norm"]

    regions = [
        af + an,
        [item for pair in zip(C, Bm["conv"]) for item in pair],
        A["gate_c_prev"] + Bm["ln"] + Bm["regate"] + [A["dot_qkvz"]]
        + A["q_sumsq"] + [A["k_sumsq"]] + A["gate_a"] + A["q_norm"]
        + [A["kv_store"]] + A["dot_ag"] + A["glu"] + A["pack"] + [A["dot_zc"]] + A["rms_next"],
    ]
    fill_regions = [
        A["rms_first"],
        [A["dot_qkvz"], A["dot_zc"]],
        A["q_sumsq"] + [A["k_sumsq"]] + A["gate_a"] + A["q_norm"]
        + [A["kv_store"]] + A["dot_ag"],
        A["glu"] + A["pack"] + A["rms_next"],
    ]
    drain_regions = [
        af + Bm["conv"] + an,
        A["gate_c_prev"] + Bm["ln"] + Bm["regate"],
        C,
    ]
    last_regions = [C]
    schedule = [(regions, (t >= 1) & (t < n_tiles)),
                (fill_regions, t == 0),
                (drain_regions, t == n_tiles),
                (last_regions, t == n_tiles + 1)]
    for group, active in schedule:
        trips = jnp.where(active, 1, 0)
        for region in group:
            def body(_, carry, region=region):
                for item in region:
                    item()
                return carry
            lax.fori_loop(0, trips, body, 0)


def _layer(x, layer, norm_w, w_in, q_norm_w, k_norm_w, sinks, dw_w, dw_b, ln_w, ln_b, w_out, rel_bias):
    B, S, D = x.shape
    ts = SEQ_TILE
    assert D == D_MODEL and S % ts == 0 and ts % BLOCK == 0
    n_tiles = B * S // ts

    idx = np.arange(256) // HEAD_DIM
    bd = jnp.asarray((idx[:, None] == idx[None, :]).astype(np.float32), dtype=_BF16)
    qw = (jnp.tile(q_norm_w[layer].astype(_F32), N_HEADS) * (HEAD_DIM ** -0.5 * LOG2E)).reshape(1, D_ATTN)
    kw = jnp.tile(k_norm_w[layer].astype(_F32), N_KV_HEADS).reshape(1, 128)
    xt = x.reshape(n_tiles, ts, D)

    const2 = lambda t: (0, 0)
    layer_row = lambda t: (layer, 0)
    in_specs = [
        pl.BlockSpec(memory_space=pltpu.SMEM),
        pl.BlockSpec(memory_space=pltpu.SMEM),
        pl.BlockSpec((None, ts, D), lambda t: (jnp.minimum(t + 1, n_tiles - 1), 0, 0)),
        pl.BlockSpec((None, ts, D), lambda t: (jnp.maximum(t - 2, 0), 0, 0)),
        pl.BlockSpec((1, D), layer_row),
        pl.BlockSpec(memory_space=pl.ANY),
        pl.BlockSpec((1, D_ATTN), const2),
        pl.BlockSpec((1, 128), const2),
        pl.BlockSpec((256, 256), const2),
        pl.BlockSpec((None, CONV_WIDTH, D_CONV), lambda t: (layer, 0, 0)),
        pl.BlockSpec((1, D_CONV), layer_row),
        pl.BlockSpec((1, D_CONV), layer_row),
        pl.BlockSpec((1, D_CONV), layer_row),
        pl.BlockSpec(memory_space=pl.ANY),
        pl.BlockSpec((BLOCK, 2 * BLOCK), const2),
    ]
    scratch = [
        pltpu.VMEM((D, D_IN), _BF16),
        pltpu.VMEM((D, D), _BF16),
        pltpu.VMEM((W_SLOTS, W_ROWS, D_IN), _F32),
        pltpu.SemaphoreType.DMA((W_SLOTS,)),
        pltpu.VMEM((CONV_WIDTH * BF16_SUBLANES, D_CONV), _BF16),
        pltpu.VMEM((ts, D), _BF16),
        pltpu.VMEM((ts, UA0 - Q0), _F32),
        pltpu.VMEM((ts, ZC0 - UA0), _F32),
        pltpu.VMEM((ts, D_IN - ZC0), _F32),
        pltpu.VMEM((ts, D_ATTN + 128), _F32),
        pltpu.VMEM((2, ts, D_ATTN), _BF16),
        pltpu.VMEM((2, 4, ts + KV_HALO, 128), _BF16),
        pltpu.VMEM((2, 4, ts + KV_HALO, 256), _BF16),
        pltpu.VMEM((2 * N_SLABS, U_ROWS, 128), _F32),
        pltpu.VMEM((N_SLABS, N_UTILES * BF16_SUBLANES, 128), _BF16),
        pltpu.VMEM((2, ts, D_ATTN), _F32),
        pltpu.VMEM((2, ts, D_CONV), _F32),
        pltpu.VMEM((ts, D_CONV), _F32),
        pltpu.VMEM((N_SLABS, C_ROWS, 128), _F32),
        pltpu.VMEM((2, ts, D), _BF16),
        pltpu.VMEM((2, 4, 2 * BLOCK, 2 * BLOCK), _F32),
        pltpu.VMEM((ATTN_SLOTS, 2 * BLOCK, 2 * BLOCK), _F32),
        pltpu.VMEM((ATTN_SLOTS, 2, BLOCK, 128), _F32),
    ]
    out = pl.pallas_call(
        functools.partial(_layer_kernel, layer=layer, tiles_per_seq=S // ts, n_tiles=n_tiles),
        grid=(n_tiles + 2,),
        in_specs=in_specs,
        out_specs=pl.BlockSpec((None, ts, D), lambda t: (jnp.maximum(t - 2, 0), 0, 0)),
        out_shape=jax.ShapeDtypeStruct((n_tiles, ts, D), x.dtype),
        scratch_shapes=scratch,
        compiler_params=pltpu.CompilerParams(
            dimension_semantics=("arbitrary",),
            vmem_limit_bytes=VMEM_LIMIT_BYTES),
        name="hybrid_layer",
    )(sinks.astype(_F32), rel_bias.astype(_F32), xt, xt,
      norm_w.astype(_F32), w_in.astype(_F32), qw, kw, bd,
      dw_w.astype(_F32), dw_b.astype(_F32), ln_w.astype(_F32),
      ln_b.astype(_F32), w_out.astype(_F32), jnp.asarray(_banded_buckets()))
    return out.reshape(B, S, D)


def kernel(x, norm_w, w_in, q_norm_w, k_norm_w, sinks, dw_w, dw_b, ln_w, ln_b, w_out, rel_bias):
    for layer in range(norm_w.shape[0]):
        x = _layer(x, layer, norm_w, w_in, q_norm_w, k_norm_w, sinks, dw_w, dw_b, ln_w, ln_b, w_out, rel_bias)
    return x
```

```python
import functools
import math

import numpy as np
import jax
import jax.numpy as jnp
from jax import lax
from jax.experimental import pallas as pl
from jax.experimental.pallas import tpu as pltpu

D_MODEL = 1024
D_ATTN = 512
D_CONV = 512
HEAD_DIM = 64
N_HEADS = 8
N_KV_HEADS = 2
BLOCK = 128
NUM_BUCKETS = 32
MAX_DISTANCE = 128
CONV_WIDTH = 31
D_IN = 2816
EPS = 1e-6
LN_EPS = 1e-5
NEG_INF = -1e30
LOG2E = math.log2(math.e)

SEQ_TILE = 512
KV_HALO = BLOCK
ROW_CHUNK = 64
SUBLANES = 8
N_SLABS = D_CONV // 128
BF16_SUBLANES = 16
N_VACC = 4
N_TBLK = BF16_SUBLANES
T_BLK = SEQ_TILE // N_TBLK
U_PITCH = T_BLK + 1
U_ROWS = ((N_TBLK + 1) * U_PITCH + 7) // 8 * 8
N_UTILES = CONV_WIDTH - 1 + T_BLK
C_PITCH = T_BLK + 8
C_ROWS = N_TBLK * C_PITCH
N_CHUNK = 256
W_ROWS = 128
W_SLOTS = 4
ATTN_SLOTS = 8
VMEM_LIMIT_BYTES = 56 * 1024 * 1024

Q0, K0, V0, ZA0, UA0, UG0, ZC0 = 0, 512, 640, 768, 1280, 1792, 2304

_F32 = jnp.float32
_BF16 = jnp.bfloat16


def _t5_bucket_table():
    qi = np.arange(BLOCK)[:, None]
    sj = np.arange(2 * BLOCK)[None, :]
    dist = qi + BLOCK - sj
    n = np.maximum(dist, 0)
    max_exact = NUM_BUCKETS // 2
    nf = np.maximum(n, 1).astype(np.float32)
    large = max_exact + (np.log(nf / max_exact) / math.log(MAX_DISTANCE / max_exact)
                         * (NUM_BUCKETS - max_exact)).astype(np.int32)
    large = np.minimum(large, NUM_BUCKETS - 1)
    bucket = np.where(n < max_exact, n, large).astype(np.int32)
    band = (dist >= 0) & (dist < BLOCK)
    return bucket, band


def _banded_buckets():
    bucket, band = _t5_bucket_table()
    return np.where(band, bucket, -1).astype(np.int32)


def _sigmoid(z):
    return 0.5 * jnp.tanh(0.5 * z) + 0.5


def _silu(z):
    h = 0.5 * z
    return h + h * jnp.tanh(h)


def _bias_table_items(relb_ref, bucket_ref, bias_s):
    def rows8(r0):
        def run():
            cur_keys = lax.broadcasted_iota(jnp.int32, (SUBLANES, 2 * BLOCK), 1) >= BLOCK
            bk = bucket_ref[r0:r0 + SUBLANES, :]
            accs = [jnp.full((SUBLANES, 2 * BLOCK), NEG_INF, _F32) for _ in range(N_HEADS)]
            for b in range(NUM_BUCKETS):
                hit = bk == b
                for h in range(N_HEADS):
                    accs[h] = jnp.where(hit, relb_ref[b, h] * LOG2E, accs[h])
            for h in range(N_HEADS):
                g, rem = divmod(h, 4)
                j, e = divmod(rem, 2)
                rows = slice(BLOCK * j + r0, BLOCK * j + r0 + SUBLANES)
                bias_s[0, 2 * g + e, rows, :] = accs[h]
                bias_s[1, 2 * g + e, rows, :] = jnp.where(cur_keys, accs[h], NEG_INF)
        return run

    return [rows8(r0) for r0 in range(0, BLOCK, SUBLANES)]


class _RawProj:
    bounds = (Q0, UA0, ZC0, D_IN)

    def __init__(self, *refs):
        self.refs = refs

    def _find(self, c0, c1):
        for ref, lo, hi in zip(self.refs, self.bounds[:-1], self.bounds[1:]):
            if lo <= c0 and c1 <= hi:
                return ref, c0 - lo, c1 - lo
        raise ValueError((c0, c1))

    def load(self, rows, c0, c1):
        ref, a, b = self._find(c0, c1)
        return ref[rows, a:b]

    def store(self, c0, c1, val):
        ref, a, b = self._find(c0, c1)
        ref[:, a:b] = val


def _sink(sink_ref, layer, head):
    return sink_ref[layer, head]


def _row_chunks():
    return [slice(r, r + ROW_CHUNK) for r in range(0, SEQ_TILE, ROW_CHUNK)]


def _stream_a(sa, x_ref, x0_ref, nw_ref, win_ref, qw_ref, kw_ref, bd_ref,
              h_s, p_s, ssq_s, q_s, k_s, v_s, u_s, u2_s, ga_s, gc_s):
    ts = SEQ_TILE

    def rms(src_ref, r):
        def run():
            xc = src_ref[r:r + ROW_CHUNK, :]
            ss = jnp.sum(xc * xc, axis=-1, keepdims=True)
            rs = lax.rsqrt(ss * (1.0 / D_MODEL) + EPS)
            h_s[r:r + ROW_CHUNK, :] = ((xc * rs) * nw_ref[...]).astype(_BF16)
        return run

    def dot(c0, c1):
        def run():
            p_s.store(c0, c1, jnp.dot(h_s[...], win_ref[:, c0:c1], preferred_element_type=_F32))
        return run

    def q_sumsq(half):
        def run():
            cs = slice(256 * half, 256 * half + 256)
            q = p_s.load(slice(None), Q0 + cs.start, Q0 + cs.stop)
            ssq_s[:, cs] = jnp.dot((q * q).astype(_BF16), bd_ref[...], preferred_element_type=_F32)
        return run

    def q_norm(half):
        def run():
            cs = slice(256 * half, 256 * half + 256)
            for rr in _row_chunks():
                q = p_s.load(rr, Q0 + cs.start, Q0 + cs.stop)
                qn =(q * lax.rsqrt(ssq_s[rr, cs] * (1.0 / HEAD_DIM) + EPS)) * qw_ref[:, cs]
                q_s[sa, rr, cs] = qn.astype(_BF16)
        return run

    def k_sumsq():
        k = p_s.load(slice(None), K0, K0 + 128)
        ssq_s[:, D_ATTN:D_ATTN + 128] = jnp.dot((k * k).astype(_BF16), bd_ref[0:128, 0:128],
                                                preferred_element_type=_F32)

    def kv_store():
        lo = lax.broadcasted_iota(jnp.int32, (ROW_CHUNK, 128), 1) < HEAD_DIM
        zero = jnp.zeros((ROW_CHUNK, 128), _F32)
        one = jnp.ones((ROW_CHUNK, 128), _F32)
        ones_lo = jnp.where(lo, one, zero).astype(_BF16)
        ones_hi = jnp.where(lo, zero, one).astype(_BF16)
        for rr in _row_chunks():
            cur = slice(KV_HALO + rr.start, KV_HALO + rr.stop)
            k = p_s.load(rr, K0, K0 + 128)
            v = p_s.load(rr, V0, V0 + 128)
            kn = (k * lax.rsqrt(ssq_s[rr, D_ATTN:D_ATTN + 128] * (1.0 / HEAD_DIM) + EPS)) * kw_ref[...]
            kn_sw = pltpu.roll(kn, 64, axis=1)
            v_sw = pltpu.roll(v, 64, axis=1)
            k_s[sa, 0, cur, :] = jnp.where(lo, kn, zero).astype(_BF16)
            k_s[sa, 1, cur, :] = jnp.where(lo, zero, kn_sw).astype(_BF16)
            k_s[sa, 2, cur, :] = jnp.where(lo, kn_sw, zero).astype(_BF16)
            k_s[sa, 3, cur, :] = jnp.where(lo, zero, kn).astype(_BF16)
            v_s[sa, 0, cur, 0:128] = jnp.where(lo, v, zero).astype(_BF16)
            v_s[sa, 1, cur, 0:128] = jnp.where(lo, zero, v_sw).astype(_BF16)
            v_s[sa, 2, cur, 0:128] = jnp.where(lo, v_sw, zero).astype(_BF16)
            v_s[sa, 3, cur, 0:128] = jnp.where(lo, zero, v).astype(_BF16)
            v_s[sa, 0, cur, 128:256] = ones_lo
            v_s[sa, 1, cur, 128:256] = ones_hi
            v_s[sa, 2, cur, 128:256] = ones_lo
            v_s[sa, 3, cur, 128:256] = ones_hi

    def gate(c0, dst, half, slot):
        def run():
            cs = slice(256 * half, 256 * half + 256)
            for rr in _row_chunks():
                z = p_s.load(rr, c0 + 256 * half, c0 + 256 * half + 256)
                dst[slot, rr, cs] = _silu(z)
        return run

    def glu(half):
        def run():
            for blk in range(N_TBLK):
                tr = slice(T_BLK * blk, T_BLK * (blk + 1))
                a = p_s.load(tr, UA0 + 256 * half, UA0 + 256 * half + 256)
                g = p_s.load(tr, UG0 + 256 * half, UG0 + 256 * half + 256)
                u = a * _sigmoid(g)
                p0 = U_PITCH * (blk + 1)
                for sl in range(2):
                    u_s[sa * N_SLABS + 2 * half + sl, p0:p0 + T_BLK, :] = u[:, 128 * sl:128 * sl + 128]
        return run

    def pack(cb):
        taps = CONV_WIDTH - 1

        def run():
            for k in range(N_UTILES):
                tau = k - taps
                base = U_PITCH + tau if tau >= 0 else T_BLK + tau
                halves = [u_s[sa * N_SLABS + cb,
                              pl.ds(base + h * SUBLANES * U_PITCH, SUBLANES, stride=U_PITCH), :]
                          for h in range(N_TBLK // SUBLANES)]
                u2_s[cb, BF16_SUBLANES * k:BF16_SUBLANES * (k + 1), :] = (
                    jnp.concatenate(halves, axis=0).astype(_BF16))
        return run

    return dict(
        pack=[pack(cb) for cb in range(N_SLABS)],
        rms_next=[rms(x_ref, r) for r in range(0, ts, ROW_CHUNK)],
        rms_first=[rms(x0_ref, r) for r in range(0, ts, ROW_CHUNK)],
        dot_qkvz=dot(Q0, UA0), dot_zc=dot(ZC0, D_IN),
        dot_ag=[dot(c0 + 256 * half, c0 + 256 * half + 256) for half in range(2) for c0 in (UA0, UG0)],
        q_sumsq=[q_sumsq(0), q_sumsq(1)], q_norm=[q_norm(0), q_norm(1)],
        k_sumsq=k_sumsq, kv_store=kv_store,
        gate_a=[gate(ZA0, ga_s, 0, sa), gate(ZA0, ga_s, 1, sa)],
        gate_c_prev=[gate(ZC0, gc_s, 0, 1 - sa), gate(ZC0, gc_s, 1, 1 - sa)],
        glu=[glu(0), glu(1)])


def _stream_b(sb, first_b, sink_of, wb_ref, dwb_ref, lnw_ref, lnb_ref,
              q_s, k_s, v_s, u2_s, ga_s, gc_s, c_s, yc_s, y_s, bias_s, o_s, dn_s):
    ts = SEQ_TILE
    taps = CONV_WIDTH - 1

    def conv(cb):
        def run():
            ls = slice(128 * cb, 128 * cb + 128)
            bv = jnp.broadcast_to(dwb_ref[:, ls], (BF16_SUBLANES, 128))
            wv = [wb_ref[BF16_SUBLANES * j:BF16_SUBLANES * (j + 1), ls].astype(_F32) for j in range(CONV_WIDTH)]
            for r0 in range(0, T_BLK, N_VACC):
                accs = [None] * N_VACC
                for k in range(r0, r0 + N_VACC + taps):
                    win = u2_s[cb, BF16_SUBLANES * k:BF16_SUBLANES * (k + 1), :].astype(_F32)
                    for n in range(N_VACC):
                        j = k - (r0 + n)
                        if 0 <= j < CONV_WIDTH:
                            term = win * wv[j]
                            accs[n] = term if accs[n] is None else accs[n] + term
                for n in range(N_VACC):
                    r = r0 + n
                    c_s[BF16_SUBLANES * r:BF16_SUBLANES * (r + 1), ls] = accs[n] + bv
        return run

    def ln(r):
        def run():
            y = c_s[r:r + ROW_CHUNK, :]
            mu = jnp.sum(y, axis=-1, keepdims=True) * (1.0 / D_CONV)
            d = y - mu
            var = jnp.sum(d * d, axis=-1, keepdims=True) * (1.0 / D_CONV)
            yn = (d * lax.rsqrt(var + LN_EPS)) * lnw_ref[...] + lnb_ref[...]
            act = _silu(yn)
            for n in range(ROW_CHUNK // SUBLANES):
                row = r + SUBLANES * n
                tau, blk0 = divmod(row, BF16_SUBLANES)
                for cb in range(N_SLABS):
                    yc_s[cb, pl.ds(C_PITCH * blk0 + tau, SUBLANES, stride=C_PITCH), :] = (
                        act[SUBLANES * n:SUBLANES * (n + 1), 128 * cb:128 * cb + 128])
        return run

    def regate(blk):
        def run():
            tr = slice(T_BLK * blk, T_BLK * (blk + 1))
            for cb in range(N_SLABS):
                ls = slice(128 * cb, 128 * cb + 128)
                val = yc_s[cb, C_PITCH * blk:C_PITCH * blk + T_BLK, :] * gc_s[sb, tr, ls]
                y_s[sb, tr, D_ATTN + 128 * cb:D_ATTN + 128 * cb + 128] = val.astype(_BF16)
        return run

    def attn_front(b, g):
        def run():
            slot = (2 * b + g) % ATTN_SLOTS
            rows = slice(BLOCK * b, BLOCK * b + BLOCK)
            krows = slice(BLOCK * b, BLOCK * b + 2 * BLOCK)
            tbl = first_b if b == 0 else 0
            lo_q = lax.broadcasted_iota(jnp.int32, (BLOCK, 128), 1) < HEAD_DIM
            c0 = 256 * g
            qg = jnp.concatenate([q_s[sb, rows, c0:c0 + 128], q_s[sb, rows, c0 + 128:c0 + 256]], axis=0)
            kk = jnp.concatenate([k_s[sb, 2 * g, krows, :], k_s[sb, 2 * g + 1, krows, :]], axis=0)
            s = lax.dot_general(qg, kk, (((1,), (1,)), ((), ())),
                                preferred_element_type=_F32)
            ps = [[None, None], [None, None]]
            ex = [[None, None], [None, None]]
            for e in range(2):
                se = s[:, 2 * BLOCK * e:2 * BLOCK * (e + 1)] + bias_s[tbl, 2 * g + e]
                for j in range(2):
                    sink = sink_of(4 * g + 2 * j + e) * LOG2E
                    sj = se[BLOCK * j:BLOCK * j + BLOCK, :]
                    m = jnp.maximum(jnp.max(sj, axis=-1, keepdims=True), sink)
                    ps[j][e] = jnp.exp2(sj - m).astype(_BF16)
                    ex[j][e] = jnp.exp2(sink - m)
            p = jnp.concatenate([jnp.concatenate(ps[0], axis=1), jnp.concatenate(ps[1], axis=1)], axis=0)
            vv = jnp.concatenate([v_s[sb, 2 * g, krows, :], v_s[sb, 2 * g + 1, krows, :]], axis=0)
            o_s[slot] = jnp.dot(p, vv, preferred_element_type=_F32)
            for j in range(2):
                dn_s[slot, j] = jnp.where(lo_q, ex[j][0], ex[j][1])
        return run

    def attn_norm(b, g):
        def run():
            slot = (2 * b + g) % ATTN_SLOTS
            rows = slice(BLOCK * b, BLOCK * b + BLOCK)
            for j in range(2):
                oj = o_s[slot, BLOCK * j:BLOCK * j + BLOCK, :]
                den = oj[:, 128:256] + dn_s[slot, j]
                cs = slice(256 * g + 128 * j, 256 * g + 128 * j + 128)
                y_s[sb, rows, cs] = ((oj[:, 0:128] / den) * ga_s[sb, rows, cs]).astype(_BF16)
        return run

    pairs = [(b, g) for b in range(ts // BLOCK) for g in range(N_KV_HEADS)]
    return dict(attn_front=[attn_front(b, g) for b, g in pairs],
                attn_norm=[attn_norm(b, g) for b, g in pairs],
                conv=[conv(cb) for cb in range(N_SLABS)],
                ln=[ln(r) for r in range(0, ts, ROW_CHUNK)],
                regate=[regate(blk) for blk in range(N_TBLK)])


def _stream_c(sy, xres_ref, wout_ref, o_ref, y_s):
    def outproj(n):
        def run():
            o_ref[:, n:n + N_CHUNK] = xres_ref[:, n:n + N_CHUNK] + jnp.dot(
                y_s[sy], wout_ref[:, n:n + N_CHUNK], preferred_element_type=_F32)
        return run

    return [outproj(n) for n in range(0, D_MODEL, N_CHUNK)]


def _load_weight_bf16(w_hbm, w_s, stage, sem, side_work=()):
    rows, cols = w_s.shape
    n_chunks = rows // W_ROWS
    n_slots = stage.shape[0]
    per_chunk = -(-len(side_work) // n_chunks)

    def copy(c):
        slot = c % n_slots
        return pltpu.make_async_copy(w_hbm.at[pl.ds(c * W_ROWS, W_ROWS), :],
                                     stage.at[slot, :, pl.ds(0, cols)], sem.at[slot])

    for c in range(min(n_slots - 1, n_chunks)):
        copy(c).start()
    for c in range(n_chunks):
        nxt = c + n_slots - 1
        if nxt < n_chunks:
            copy(nxt).start()
        for item in side_work[c * per_chunk:(c + 1) * per_chunk]:
            item()
        copy(c).wait()
        w_s[c * W_ROWS:(c + 1) * W_ROWS, :] = stage[c % n_slots, :, 0:cols].astype(_BF16)


def _layer_kernel(sink_ref, relb_ref, x_ref, xres_ref, nw_ref, win_hbm, qw_ref, kw_ref, bd_ref, dww_ref,
                  dwb_ref, lnw_ref, lnb_ref, wout_hbm, bucket_ref,
                  o_ref,
                  win_ref, wout_ref, wstage_s, wsem, wb_ref,
                  h_s, p1_s, p2_s, p3_s, ssq_s, q_s, k_s, v_s, u_s, u2_s, ga_s, gc_s, c_s, yc_s, y_s,
                  bias_s, o_s, dn_s, *, layer, tiles_per_seq, n_tiles):
    p_s = _RawProj(p1_s, p2_s, p3_s)
    ts = SEQ_TILE
    t = pl.program_id(0)
    sa = lax.rem(t, 2)
    sb = 1 - sa
    first_a = lax.rem(t, tiles_per_seq) == 0
    first_b = jnp.where(lax.rem(t + tiles_per_seq - 1, tiles_per_seq) == 0, 1, 0)

    @pl.when(t == 0)
    def _():
        _load_weight_bf16(win_hbm.at[layer], win_ref, wstage_s, wsem,
                          side_work=_bias_table_items(relb_ref, bucket_ref, bias_s))
        _load_weight_bf16(wout_hbm.at[layer], wout_ref, wstage_s, wsem)
        for j in range(CONV_WIDTH):
            wb_ref[BF16_SUBLANES * j:BF16_SUBLANES * (j + 1), :] = jnp.broadcast_to(
                dww_ref[j:j + 1, :], (BF16_SUBLANES, D_CONV)).astype(_BF16)
        y_s[1] = jnp.zeros(y_s.shape[1:], _BF16)

    @pl.when(first_a)
    def _():
        k_s[sa, :, 0:KV_HALO, :] = jnp.zeros((4, KV_HALO, 128), _BF16)
        v_s[sa, :, 0:KV_HALO, :] = jnp.zeros((4, KV_HALO, 256), _BF16)
        for cb in range(N_SLABS):
            u_s[sa * N_SLABS + cb, 0:T_BLK, :] = jnp.zeros((T_BLK, 128), _F32)

    @pl.when(jnp.logical_not(first_a))
    def _():
        k_s[sa, :, 0:KV_HALO, :] = k_s[sb, :, ts:ts + KV_HALO, :]
        v_s[sa, :, 0:KV_HALO, :] = v_s[sb, :, ts:ts + KV_HALO, :]
        for cb in range(N_SLABS):
            u_s[sa * N_SLABS + cb, 0:T_BLK, :] = u_s[sb * N_SLABS + cb,
                                                     N_TBLK * U_PITCH:N_TBLK * U_PITCH + T_BLK, :]

    A = _stream_a(sa, x_ref, xres_ref, nw_ref, win_ref, qw_ref, kw_ref, bd_ref,
                  h_s, p_s, ssq_s, q_s, k_s, v_s, u_s, u2_s, ga_s, gc_s)
    Bm = _stream_b(sb, first_b, functools.partial(_sink, sink_ref, layer), wb_ref, dwb_ref, lnw_ref, lnb_ref,
                   q_s, k_s, v_s, u2_s, ga_s, gc_s, c_s, yc_s, y_s, bias_s, o_s, dn_s)
    C = _stream_c(sa, xres_ref, wout_ref, o_ref, y_s)
    af, an = Bm["attn_front"], Bm["attn_norm"]

    regions = [
        af + an,
        [item for pair in zip(C, Bm["conv"]) for item in pair],
        A["gate_c_prev"] + Bm["ln"] + Bm["regate"] + [A["dot_qkvz"]]
        + A["q_sumsq"] + [A["k_sumsq"]] + A["gate_a"] + A["q_norm"]
        + [A["kv_store"]] + A["dot_ag"] + A["glu"] + A["pack"] + [A["dot_zc"]] + A["rms_next"],
    ]
    fill_regions = [
        A["rms_first"],
        [A["dot_qkvz"], A["dot_zc"]],
        A["q_sumsq"] + [A["k_sumsq"]] + A["gate_a"] + A["q_norm"]
        + [A["kv_store"]] + A["dot_ag"],
        A["glu"] + A["pack"] + A["rms_next"],
    ]
    drain_regions = [
        af + Bm["conv"] + an,
        A["gate_c_prev"] + Bm["ln"] + Bm["regate"],
        C,
    ]
    last_regions = [C]
    schedule = [(regions, (t >= 1) & (t < n_tiles)),
                (fill_regions, t == 0),
                (drain_regions, t == n_tiles),
                (last_regions, t == n_tiles + 1)]
    for group, active in schedule:
        trips = jnp.where(active, 1, 0)
        for region in group:
            def body(_, carry, region=region):
                for item in region:
                    item()
                return carry
            lax.fori_loop(0, trips, body, 0)


def _layer(x, layer, norm_w, w_in, q_norm_w, k_norm_w, sinks, dw_w, dw_b, ln_w, ln_b, w_out, rel_bias):
    B, S, D = x.shape
    ts = SEQ_TILE
    assert D == D_MODEL and S % ts == 0 and ts % BLOCK == 0
    n_tiles = B * S // ts

    idx = np.arange(256) // HEAD_DIM
    bd = jnp.asarray((idx[:, None] == idx[None, :]).astype(np.float32), dtype=_BF16)
    qw = (jnp.tile(q_norm_w[layer].astype(_F32), N_HEADS) * (HEAD_DIM ** -0.5 * LOG2E)).reshape(1, D_ATTN)
    kw = jnp.tile(k_norm_w[layer].astype(_F32), N_KV_HEADS).reshape(1, 128)
    xt = x.reshape(n_tiles, ts, D)

    const2 = lambda t: (0, 0)
    layer_row = lambda t: (layer, 0)
    in_specs = [
        pl.BlockSpec(memory_space=pltpu.SMEM),
        pl.BlockSpec(memory_space=pltpu.SMEM),
        pl.BlockSpec((None, ts, D), lambda t: (jnp.minimum(t + 1, n_tiles - 1), 0, 0)),
        pl.BlockSpec((None, ts, D), lambda t: (jnp.maximum(t - 2, 0), 0, 0)),
        pl.BlockSpec((1, D), layer_row),
        pl.BlockSpec(memory_space=pl.ANY),
        pl.BlockSpec((1, D_ATTN), const2),
        pl.BlockSpec((1, 128), const2),
        pl.BlockSpec((256, 256), const2),
        pl.BlockSpec((None, CONV_WIDTH, D_CONV), lambda t: (layer, 0, 0)),
        pl.BlockSpec((1, D_CONV), layer_row),
        pl.BlockSpec((1, D_CONV), layer_row),
        pl.BlockSpec((1, D_CONV), layer_row),
        pl.BlockSpec(memory_space=pl.ANY),
        pl.BlockSpec((BLOCK, 2 * BLOCK), const2),
    ]
    scratch = [
        pltpu.VMEM((D, D_IN), _BF16),
        pltpu.VMEM((D, D), _BF16),
        pltpu.VMEM((W_SLOTS, W_ROWS, D_IN), _F32),
        pltpu.SemaphoreType.DMA((W_SLOTS,)),
        pltpu.VMEM((CONV_WIDTH * BF16_SUBLANES, D_CONV), _BF16),
        pltpu.VMEM((ts, D), _BF16),
        pltpu.VMEM((ts, UA0 - Q0), _F32),
        pltpu.VMEM((ts, ZC0 - UA0), _F32),
        pltpu.VMEM((ts, D_IN - ZC0), _F32),
        pltpu.VMEM((ts, D_ATTN + 128), _F32),
        pltpu.VMEM((2, ts, D_ATTN), _BF16),
        pltpu.VMEM((2, 4, ts + KV_HALO, 128), _BF16),
        pltpu.VMEM((2, 4, ts + KV_HALO, 256), _BF16),
        pltpu.VMEM((2 * N_SLABS, U_ROWS, 128), _F32),
        pltpu.VMEM((N_SLABS, N_UTILES * BF16_SUBLANES, 128), _BF16),
        pltpu.VMEM((2, ts, D_ATTN), _F32),
        pltpu.VMEM((2, ts, D_CONV), _F32),
        pltpu.VMEM((ts, D_CONV), _F32),
        pltpu.VMEM((N_SLABS, C_ROWS, 128), _F32),
        pltpu.VMEM((2, ts, D), _BF16),
        pltpu.VMEM((2, 4, 2 * BLOCK, 2 * BLOCK), _F32),
        pltpu.VMEM((ATTN_SLOTS, 2 * BLOCK, 2 * BLOCK), _F32),
        pltpu.VMEM((ATTN_SLOTS, 2, BLOCK, 128), _F32),
    ]
    out = pl.pallas_call(
        functools.partial(_layer_kernel, layer=layer, tiles_per_seq=S // ts, n_tiles=n_tiles),
        grid=(n_tiles + 2,),
        in_specs=in_specs,
        out_specs=pl.BlockSpec((None, ts, D), lambda t: (jnp.maximum(t - 2, 0), 0, 0)),
        out_shape=jax.ShapeDtypeStruct((n_tiles, ts, D), x.dtype),
        scratch_shapes=scratch,
        compiler_params=pltpu.CompilerParams(
            dimension_semantics=("arbitrary",),
            vmem_limit_bytes=VMEM_LIMIT_BYTES),
        name="hybrid_layer",
    )(sinks.astype(_F32), rel_bias.astype(_F32), xt, xt,
      norm_w.astype(_F32), w_in.astype(_F32), qw, kw, bd,
      dw_w.astype(_F32), dw_b.astype(_F32), ln_w.astype(_F32),
      ln_b.astype(_F32), w_out.astype(_F32), jnp.asarray(_banded_buckets()))
    return out.reshape(B, S, D)


def kernel(x, norm_w, w_in, q_norm_w, k_norm_w, sinks, dw_w, dw_b, ln_w, ln_b, w_out, rel_bias):
    for layer in range(norm_w.shape[0]):
        x = _layer(x, layer, norm_w, w_in, q_norm_w, k_norm_w, sinks, dw_w, dw_b, ln_w, ln_b, w_out, rel_bias)
    return x
```

```python
import functools
import math

import numpy as np
import jax
import jax.numpy as jnp
from jax import lax
from jax.experimental import pallas as pl
from jax.experimental.pallas import tpu as pltpu

D_MODEL = 1024
D_ATTN = 512
D_CONV = 512
HEAD_DIM = 64
N_HEADS = 8
N_KV_HEADS = 2
BLOCK = 128
NUM_BUCKETS = 32
MAX_DISTANCE = 128
CONV_WIDTH = 31
D_IN = 2816
EPS = 1e-6
LN_EPS = 1e-5
NEG_INF = -1e30
LOG2E = math.log2(math.e)

SEQ_TILE = 512
KV_HALO = BLOCK
ROW_CHUNK = 64
SUBLANES = 8
N_SLABS = D_CONV // 128
BF16_SUBLANES = 16
N_VACC = 4
N_TBLK = BF16_SUBLANES
T_BLK = SEQ_TILE // N_TBLK
U_PITCH = T_BLK + 1
U_ROWS = ((N_TBLK + 1) * U_PITCH + 7) // 8 * 8
N_UTILES = CONV_WIDTH - 1 + T_BLK
C_PITCH = T_BLK + 8
C_ROWS = N_TBLK * C_PITCH
N_CHUNK = 256
W_ROWS = 128
W_SLOTS = 4
ATTN_SLOTS = 8
VMEM_LIMIT_BYTES = 56 * 1024 * 1024

Q0, K0, V0, ZA0, UA0, UG0, ZC0 = 0, 512, 640, 768, 1280, 1792, 2304

_F32 = jnp.float32
_BF16 = jnp.bfloat16


def _t5_bucket_table():
    qi = np.arange(BLOCK)[:, None]
    sj = np.arange(2 * BLOCK)[None, :]
    dist = qi + BLOCK - sj
    n = np.maximum(dist, 0)
    max_exact = NUM_BUCKETS // 2
    nf = np.maximum(n, 1).astype(np.float32)
    large = max_exact + (np.log(nf / max_exact) / math.log(MAX_DISTANCE / max_exact)
                         * (NUM_BUCKETS - max_exact)).astype(np.int32)
    large = np.minimum(large, NUM_BUCKETS - 1)
    bucket = np.where(n < max_exact, n, large).astype(np.int32)
    band = (dist >= 0) & (dist < BLOCK)
    return bucket, band


def _banded_buckets():
    bucket, band = _t5_bucket_table()
    return np.where(band, bucket, -1).astype(np.int32)


def _sigmoid(z):
    return 0.5 * jnp.tanh(0.5 * z) + 0.5


def _silu(z):
    h = 0.5 * z
    return h + h * jnp.tanh(h)


def _bias_table_items(relb_ref, bucket_ref, bias_s):
    def rows8(r0):
        def run():
            cur_keys = lax.broadcasted_iota(jnp.int32, (SUBLANES, 2 * BLOCK), 1) >= BLOCK
            bk = bucket_ref[r0:r0 + SUBLANES, :]
            accs = [jnp.full((SUBLANES, 2 * BLOCK), NEG_INF, _F32) for _ in range(N_HEADS)]
            for b in range(NUM_BUCKETS):
                hit = bk == b
                for h in range(N_HEADS):
                    accs[h] = jnp.where(hit, relb_ref[b, h] * LOG2E, accs[h])
            for h in range(N_HEADS):
                g, rem = divmod(h, 4)
                j, e = divmod(rem, 2)
                rows = slice(BLOCK * j + r0, BLOCK * j + r0 + SUBLANES)
                bias_s[0, 2 * g + e, rows, :] = accs[h]
                bias_s[1, 2 * g + e, rows, :] = jnp.where(cur_keys, accs[h], NEG_INF)
        return run

    return [rows8(r0) for r0 in range(0, BLOCK, SUBLANES)]


class _RawProj:
    bounds = (Q0, UA0, ZC0, D_IN)

    def __init__(self, *refs):
        self.refs = refs

    def _find(self, c0, c1):
        for ref, lo, hi in zip(self.refs, self.bounds[:-1], self.bounds[1:]):
            if lo <= c0 and c1 <= hi:
                return ref, c0 - lo, c1 - lo
        raise ValueError((c0, c1))

    def load(self, rows, c0, c1):
        ref, a, b = self._find(c0, c1)
        return ref[rows, a:b]

    def store(self, c0, c1, val):
        ref, a, b = self._find(c0, c1)
        ref[:, a:b] = val


def _sink(sink_ref, layer, head):
    return sink_ref[layer, head]


def _row_chunks():
    return [slice(r, r + ROW_CHUNK) for r in range(0, SEQ_TILE, ROW_CHUNK)]


def _stream_a(sa, x_ref, x0_ref, nw_ref, win_ref, qw_ref, kw_ref, bd_ref,
              h_s, p_s, ssq_s, q_s, k_s, v_s, u_s, u2_s, ga_s, gc_s):
    ts = SEQ_TILE

    def rms(src_ref, r):
        def run():
            xc = src_ref[r:r + ROW_CHUNK, :]
            ss = jnp.sum(xc * xc, axis=-1, keepdims=True)
            rs = lax.rsqrt(ss * (1.0 / D_MODEL) + EPS)
            h_s[r:r + ROW_CHUNK, :] = ((xc * rs) * nw_ref[...]).astype(_BF16)
        return run

    def dot(c0, c1):
        def run():
            p_s.store(c0, c1, jnp.dot(h_s[...], win_ref[:, c0:c1], preferred_element_type=_F32))
        return run

    def q_sumsq(half):
        def run():
            cs = slice(256 * half, 256 * half + 256)
            q = p_s.load(slice(None), Q0 + cs.start, Q0 + cs.stop)
            ssq_s[:, cs] = jnp.dot((q * q).astype(_BF16), bd_ref[...], preferred_element_type=_F32)
        return run

    def q_norm(half):
        def run():
            cs = slice(256 * half, 256 * half + 256)
            for rr in _row_chunks():
                q = p_s.load(rr, Q0 + cs.start, Q0 + cs.stop)
                qn =(q * lax.rsqrt(ssq_s[rr, cs] * (1.0 / HEAD_DIM) + EPS)) * qw_ref[:, cs]
                q_s[sa, rr, cs] = qn.astype(_BF16)
        return run

    def k_sumsq():
        k = p_s.load(slice(None), K0, K0 + 128)
        ssq_s[:, D_ATTN:D_ATTN + 128] = jnp.dot((k * k).astype(_BF16), bd_ref[0:128, 0:128],
                                                preferred_element_type=_F32)

    def kv_store():
        lo = lax.broadcasted_iota(jnp.int32, (ROW_CHUNK, 128), 1) < HEAD_DIM
        zero = jnp.zeros((ROW_CHUNK, 128), _F32)
        one = jnp.ones((ROW_CHUNK, 128), _F32)
        ones_lo = jnp.where(lo, one, zero).astype(_BF16)
        ones_hi = jnp.where(lo, zero, one).astype(_BF16)
        for rr in _row_chunks():
            cur = slice(KV_HALO + rr.start, KV_HALO + rr.stop)
            k = p_s.load(rr, K0, K0 + 128)
            v = p_s.load(rr, V0, V0 + 128)
            kn = (k * lax.rsqrt(ssq_s[rr, D_ATTN:D_ATTN + 128] * (1.0 / HEAD_DIM) + EPS)) * kw_ref[...]
            kn_sw = pltpu.roll(kn, 64, axis=1)
            v_sw = pltpu.roll(v, 64, axis=1)
            k_s[sa, 0, cur, :] = jnp.where(lo, kn, zero).astype(_BF16)
            k_s[sa, 1, cur, :] = jnp.where(lo, zero, kn_sw).astype(_BF16)
            k_s[sa, 2, cur, :] = jnp.where(lo, kn_sw, zero).astype(_BF16)
            k_s[sa, 3, cur, :] = jnp.where(lo, zero, kn).astype(_BF16)
            v_s[sa, 0, cur, 0:128] = jnp.where(lo, v, zero).astype(_BF16)
            v_s[sa, 1, cur, 0:128] = jnp.where(lo, zero, v_sw).astype(_BF16)
            v_s[sa, 2, cur, 0:128] = jnp.where(lo, v_sw, zero).astype(_BF16)
            v_s[sa, 3, cur, 0:128] = jnp.where(lo, zero, v).astype(_BF16)
            v_s[sa, 0, cur, 128:256] = ones_lo
            v_s[sa, 1, cur, 128:256] = ones_hi
            v_s[sa, 2, cur, 128:256] = ones_lo
            v_s[sa, 3, cur, 128:256] = ones_hi

    def gate(c0, dst, half, slot):
        def run():
            cs = slice(256 * half, 256 * half + 256)
            for rr in _row_chunks():
                z = p_s.load(rr, c0 + 256 * half, c0 + 256 * half + 256)
                dst[slot, rr, cs] = _silu(z)
        return run

    def glu(half):
        def run():
            for blk in range(N_TBLK):
                tr = slice(T_BLK * blk, T_BLK * (blk + 1))
                a = p_s.load(tr, UA0 + 256 * half, UA0 + 256 * half + 256)
                g = p_s.load(tr, UG0 + 256 * half, UG0 + 256 * half + 256)
                u = a * _sigmoid(g)
                p0 = U_PITCH * (blk + 1)
                for sl in range(2):
                    u_s[sa * N_SLABS + 2 * half + sl, p0:p0 + T_BLK, :] = u[:, 128 * sl:128 * sl + 128]
        return run

    def pack(cb):
        taps = CONV_WIDTH - 1

        def run():
            for k in range(N_UTILES):
                tau = k - taps
                base = U_PITCH + tau if tau >= 0 else T_BLK + tau
                halves = [u_s[sa * N_SLABS + cb,
                              pl.ds(base + h * SUBLANES * U_PITCH, SUBLANES, stride=U_PITCH), :]
                          for h in range(N_TBLK // SUBLANES)]
                u2_s[cb, BF16_SUBLANES * k:BF16_SUBLANES * (k + 1), :] = (
                    jnp.concatenate(halves, axis=0).astype(_BF16))
        return run

    return dict(
        pack=[pack(cb) for cb in range(N_SLABS)],
        rms_next=[rms(x_ref, r) for r in range(0, ts, ROW_CHUNK)],
        rms_first=[rms(x0_ref, r) for r in range(0, ts, ROW_CHUNK)],
        dot_qkvz=dot(Q0, UA0), dot_zc=dot(ZC0, D_IN),
        dot_ag=[dot(c0 + 256 * half, c0 + 256 * half + 256) for half in range(2) for c0 in (UA0, UG0)],
        q_sumsq=[q_sumsq(0), q_sumsq(1)], q_norm=[q_norm(0), q_norm(1)],
        k_sumsq=k_sumsq, kv_store=kv_store,
        gate_a=[gate(ZA0, ga_s, 0, sa), gate(ZA0, ga_s, 1, sa)],
        gate_c_prev=[gate(ZC0, gc_s, 0, 1 - sa), gate(ZC0, gc_s, 1, 1 - sa)],
        glu=[glu(0), glu(1)])


def _stream_b(sb, first_b, sink_of, wb_ref, dwb_ref, lnw_ref, lnb_ref,
              q_s, k_s, v_s, u2_s, ga_s, gc_s, c_s, yc_s, y_s, bias_s, o_s, dn_s):
    ts = SEQ_TILE
    taps = CONV_WIDTH - 1

    def conv(cb):
        def run():
            ls = slice(128 * cb, 128 * cb + 128)
            bv = jnp.broadcast_to(dwb_ref[:, ls], (BF16_SUBLANES, 128))
            wv = [wb_ref[BF16_SUBLANES * j:BF16_SUBLANES * (j + 1), ls].astype(_F32) for j in range(CONV_WIDTH)]
            for r0 in range(0, T_BLK, N_VACC):
                accs = [None] * N_VACC
                for k in range(r0, r0 + N_VACC + taps):
                    win = u2_s[cb, BF16_SUBLANES * k:BF16_SUBLANES * (k + 1), :].astype(_F32)
                    for n in range(N_VACC):
                        j = k - (r0 + n)
                        if 0 <= j < CONV_WIDTH:
                            term = win * wv[j]
                            accs[n] = term if accs[n] is None else accs[n] + term
                for n in range(N_VACC):
                    r = r0 + n
                    c_s[BF16_SUBLANES * r:BF16_SUBLANES * (r + 1), ls] = accs[n] + bv
        return run

    def ln(r):
        def run():
            y = c_s[r:r + ROW_CHUNK, :]
            mu = jnp.sum(y, axis=-1, keepdims=True) * (1.0 / D_CONV)
            d = y - mu
            var = jnp.sum(d * d, axis=-1, keepdims=True) * (1.0 / D_CONV)
            yn = (d * lax.rsqrt(var + LN_EPS)) * lnw_ref[...] + lnb_ref[...]
            act = _silu(yn)
            for n in range(ROW_CHUNK // SUBLANES):
                row = r + SUBLANES * n
                tau, blk0 = divmod(row, BF16_SUBLANES)
                for cb in range(N_SLABS):
                    yc_s[cb, pl.ds(C_PITCH * blk0 + tau, SUBLANES, stride=C_PITCH), :] = (
                        act[SUBLANES * n:SUBLANES * (n + 1), 128 * cb:128 * cb + 128])
        return run

    def regate(blk):
        def run():
            tr = slice(T_BLK * blk, T_BLK * (blk + 1))
            for cb in range(N_SLABS):
                ls = slice(128 * cb, 128 * cb + 128)
                val = yc_s[cb, C_PITCH * blk:C_PITCH * blk + T_BLK, :] * gc_s[sb, tr, ls]
                y_s[sb, tr, D_ATTN + 128 * cb:D_ATTN + 128 * cb + 128] = val.astype(_BF16)
        return run

    def attn_front(b, g):
        def run():
            slot = (2 * b + g) % ATTN_SLOTS
            rows = slice(BLOCK * b, BLOCK * b + BLOCK)
            krows = slice(BLOCK * b, BLOCK * b + 2 * BLOCK)
            tbl = first_b if b == 0 else 0
            lo_q = lax.broadcasted_iota(jnp.int32, (BLOCK, 128), 1) < HEAD_DIM
            c0 = 256 * g
            qg = jnp.concatenate([q_s[sb, rows, c0:c0 + 128], q_s[sb, rows, c0 + 128:c0 + 256]], axis=0)
            kk = jnp.concatenate([k_s[sb, 2 * g, krows, :], k_s[sb, 2 * g + 1, krows, :]], axis=0)
            s = lax.dot_general(qg, kk, (((1,), (1,)), ((), ())),
                                preferred_element_type=_F32)
            ps = [[None, None], [None, None]]
            ex = [[None, None], [None, None]]
            for e in range(2):
                se = s[:, 2 * BLOCK * e:2 * BLOCK * (e + 1)] + bias_s[tbl, 2 * g + e]
                for j in range(2):
                    sink = sink_of(4 * g + 2 * j + e) * LOG2E
                    sj = se[BLOCK * j:BLOCK * j + BLOCK, :]
                    m = jnp.maximum(jnp.max(sj, axis=-1, keepdims=True), sink)
                    ps[j][e] = jnp.exp2(sj - m).astype(_BF16)
                    ex[j][e] = jnp.exp2(sink - m)
            p = jnp.concatenate([jnp.concatenate(ps[0], axis=1), jnp.concatenate(ps[1], axis=1)], axis=0)
            vv = jnp.concatenate([v_s[sb, 2 * g, krows, :], v_s[sb, 2 * g + 1, krows, :]], axis=0)
            o_s[slot] = jnp.dot(p, vv, preferred_element_type=_F32)
            for j in range(2):
                dn_s[slot, j] = jnp.where(lo_q, ex[j][0], ex[j][1])
        return run

    def attn_norm(b, g):
        def run():
            slot = (2 * b + g) % ATTN_SLOTS
            rows = slice(BLOCK * b, BLOCK * b + BLOCK)
            for j in range(2):
                oj = o_s[slot, BLOCK * j:BLOCK * j + BLOCK, :]
                den = oj[:, 128:256] + dn_s[slot, j]
                cs = slice(256 * g + 128 * j, 256 * g + 128 * j + 128)
                y_s[sb, rows, cs] = ((oj[:, 0:128] / den) * ga_s[sb, rows, cs]).astype(_BF16)
        return run

    pairs = [(b, g) for b in range(ts // BLOCK) for g in range(N_KV_HEADS)]
    return dict(attn_front=[attn_front(b, g) for b, g in pairs],
                attn_norm=[attn_norm(b, g) for b, g in pairs],
                conv=[conv(cb) for cb in range(N_SLABS)],
                ln=[ln(r) for r in range(0, ts, ROW_CHUNK)],
                regate=[regate(blk) for blk in range(N_TBLK)])


def _stream_c(sy, xres_ref, wout_ref, o_ref, y_s):
    def outproj(n):
        def run():
            o_ref[:, n:n + N_CHUNK] = xres_ref[:, n:n + N_CHUNK] + jnp.dot(
                y_s[sy], wout_ref[:, n:n + N_CHUNK], preferred_element_type=_F32)
        return run

    return [outproj(n) for n in range(0, D_MODEL, N_CHUNK)]


def _load_weight_bf16(w_hbm, w_s, stage, sem, side_work=()):
    rows, cols = w_s.shape
    n_chunks = rows // W_ROWS
    n_slots = stage.shape[0]
    per_chunk = -(-len(side_work) // n_chunks)

    def copy(c):
        slot = c % n_slots
        return pltpu.make_async_copy(w_hbm.at[pl.ds(c * W_ROWS, W_ROWS), :],
                                     stage.at[slot, :, pl.ds(0, cols)], sem.at[slot])

    for c in range(min(n_slots - 1, n_chunks)):
        copy(c).start()
    for c in range(n_chunks):
        nxt = c + n_slots - 1
        if nxt < n_chunks:
            copy(nxt).start()
        for item in side_work[c * per_chunk:(c + 1) * per_chunk]:
            item()
        copy(c).wait()
        w_s[c * W_ROWS:(c + 1) * W_ROWS, :] = stage[c % n_slots, :, 0:cols].astype(_BF16)


class _BackgroundWeightLoad:
    def __init__(self, w_hbm, w_s, stage, sem):
        rows, cols = w_s.shape
        per_slot = stage.shape[2] // cols
        self.n_chunks = rows // W_ROWS
        assert self.n_chunks <= per_slot * stage.shape[0]
        self.w_s, self.stage, self.cols, self.per_slot = w_s, stage, cols, per_slot
        self.copies = [
            pltpu.make_async_copy(w_hbm.at[pl.ds(c * W_ROWS, W_ROWS), :],
                                  stage.at[c // per_slot, :, pl.ds((c % per_slot) * cols, cols)], sem.at[c])
            for c in range(self.n_chunks)]

    def start(self):
        for copy in self.copies:
            copy.start()

    def finish(self):
        for c, copy in enumerate(self.copies):
            copy.wait()
            c0 = (c % self.per_slot) * self.cols
            self.w_s[c * W_ROWS:(c + 1) * W_ROWS, :] = (
                self.stage[c // self.per_slot, :, c0:c0 + self.cols].astype(_BF16))


def _layer_kernel(sink_ref, relb_ref, x_ref, xres_ref, nw_ref, win_hbm, qw_ref, kw_ref, bd_ref, dww_ref,
                  dwb_ref, lnw_ref, lnb_ref, wout_hbm, bucket_ref,
                  o_ref,
                  win_ref, wout_ref, wstage_s, wsem, wout_sem, wb_ref,
                  h_s, p1_s, p2_s, p3_s, ssq_s, q_s, k_s, v_s, u_s, u2_s, ga_s, gc_s, c_s, yc_s, y_s,
                  bias_s, o_s, dn_s, *, layer, tiles_per_seq, n_tiles):
    p_s = _RawProj(p1_s, p2_s, p3_s)
    ts = SEQ_TILE
    t = pl.program_id(0)
    sa = lax.rem(t, 2)
    sb = 1 - sa
    first_a = lax.rem(t, tiles_per_seq) == 0
    first_b = jnp.where(lax.rem(t + tiles_per_seq - 1, tiles_per_seq) == 0, 1, 0)

    wout_load = _BackgroundWeightLoad(wout_hbm.at[layer], wout_ref, wstage_s, wout_sem)

    @pl.when(t == 0)
    def _():
        _load_weight_bf16(win_hbm.at[layer], win_ref, wstage_s, wsem,
                          side_work=_bias_table_items(relb_ref, bucket_ref, bias_s))
        wout_load.start()
        for j in range(CONV_WIDTH):
            wb_ref[BF16_SUBLANES * j:BF16_SUBLANES * (j + 1), :] = jnp.broadcast_to(
                dww_ref[j:j + 1, :], (BF16_SUBLANES, D_CONV)).astype(_BF16)
        y_s[1] = jnp.zeros(y_s.shape[1:], _BF16)

    @pl.when(first_a)
    def _():
        k_s[sa, :, 0:KV_HALO, :] = jnp.zeros((4, KV_HALO, 128), _BF16)
        v_s[sa, :, 0:KV_HALO, :] = jnp.zeros((4, KV_HALO, 256), _BF16)
        for cb in range(N_SLABS):
            u_s[sa * N_SLABS + cb, 0:T_BLK, :] = jnp.zeros((T_BLK, 128), _F32)

    @pl.when(jnp.logical_not(first_a))
    def _():
        k_s[sa, :, 0:KV_HALO, :] = k_s[sb, :, ts:ts + KV_HALO, :]
        v_s[sa, :, 0:KV_HALO, :] = v_s[sb, :, ts:ts + KV_HALO, :]
        for cb in range(N_SLABS):
            u_s[sa * N_SLABS + cb, 0:T_BLK, :] = u_s[sb * N_SLABS + cb,
                                                     N_TBLK * U_PITCH:N_TBLK * U_PITCH + T_BLK, :]

    A = _stream_a(sa, x_ref, xres_ref, nw_ref, win_ref, qw_ref, kw_ref, bd_ref,
                  h_s, p_s, ssq_s, q_s, k_s, v_s, u_s, u2_s, ga_s, gc_s)
    Bm = _stream_b(sb, first_b, functools.partial(_sink, sink_ref, layer), wb_ref, dwb_ref, lnw_ref, lnb_ref,
                   q_s, k_s, v_s, u2_s, ga_s, gc_s, c_s, yc_s, y_s, bias_s, o_s, dn_s)
    C = _stream_c(sa, xres_ref, wout_ref, o_ref, y_s)
    af, an = Bm["attn_front"], Bm["attn_norm"]

    regions = [
        af + an,
        [item for pair in zip(C, Bm["conv"]) for item in pair],
        A["gate_c_prev"] + Bm["ln"] + Bm["regate"] + [A["dot_qkvz"]]
        + A["q_sumsq"] + [A["k_sumsq"]] + A["gate_a"] + A["q_norm"]
        + [A["kv_store"]] + A["dot_ag"] + A["glu"] + A["pack"] + [A["dot_zc"]] + A["rms_next"],
    ]
    fill_regions = [
        A["rms_first"],
        [A["dot_qkvz"], A["dot_zc"]],
        A["q_sumsq"] + [A["k_sumsq"]] + A["gate_a"] + A["q_norm"]
        + [A["kv_store"]] + A["dot_ag"],
        A["glu"] + A["pack"] + A["rms_next"],
    ]
    drain_regions = [
        af + Bm["conv"] + an,
        A["gate_c_prev"] + Bm["ln"] + Bm["regate"],
        C,
    ]
    last_regions = [C]
    schedule = [(regions, (t >= 1) & (t < n_tiles)),
                (fill_regions, t == 0),
                (drain_regions, t == n_tiles),
                (last_regions, t == n_tiles + 1)]
    for group, active in schedule:
        trips = jnp.where(active, 1, 0)
        for region in group:
            def body(_, carry, region=region):
                for item in region:
                    item()
                return carry
            lax.fori_loop(0, trips, body, 0)

    @pl.when(t == 0)
    def _():
        wout_load.finish()


def _layer(x, layer, norm_w, w_in, q_norm_w, k_norm_w, sinks, dw_w, dw_b, ln_w, ln_b, w_out, rel_bias):
    B, S, D = x.shape
    ts = SEQ_TILE
    assert D == D_MODEL and S % ts == 0 and ts % BLOCK == 0
    n_tiles = B * S // ts

    idx = np.arange(256) // HEAD_DIM
    bd = jnp.asarray((idx[:, None] == idx[None, :]).astype(np.float32), dtype=_BF16)
    qw = (jnp.tile(q_norm_w[layer].astype(_F32), N_HEADS) * (HEAD_DIM ** -0.5 * LOG2E)).reshape(1, D_ATTN)
    kw = jnp.tile(k_norm_w[layer].astype(_F32), N_KV_HEADS).reshape(1, 128)
    xt = x.reshape(n_tiles, ts, D)

    const2 = lambda t: (0, 0)
    layer_row = lambda t: (layer, 0)
    in_specs = [
        pl.BlockSpec(memory_space=pltpu.SMEM),
        pl.BlockSpec(memory_space=pltpu.SMEM),
        pl.BlockSpec((None, ts, D), lambda t: (jnp.minimum(t + 1, n_tiles - 1), 0, 0)),
        pl.BlockSpec((None, ts, D), lambda t: (jnp.maximum(t - 2, 0), 0, 0)),
        pl.BlockSpec((1, D), layer_row),
        pl.BlockSpec(memory_space=pl.ANY),
        pl.BlockSpec((1, D_ATTN), const2),
        pl.BlockSpec((1, 128), const2),
        pl.BlockSpec((256, 256), const2),
        pl.BlockSpec((None, CONV_WIDTH, D_CONV), lambda t: (layer, 0, 0)),
        pl.BlockSpec((1, D_CONV), layer_row),
        pl.BlockSpec((1, D_CONV), layer_row),
        pl.BlockSpec((1, D_CONV), layer_row),
        pl.BlockSpec(memory_space=pl.ANY),
        pl.BlockSpec((BLOCK, 2 * BLOCK), const2),
    ]
    scratch = [
        pltpu.VMEM((D, D_IN), _BF16),
        pltpu.VMEM((D, D), _BF16),
        pltpu.VMEM((W_SLOTS, W_ROWS, D_IN), _F32),
        pltpu.SemaphoreType.DMA((W_SLOTS,)),
        pltpu.SemaphoreType.DMA((D // W_ROWS,)),
        pltpu.VMEM((CONV_WIDTH * BF16_SUBLANES, D_CONV), _BF16),
        pltpu.VMEM((ts, D), _BF16),
        pltpu.VMEM((ts, UA0 - Q0), _F32),
        pltpu.VMEM((ts, ZC0 - UA0), _F32),
        pltpu.VMEM((ts, D_IN - ZC0), _F32),
        pltpu.VMEM((ts, D_ATTN + 128), _F32),
        pltpu.VMEM((2, ts, D_ATTN), _BF16),
        pltpu.VMEM((2, 4, ts + KV_HALO, 128), _BF16),
        pltpu.VMEM((2, 4, ts + KV_HALO, 256), _BF16),
        pltpu.VMEM((2 * N_SLABS, U_ROWS, 128), _F32),
        pltpu.VMEM((N_SLABS, N_UTILES * BF16_SUBLANES, 128), _BF16),
        pltpu.VMEM((2, ts, D_ATTN), _F32),
        pltpu.VMEM((2, ts, D_CONV), _F32),
        pltpu.VMEM((ts, D_CONV), _F32),
        pltpu.VMEM((N_SLABS, C_ROWS, 128), _F32),
        pltpu.VMEM((2, ts, D), _BF16),
        pltpu.VMEM((2, 4, 2 * BLOCK, 2 * BLOCK), _F32),
        pltpu.VMEM((ATTN_SLOTS, 2 * BLOCK, 2 * BLOCK), _F32),
        pltpu.VMEM((ATTN_SLOTS, 2, BLOCK, 128), _F32),
    ]
    out = pl.pallas_call(
        functools.partial(_layer_kernel, layer=layer, tiles_per_seq=S // ts, n_tiles=n_tiles),
        grid=(n_tiles + 2,),
        in_specs=in_specs,
        out_specs=pl.BlockSpec((None, ts, D), lambda t: (jnp.maximum(t - 2, 0), 0, 0)),
        out_shape=jax.ShapeDtypeStruct((n_tiles, ts, D), x.dtype),
        scratch_shapes=scratch,
        compiler_params=pltpu.CompilerParams(
            dimension_semantics=("arbitrary",),
            vmem_limit_bytes=VMEM_LIMIT_BYTES),
        name="hybrid_layer",
    )(sinks.astype(_F32), rel_bias.astype(_F32), xt, xt,
      norm_w.astype(_F32), w_in.astype(_F32), qw, kw, bd,
      dw_w.astype(_F32), dw_b.astype(_F32), ln_w.astype(_F32),
      ln_b.astype(_F32), w_out.astype(_F32), jnp.asarray(_banded_buckets()))
    return out.reshape(B, S, D)


def kernel(x, norm_w, w_in, q_norm_w, k_norm_w, sinks, dw_w, dw_b, ln_w, ln_b, w_out, rel_bias):
    for layer in range(norm_w.shape[0]):
        x = _layer(x, layer, norm_w, w_in, q_norm_w, k_norm_w, sinks, dw_w, dw_b, ln_w, ln_b, w_out, rel_bias)
    return x
```

```python
import functools
import math

import numpy as np
import jax
import jax.numpy as jnp
from jax import lax
from jax.experimental import pallas as pl
from jax.experimental.pallas import tpu as pltpu

D_MODEL = 1024
D_ATTN = 512
D_CONV = 512
HEAD_DIM = 64
N_HEADS = 8
N_KV_HEADS = 2
BLOCK = 128
NUM_BUCKETS = 32
MAX_DISTANCE = 128
CONV_WIDTH = 31
D_IN = 2816
EPS = 1e-6
LN_EPS = 1e-5
NEG_INF = -1e30
LOG2E = math.log2(math.e)

SEQ_TILE = 512
KV_HALO = BLOCK
ROW_CHUNK = 64
SUBLANES = 8
N_SLABS = D_CONV // 128
BF16_SUBLANES = 16
N_VACC = 4
N_TBLK = BF16_SUBLANES
T_BLK = SEQ_TILE // N_TBLK
U_PITCH = T_BLK + 1
U_ROWS = ((N_TBLK + 1) * U_PITCH + 7) // 8 * 8
N_UTILES = CONV_WIDTH - 1 + T_BLK
C_PITCH = T_BLK + 8
C_ROWS = N_TBLK * C_PITCH
N_CHUNK = 256
W_ROWS = 128
W_SLOTS = 4
ATTN_SLOTS = 8
VMEM_LIMIT_BYTES = 56 * 1024 * 1024

Q0, K0, V0, ZA0, UA0, UG0, ZC0 = 0, 512, 640, 768, 1280, 1792, 2304

_F32 = jnp.float32
_BF16 = jnp.bfloat16


def _t5_bucket_table():
    qi = np.arange(BLOCK)[:, None]
    sj = np.arange(2 * BLOCK)[None, :]
    dist = qi + BLOCK - sj
    n = np.maximum(dist, 0)
    max_exact = NUM_BUCKETS // 2
    nf = np.maximum(n, 1).astype(np.float32)
    large = max_exact + (np.log(nf / max_exact) / math.log(MAX_DISTANCE / max_exact)
                         * (NUM_BUCKETS - max_exact)).astype(np.int32)
    large = np.minimum(large, NUM_BUCKETS - 1)
    bucket = np.where(n < max_exact, n, large).astype(np.int32)
    band = (dist >= 0) & (dist < BLOCK)
    return bucket, band


def _banded_buckets():
    bucket, band = _t5_bucket_table()
    return np.where(band, bucket, -1).astype(np.int32)


def _sigmoid(z):
    return 0.5 * jnp.tanh(0.5 * z) + 0.5


def _silu(z):
    h = 0.5 * z
    return h + h * jnp.tanh(h)


def _bias_table_items(relb_ref, bucket_ref, bias_s):
    def rows8(r0):
        def run():
            cur_keys = lax.broadcasted_iota(jnp.int32, (SUBLANES, 2 * BLOCK), 1) >= BLOCK
            bk = bucket_ref[r0:r0 + SUBLANES, :]
            accs = [jnp.full((SUBLANES, 2 * BLOCK), NEG_INF, _F32) for _ in range(N_HEADS)]
            for b in range(NUM_BUCKETS):
                hit = bk == b
                for h in range(N_HEADS):
                    accs[h] = jnp.where(hit, relb_ref[b, h] * LOG2E, accs[h])
            for h in range(N_HEADS):
                g, rem = divmod(h, 4)
                j, e = divmod(rem, 2)
                rows = slice(BLOCK * j + r0, BLOCK * j + r0 + SUBLANES)
                bias_s[0, 2 * g + e, rows, :] = accs[h]
                bias_s[1, 2 * g + e, rows, :] = jnp.where(cur_keys, accs[h], NEG_INF)
        return run

    return [rows8(r0) for r0 in range(0, BLOCK, SUBLANES)]


class _RawProj:
    bounds = (Q0, UA0, ZC0, D_IN)

    def __init__(self, *refs):
        self.refs = refs

    def _find(self, c0, c1):
        for ref, lo, hi in zip(self.refs, self.bounds[:-1], self.bounds[1:]):
            if lo <= c0 and c1 <= hi:
                return ref, c0 - lo, c1 - lo
        raise ValueError((c0, c1))

    def load(self, rows, c0, c1):
        ref, a, b = self._find(c0, c1)
        return ref[rows, a:b]

    def store(self, c0, c1, val):
        ref, a, b = self._find(c0, c1)
        ref[:, a:b] = val


def _sink(sink_ref, layer, head):
    return sink_ref[layer, head]


def _row_chunks():
    return [slice(r, r + ROW_CHUNK) for r in range(0, SEQ_TILE, ROW_CHUNK)]


def _stream_a(sa, x_ref, x0_ref, nw_ref, win_ref, qw_ref, kw_ref, bd_ref,
              h_s, p_s, ssq_s, q_s, k_s, v_s, u_s, u2_s, ga_s, gc_s):
    ts = SEQ_TILE

    def rms(src_ref, r):
        def run():
            xc = src_ref[r:r + ROW_CHUNK, :]
            ss = jnp.sum(xc * xc, axis=-1, keepdims=True)
            rs = lax.rsqrt(ss * (1.0 / D_MODEL) + EPS)
            h_s[r:r + ROW_CHUNK, :] = ((xc * rs) * nw_ref[...]).astype(_BF16)
        return run

    def dot(c0, c1):
        def run():
            p_s.store(c0, c1, jnp.dot(h_s[...], win_ref[:, c0:c1], preferred_element_type=_F32))
        return run

    def q_sumsq(half):
        def run():
            cs = slice(256 * half, 256 * half + 256)
            q = p_s.load(slice(None), Q0 + cs.start, Q0 + cs.stop)
            ssq_s[:, cs] = jnp.dot((q * q).astype(_BF16), bd_ref[...], preferred_element_type=_F32)
        return run

    def q_norm(half):
        def run():
            cs = slice(256 * half, 256 * half + 256)
            for rr in _row_chunks():
                q = p_s.load(rr, Q0 + cs.start, Q0 + cs.stop)
                qn =(q * lax.rsqrt(ssq_s[rr, cs] * (1.0 / HEAD_DIM) + EPS)) * qw_ref[:, cs]
                q_s[sa, rr, cs] = qn.astype(_BF16)
        return run

    def k_sumsq():
        k = p_s.load(slice(None), K0, K0 + 128)
        ssq_s[:, D_ATTN:D_ATTN + 128] = jnp.dot((k * k).astype(_BF16), bd_ref[0:128, 0:128],
                                                preferred_element_type=_F32)

    def kv_store():
        lo = lax.broadcasted_iota(jnp.int32, (ROW_CHUNK, 128), 1) < HEAD_DIM
        zero = jnp.zeros((ROW_CHUNK, 128), _F32)
        one = jnp.ones((ROW_CHUNK, 128), _F32)
        ones_lo = jnp.where(lo, one, zero).astype(_BF16)
        ones_hi = jnp.where(lo, zero, one).astype(_BF16)
        for rr in _row_chunks():
            cur = slice(KV_HALO + rr.start, KV_HALO + rr.stop)
            k = p_s.load(rr, K0, K0 + 128)
            v = p_s.load(rr, V0, V0 + 128)
            kn = (k * lax.rsqrt(ssq_s[rr, D_ATTN:D_ATTN + 128] * (1.0 / HEAD_DIM) + EPS)) * kw_ref[...]
            kn_sw = pltpu.roll(kn, 64, axis=1)
            v_sw = pltpu.roll(v, 64, axis=1)
            k_s[sa, 0, cur, :] = jnp.where(lo, kn, zero).astype(_BF16)
            k_s[sa, 1, cur, :] = jnp.where(lo, zero, kn_sw).astype(_BF16)
            k_s[sa, 2, cur, :] = jnp.where(lo, kn_sw, zero).astype(_BF16)
            k_s[sa, 3, cur, :] = jnp.where(lo, zero, kn).astype(_BF16)
            v_s[sa, 0, cur, 0:128] = jnp.where(lo, v, zero).astype(_BF16)
            v_s[sa, 1, cur, 0:128] = jnp.where(lo, zero, v_sw).astype(_BF16)
            v_s[sa, 2, cur, 0:128] = jnp.where(lo, v_sw, zero).astype(_BF16)
            v_s[sa, 3, cur, 0:128] = jnp.where(lo, zero, v).astype(_BF16)
            v_s[sa, 0, cur, 128:256] = ones_lo
            v_s[sa, 1, cur, 128:256] = ones_hi
            v_s[sa, 2, cur, 128:256] = ones_lo
            v_s[sa, 3, cur, 128:256] = ones_hi

    def gate(c0, dst, half, slot):
        def run():
            cs = slice(256 * half, 256 * half + 256)
            for rr in _row_chunks():
                z = p_s.load(rr, c0 + 256 * half, c0 + 256 * half + 256)
                dst[slot, rr, cs] = _silu(z)
        return run

    def glu(half):
        def run():
            for blk in range(N_TBLK):
                tr = slice(T_BLK * blk, T_BLK * (blk + 1))
                a = p_s.load(tr, UA0 + 256 * half, UA0 + 256 * half + 256)
                g = p_s.load(tr, UG0 + 256 * half, UG0 + 256 * half + 256)
                u = a * _sigmoid(g)
                p0 = U_PITCH * (blk + 1)
                for sl in range(2):
                    u_s[sa * N_SLABS + 2 * half + sl, p0:p0 + T_BLK, :] = u[:, 128 * sl:128 * sl + 128]
        return run

    def pack(cb):
        taps = CONV_WIDTH - 1

        def run():
            for k in range(N_UTILES):
                tau = k - taps
                base = U_PITCH + tau if tau >= 0 else T_BLK + tau
                halves = [u_s[sa * N_SLABS + cb,
                              pl.ds(base + h * SUBLANES * U_PITCH, SUBLANES, stride=U_PITCH), :]
                          for h in range(N_TBLK // SUBLANES)]
                u2_s[cb, BF16_SUBLANES * k:BF16_SUBLANES * (k + 1), :] = (
                    jnp.concatenate(halves, axis=0).astype(_BF16))
        return run

    return dict(
        pack=[pack(cb) for cb in range(N_SLABS)],
        rms_next=[rms(x_ref, r) for r in range(0, ts, ROW_CHUNK)],
        rms_first=[rms(x0_ref, r) for r in range(0, ts, ROW_CHUNK)],
        dot_qkvz=dot(Q0, UA0), dot_zc=dot(ZC0, D_IN),
        dot_ag=[dot(c0 + 256 * half, c0 + 256 * half + 256) for half in range(2) for c0 in (UA0, UG0)],
        q_sumsq=[q_sumsq(0), q_sumsq(1)], q_norm=[q_norm(0), q_norm(1)],
        k_sumsq=k_sumsq, kv_store=kv_store,
        gate_a=[gate(ZA0, ga_s, 0, sa), gate(ZA0, ga_s, 1, sa)],
        gate_c_prev=[gate(ZC0, gc_s, 0, 1 - sa), gate(ZC0, gc_s, 1, 1 - sa)],
        glu=[glu(0), glu(1)])


def _stream_b(sb, first_b, sink_of, wb_ref, dwb_ref, lnw_ref, lnb_ref,
              q_s, k_s, v_s, u2_s, ga_s, gc_s, c_s, yc_s, y_s, bias_s, o_s, dn_s):
    ts = SEQ_TILE
    taps = CONV_WIDTH - 1

    def conv(cb):
        def run():
            ls = slice(128 * cb, 128 * cb + 128)
            bv = jnp.broadcast_to(dwb_ref[:, ls], (BF16_SUBLANES, 128))
            wv = [wb_ref[BF16_SUBLANES * j:BF16_SUBLANES * (j + 1), ls].astype(_F32) for j in range(CONV_WIDTH)]
            for r0 in range(0, T_BLK, N_VACC):
                accs = [None] * N_VACC
                for k in range(r0, r0 + N_VACC + taps):
                    win = u2_s[cb, BF16_SUBLANES * k:BF16_SUBLANES * (k + 1), :].astype(_F32)
                    for n in range(N_VACC):
                        j = k - (r0 + n)
                        if 0 <= j < CONV_WIDTH:
                            term = win * wv[j]
                            accs[n] = term if accs[n] is None else accs[n] + term
                for n in range(N_VACC):
                    r = r0 + n
                    c_s[BF16_SUBLANES * r:BF16_SUBLANES * (r + 1), ls] = accs[n] + bv
        return run

    def ln(r):
        def run():
            y = c_s[r:r + ROW_CHUNK, :]
            mu = jnp.sum(y, axis=-1, keepdims=True) * (1.0 / D_CONV)
            d = y - mu
            var = jnp.sum(d * d, axis=-1, keepdims=True) * (1.0 / D_CONV)
            yn = (d * lax.rsqrt(var + LN_EPS)) * lnw_ref[...] + lnb_ref[...]
            act = _silu(yn)
            for n in range(ROW_CHUNK // SUBLANES):
                row = r + SUBLANES * n
                tau, blk0 = divmod(row, BF16_SUBLANES)
                for cb in range(N_SLABS):
                    yc_s[cb, pl.ds(C_PITCH * blk0 + tau, SUBLANES, stride=C_PITCH), :] = (
                        act[SUBLANES * n:SUBLANES * (n + 1), 128 * cb:128 * cb + 128])
        return run

    def regate(blk):
        def run():
            tr = slice(T_BLK * blk, T_BLK * (blk + 1))
            for cb in range(N_SLABS):
                ls = slice(128 * cb, 128 * cb + 128)
                val = yc_s[cb, C_PITCH * blk:C_PITCH * blk + T_BLK, :] * gc_s[sb, tr, ls]
                y_s[sb, tr, D_ATTN + 128 * cb:D_ATTN + 128 * cb + 128] = val.astype(_BF16)
        return run

    def attn_front(b, g):
        def run():
            slot = (2 * b + g) % ATTN_SLOTS
            rows = slice(BLOCK * b, BLOCK * b + BLOCK)
            krows = slice(BLOCK * b, BLOCK * b + 2 * BLOCK)
            tbl = first_b if b == 0 else 0
            lo_q = lax.broadcasted_iota(jnp.int32, (BLOCK, 128), 1) < HEAD_DIM
            c0 = 256 * g
            qg = jnp.concatenate([q_s[sb, rows, c0:c0 + 128], q_s[sb, rows, c0 + 128:c0 + 256]], axis=0)
            kk = jnp.concatenate([k_s[sb, 2 * g, krows, :], k_s[sb, 2 * g + 1, krows, :]], axis=0)
            s = lax.dot_general(qg, kk, (((1,), (1,)), ((), ())),
                                preferred_element_type=_F32)
            ps = [[None, None], [None, None]]
            ex = [[None, None], [None, None]]
            for e in range(2):
                se = s[:, 2 * BLOCK * e:2 * BLOCK * (e + 1)] + bias_s[tbl, 2 * g + e]
                for j in range(2):
                    sink = sink_of(4 * g + 2 * j + e) * LOG2E
                    sj = se[BLOCK * j:BLOCK * j + BLOCK, :]
                    m = jnp.maximum(jnp.max(sj, axis=-1, keepdims=True), sink)
                    ps[j][e] = jnp.exp2(sj - m).astype(_BF16)
                    ex[j][e] = jnp.exp2(sink - m)
            p = jnp.concatenate([jnp.concatenate(ps[0], axis=1), jnp.concatenate(ps[1], axis=1)], axis=0)
            vv = jnp.concatenate([v_s[sb, 2 * g, krows, :], v_s[sb, 2 * g + 1, krows, :]], axis=0)
            o_s[slot] = jnp.dot(p, vv, preferred_element_type=_F32)
            for j in range(2):
                dn_s[slot, j] = jnp.where(lo_q, ex[j][0], ex[j][1])
        return run

    def attn_norm(b, g):
        def run():
            slot = (2 * b + g) % ATTN_SLOTS
            rows = slice(BLOCK * b, BLOCK * b + BLOCK)
            for j in range(2):
                oj = o_s[slot, BLOCK * j:BLOCK * j + BLOCK, :]
                den = oj[:, 128:256] + dn_s[slot, j]
                cs = slice(256 * g + 128 * j, 256 * g + 128 * j + 128)
                y_s[sb, rows, cs] = ((oj[:, 0:128] / den) * ga_s[sb, rows, cs]).astype(_BF16)
        return run

    pairs = [(b, g) for b in range(ts // BLOCK) for g in range(N_KV_HEADS)]
    return dict(attn_front=[attn_front(b, g) for b, g in pairs],
                attn_norm=[attn_norm(b, g) for b, g in pairs],
                conv=[conv(cb) for cb in range(N_SLABS)],
                ln=[ln(r) for r in range(0, ts, ROW_CHUNK)],
                regate=[regate(blk) for blk in range(N_TBLK)])


def _stream_c(sy, xres_ref, wout_ref, o_ref, y_s):
    def outproj(n):
        def run():
            o_ref[:, n:n + N_CHUNK] = xres_ref[:, n:n + N_CHUNK] + jnp.dot(
                y_s[sy], wout_ref[:, n:n + N_CHUNK], preferred_element_type=_F32)
        return run

    return [outproj(n) for n in range(0, D_MODEL, N_CHUNK)]


def _load_weight_bf16(w_hbm, w_s, stage, sem, side_work=()):
    rows, cols = w_s.shape
    n_chunks = rows // W_ROWS
    n_slots = stage.shape[0]
    per_chunk = -(-len(side_work) // n_chunks)

    def copy(c):
        slot = c % n_slots
        return pltpu.make_async_copy(w_hbm.at[pl.ds(c * W_ROWS, W_ROWS), :],
                                     stage.at[slot, :, pl.ds(0, cols)], sem.at[slot])

    for c in range(min(n_slots - 1, n_chunks)):
        copy(c).start()
    for c in range(n_chunks):
        nxt = c + n_slots - 1
        if nxt < n_chunks:
            copy(nxt).start()
        for item in side_work[c * per_chunk:(c + 1) * per_chunk]:
            item()
        copy(c).wait()
        w_s[c * W_ROWS:(c + 1) * W_ROWS, :] = stage[c % n_slots, :, 0:cols].astype(_BF16)


class _BackgroundWeightLoad:
    def __init__(self, w_hbm, w_s, stage, sem):
        rows, cols = w_s.shape
        per_slot = stage.shape[2] // cols
        self.n_chunks = rows // W_ROWS
        assert self.n_chunks <= per_slot * stage.shape[0]
        self.w_s, self.stage, self.cols, self.per_slot = w_s, stage, cols, per_slot
        self.copies = [
            pltpu.make_async_copy(w_hbm.at[pl.ds(c * W_ROWS, W_ROWS), :],
                                  stage.at[c // per_slot, :, pl.ds((c % per_slot) * cols, cols)], sem.at[c])
            for c in range(self.n_chunks)]

    def start(self):
        for copy in self.copies:
            copy.start()

    def finish(self):
        for c, copy in enumerate(self.copies):
            copy.wait()
            c0 = (c % self.per_slot) * self.cols
            self.w_s[c * W_ROWS:(c + 1) * W_ROWS, :] = (
                self.stage[c // self.per_slot, :, c0:c0 + self.cols].astype(_BF16))


def _layer_kernel(sink_ref, relb_ref, x_ref, xres_ref, nw_ref, win_hbm, qw_ref, kw_ref, bd_ref, dww_ref,
                  dwb_ref, lnw_ref, lnb_ref, wout_hbm, bucket_ref,
                  o_ref,
                  win_ref, wout_ref, wstage_s, wsem, wout_sem, wb_ref,
                  h_s, p1_s, p2_s, p3_s, ssq_s, q_s, k_s, v_s, u_s, u2_s, ga_s, gc_s, c_s, yc_s, y_s,
                  bias_s, o_s, dn_s, *, layer, tiles_per_seq, n_tiles):
    p_s = _RawProj(p1_s, p2_s, p3_s)
    ts = SEQ_TILE
    t = pl.program_id(0)
    sa = lax.rem(t, 2)
    sb = 1 - sa
    first_a = lax.rem(t, tiles_per_seq) == 0
    first_b = jnp.where(lax.rem(t + tiles_per_seq - 1, tiles_per_seq) == 0, 1, 0)

    A = _stream_a(sa, x_ref, xres_ref, nw_ref, win_ref, qw_ref, kw_ref, bd_ref,
                  h_s, p_s, ssq_s, q_s, k_s, v_s, u_s, u2_s, ga_s, gc_s)
    wout_load = _BackgroundWeightLoad(wout_hbm.at[layer], wout_ref, wstage_s, wout_sem)

    @pl.when(t == 0)
    def _():
        _load_weight_bf16(win_hbm.at[layer], win_ref, wstage_s, wsem,
                          side_work=_bias_table_items(relb_ref, bucket_ref, bias_s) + A["rms_first"])
        wout_load.start()
        for j in range(CONV_WIDTH):
            wb_ref[BF16_SUBLANES * j:BF16_SUBLANES * (j + 1), :] = jnp.broadcast_to(
                dww_ref[j:j + 1, :], (BF16_SUBLANES, D_CONV)).astype(_BF16)
        y_s[1] = jnp.zeros(y_s.shape[1:], _BF16)

    @pl.when(first_a)
    def _():
        k_s[sa, :, 0:KV_HALO, :] = jnp.zeros((4, KV_HALO, 128), _BF16)
        v_s[sa, :, 0:KV_HALO, :] = jnp.zeros((4, KV_HALO, 256), _BF16)
        for cb in range(N_SLABS):
            u_s[sa * N_SLABS + cb, 0:T_BLK, :] = jnp.zeros((T_BLK, 128), _F32)

    @pl.when(jnp.logical_not(first_a))
    def _():
        k_s[sa, :, 0:KV_HALO, :] = k_s[sb, :, ts:ts + KV_HALO, :]
        v_s[sa, :, 0:KV_HALO, :] = v_s[sb, :, ts:ts + KV_HALO, :]
        for cb in range(N_SLABS):
            u_s[sa * N_SLABS + cb, 0:T_BLK, :] = u_s[sb * N_SLABS + cb,
                                                     N_TBLK * U_PITCH:N_TBLK * U_PITCH + T_BLK, :]

    Bm = _stream_b(sb, first_b, functools.partial(_sink, sink_ref, layer), wb_ref, dwb_ref, lnw_ref, lnb_ref,
                   q_s, k_s, v_s, u2_s, ga_s, gc_s, c_s, yc_s, y_s, bias_s, o_s, dn_s)
    C = _stream_c(sa, xres_ref, wout_ref, o_ref, y_s)
    af, an = Bm["attn_front"], Bm["attn_norm"]

    conv_and_out = [item for pair in zip(C, Bm["conv"]) for item in pair]
    conv_tail = A["gate_c_prev"] + Bm["ln"] + Bm["regate"]
    in_proj = ([A["dot_qkvz"]] + A["q_sumsq"] + [A["k_sumsq"]] + A["gate_a"] + A["q_norm"]
               + [A["kv_store"]] + A["dot_ag"] + A["glu"] + A["pack"] + [A["dot_zc"]] + A["rms_next"])
    regions = [af + an, conv_and_out, conv_tail + in_proj]
    fill_regions = [in_proj]
    drain_regions = [af + an, conv_and_out, conv_tail]
    last_regions = [C]
    schedule = [(regions, (t >= 1) & (t < n_tiles)),
                (fill_regions, t == 0),
                (drain_regions, t == n_tiles),
                (last_regions, t == n_tiles + 1)]
    for group, active in schedule:
        trips = jnp.where(active, 1, 0)
        for region in group:
            def body(_, carry, region=region):
                for item in region:
                    item()
                return carry
            lax.fori_loop(0, trips, body, 0)

    @pl.when(t == 0)
    def _():
        wout_load.finish()


def _layer(x, layer, norm_w, w_in, q_norm_w, k_norm_w, sinks, dw_w, dw_b, ln_w, ln_b, w_out, rel_bias):
    B, S, D = x.shape
    ts = SEQ_TILE
    assert D == D_MODEL and S % ts == 0 and ts % BLOCK == 0
    n_tiles = B * S // ts

    idx = np.arange(256) // HEAD_DIM
    bd = jnp.asarray((idx[:, None] == idx[None, :]).astype(np.float32), dtype=_BF16)
    qw = (jnp.tile(q_norm_w[layer].astype(_F32), N_HEADS) * (HEAD_DIM ** -0.5 * LOG2E)).reshape(1, D_ATTN)
    kw = jnp.tile(k_norm_w[layer].astype(_F32), N_KV_HEADS).reshape(1, 128)
    xt = x.reshape(n_tiles, ts, D)

    const2 = lambda t: (0, 0)
    layer_row = lambda t: (layer, 0)
    in_specs = [
        pl.BlockSpec(memory_space=pltpu.SMEM),
        pl.BlockSpec(memory_space=pltpu.SMEM),
        pl.BlockSpec((None, ts, D), lambda t: (jnp.minimum(t + 1, n_tiles - 1), 0, 0)),
        pl.BlockSpec((None, ts, D), lambda t: (jnp.maximum(t - 2, 0), 0, 0)),
        pl.BlockSpec((1, D), layer_row),
        pl.BlockSpec(memory_space=pl.ANY),
        pl.BlockSpec((1, D_ATTN), const2),
        pl.BlockSpec((1, 128), const2),
        pl.BlockSpec((256, 256), const2),
        pl.BlockSpec((None, CONV_WIDTH, D_CONV), lambda t: (layer, 0, 0)),
        pl.BlockSpec((1, D_CONV), layer_row),
        pl.BlockSpec((1, D_CONV), layer_row),
        pl.BlockSpec((1, D_CONV), layer_row),
        pl.BlockSpec(memory_space=pl.ANY),
        pl.BlockSpec((BLOCK, 2 * BLOCK), const2),
    ]
    scratch = [
        pltpu.VMEM((D, D_IN), _BF16),
        pltpu.VMEM((D, D), _BF16),
        pltpu.VMEM((W_SLOTS, W_ROWS, D_IN), _F32),
        pltpu.SemaphoreType.DMA((W_SLOTS,)),
        pltpu.SemaphoreType.DMA((D // W_ROWS,)),
        pltpu.VMEM((CONV_WIDTH * BF16_SUBLANES, D_CONV), _BF16),
        pltpu.VMEM((ts, D), _BF16),
        pltpu.VMEM((ts, UA0 - Q0), _F32),
        pltpu.VMEM((ts, ZC0 - UA0), _F32),
        pltpu.VMEM((ts, D_IN - ZC0), _F32),
        pltpu.VMEM((ts, D_ATTN + 128), _F32),
        pltpu.VMEM((2, ts, D_ATTN), _BF16),
        pltpu.VMEM((2, 4, ts + KV_HALO, 128), _BF16),
        pltpu.VMEM((2, 4, ts + KV_HALO, 256), _BF16),
        pltpu.VMEM((2 * N_SLABS, U_ROWS, 128), _F32),
        pltpu.VMEM((N_SLABS, N_UTILES * BF16_SUBLANES, 128), _BF16),
        pltpu.VMEM((2, ts, D_ATTN), _F32),
        pltpu.VMEM((2, ts, D_CONV), _F32),
        pltpu.VMEM((ts, D_CONV), _F32),
        pltpu.VMEM((N_SLABS, C_ROWS, 128), _F32),
        pltpu.VMEM((2, ts, D), _BF16),
        pltpu.VMEM((2, 4, 2 * BLOCK, 2 * BLOCK), _F32),
        pltpu.VMEM((ATTN_SLOTS, 2 * BLOCK, 2 * BLOCK), _F32),
        pltpu.VMEM((ATTN_SLOTS, 2, BLOCK, 128), _F32),
    ]
    out = pl.pallas_call(
        functools.partial(_layer_kernel, layer=layer, tiles_per_seq=S // ts, n_tiles=n_tiles),
        grid=(n_tiles + 2,),
        in_specs=in_specs,
        out_specs=pl.BlockSpec((None, ts, D), lambda t: (jnp.maximum(t - 2, 0), 0, 0)),
        out_shape=jax.ShapeDtypeStruct((n_tiles, ts, D), x.dtype),
        scratch_shapes=scratch,
        compiler_params=pltpu.CompilerParams(
            dimension_semantics=("arbitrary",),
            vmem_limit_bytes=VMEM_LIMIT_BYTES),
        name="hybrid_layer",
    )(sinks.astype(_F32), rel_bias.astype(_F32), xt, xt,
      norm_w.astype(_F32), w_in.astype(_F32), qw, kw, bd,
      dw_w.astype(_F32), dw_b.astype(_F32), ln_w.astype(_F32),
      ln_b.astype(_F32), w_out.astype(_F32), jnp.asarray(_banded_buckets()))
    return out.reshape(B, S, D)


def kernel(x, norm_w, w_in, q_norm_w, k_norm_w, sinks, dw_w, dw_b, ln_w, ln_b, w_out, rel_bias):
    for layer in range(norm_w.shape[0]):
        x = _layer(x, layer, norm_w, w_in, q_norm_w, k_norm_w, sinks, dw_w, dw_b, ln_w, ln_b, w_out, rel_bias)
    return x
```

```python
import functools
import math

import numpy as np
import jax
import jax.numpy as jnp
from jax import lax
from jax.experimental import pallas as pl
from jax.experimental.pallas import tpu as pltpu

D_MODEL = 1024
D_ATTN = 512
D_CONV = 512
HEAD_DIM = 64
N_HEADS = 8
N_KV_HEADS = 2
BLOCK = 128
NUM_BUCKETS = 32
MAX_DISTANCE = 128
CONV_WIDTH = 31
D_IN = 2816
EPS = 1e-6
LN_EPS = 1e-5
NEG_INF = -1e30
LOG2E = math.log2(math.e)

SEQ_TILE = 512
KV_HALO = BLOCK
ROW_CHUNK = 64
SUBLANES = 8
N_SLABS = D_CONV // 128
BF16_SUBLANES = 16
N_VACC = 4
N_TBLK = BF16_SUBLANES
T_BLK = SEQ_TILE // N_TBLK
U_PITCH = T_BLK + 1
U_ROWS = ((N_TBLK + 1) * U_PITCH + 7) // 8 * 8
N_UTILES = CONV_WIDTH - 1 + T_BLK
C_PITCH = T_BLK + 8
C_ROWS = N_TBLK * C_PITCH
N_CHUNK = 256
W_ROWS = 128
W_SLOTS = 4
ATTN_SLOTS = 8
VMEM_LIMIT_BYTES = 56 * 1024 * 1024

Q0, K0, V0, ZA0, UA0, UG0, ZC0 = 0, 512, 640, 768, 1280, 1792, 2304

_F32 = jnp.float32
_BF16 = jnp.bfloat16


def _t5_bucket_table():
    qi = np.arange(BLOCK)[:, None]
    sj = np.arange(2 * BLOCK)[None, :]
    dist = qi + BLOCK - sj
    n = np.maximum(dist, 0)
    max_exact = NUM_BUCKETS // 2
    nf = np.maximum(n, 1).astype(np.float32)
    large = max_exact + (np.log(nf / max_exact) / math.log(MAX_DISTANCE / max_exact)
                         * (NUM_BUCKETS - max_exact)).astype(np.int32)
    large = np.minimum(large, NUM_BUCKETS - 1)
    bucket = np.where(n < max_exact, n, large).astype(np.int32)
    band = (dist >= 0) & (dist < BLOCK)
    return bucket, band


def _banded_buckets():
    bucket, band = _t5_bucket_table()
    return np.where(band, bucket, -1).astype(np.int32)


def _sigmoid(z):
    return 0.5 * jnp.tanh(0.5 * z) + 0.5


def _silu(z):
    h = 0.5 * z
    return h + h * jnp.tanh(h)


def _bias_table_items(relb_ref, bucket_ref, bias_s):
    def rows8(r0):
        def run():
            cur_keys = lax.broadcasted_iota(jnp.int32, (SUBLANES, 2 * BLOCK), 1) >= BLOCK
            bk = bucket_ref[r0:r0 + SUBLANES, :]
            accs = [jnp.full((SUBLANES, 2 * BLOCK), NEG_INF, _F32) for _ in range(N_HEADS)]
            for b in range(NUM_BUCKETS):
                hit = bk == b
                for h in range(N_HEADS):
                    accs[h] = jnp.where(hit, relb_ref[b, h] * LOG2E, accs[h])
            for h in range(N_HEADS):
                g, rem = divmod(h, 4)
                j, e = divmod(rem, 2)
                rows = slice(BLOCK * j + r0, BLOCK * j + r0 + SUBLANES)
                bias_s[0, 2 * g + e, rows, :] = accs[h]
                bias_s[1, 2 * g + e, rows, :] = jnp.where(cur_keys, accs[h], NEG_INF)
        return run

    return [rows8(r0) for r0 in range(0, BLOCK, SUBLANES)]


class _RawProj:
    bounds = (Q0, UA0, ZC0, D_IN)

    def __init__(self, *refs):
        self.refs = refs

    def _find(self, c0, c1):
        for ref, lo, hi in zip(self.refs, self.bounds[:-1], self.bounds[1:]):
            if lo <= c0 and c1 <= hi:
                return ref, c0 - lo, c1 - lo
        raise ValueError((c0, c1))

    def load(self, rows, c0, c1):
        ref, a, b = self._find(c0, c1)
        return ref[rows, a:b]

    def store(self, c0, c1, val):
        ref, a, b = self._find(c0, c1)
        ref[:, a:b] = val


def _sink(sink_ref, layer, head):
    return sink_ref[layer, head]


def _row_chunks():
    return [slice(r, r + ROW_CHUNK) for r in range(0, SEQ_TILE, ROW_CHUNK)]


def _stream_a(sa, x_ref, x0_ref, nw_ref, win_ref, qw_ref, kw_ref, bd_ref,
              h_s, p_s, ssq_s, q_s, k_s, v_s, u_s, u2_s, ga_s, gc_s):
    ts = SEQ_TILE

    def rms(src_ref, r):
        def run():
            xc = src_ref[r:r + ROW_CHUNK, :]
            ss = jnp.sum(xc * xc, axis=-1, keepdims=True)
            rs = lax.rsqrt(ss * (1.0 / D_MODEL) + EPS)
            h_s[r:r + ROW_CHUNK, :] = ((xc * rs) * nw_ref[...]).astype(_BF16)
        return run

    def dot(c0, c1):
        def run():
            p_s.store(c0, c1, jnp.dot(h_s[...], win_ref[:, c0:c1], preferred_element_type=_F32))
        return run

    def q_sumsq(half):
        def run():
            cs = slice(256 * half, 256 * half + 256)
            q = p_s.load(slice(None), Q0 + cs.start, Q0 + cs.stop)
            ssq_s[:, cs] = jnp.dot((q * q).astype(_BF16), bd_ref[...], preferred_element_type=_F32)
        return run

    def q_norm(half):
        def run():
            cs = slice(256 * half, 256 * half + 256)
            for rr in _row_chunks():
                q = p_s.load(rr, Q0 + cs.start, Q0 + cs.stop)
                qn =(q * lax.rsqrt(ssq_s[rr, cs] * (1.0 / HEAD_DIM) + EPS)) * qw_ref[:, cs]
                q_s[sa, rr, cs] = qn.astype(_BF16)
        return run

    def k_sumsq():
        k = p_s.load(slice(None), K0, K0 + 128)
        ssq_s[:, D_ATTN:D_ATTN + 128] = jnp.dot((k * k).astype(_BF16), bd_ref[0:128, 0:128],
                                                preferred_element_type=_F32)

    def kv_store():
        lo = lax.broadcasted_iota(jnp.int32, (ROW_CHUNK, 128), 1) < HEAD_DIM
        zero = jnp.zeros((ROW_CHUNK, 128), _F32)
        one = jnp.ones((ROW_CHUNK, 128), _F32)
        ones_lo = jnp.where(lo, one, zero).astype(_BF16)
        ones_hi = jnp.where(lo, zero, one).astype(_BF16)
        for rr in _row_chunks():
            cur = slice(KV_HALO + rr.start, KV_HALO + rr.stop)
            k = p_s.load(rr, K0, K0 + 128)
            v = p_s.load(rr, V0, V0 + 128)
            kn = (k * lax.rsqrt(ssq_s[rr, D_ATTN:D_ATTN + 128] * (1.0 / HEAD_DIM) + EPS)) * kw_ref[...]
            kn_sw = pltpu.roll(kn, 64, axis=1)
            v_sw = pltpu.roll(v, 64, axis=1)
            k_s[sa, 0, cur, :] = jnp.where(lo, kn, zero).astype(_BF16)
            k_s[sa, 1, cur, :] = jnp.where(lo, zero, kn_sw).astype(_BF16)
            k_s[sa, 2, cur, :] = jnp.where(lo, kn_sw, zero).astype(_BF16)
            k_s[sa, 3, cur, :] = jnp.where(lo, zero, kn).astype(_BF16)
            v_s[sa, 0, cur, 0:128] = jnp.where(lo, v, zero).astype(_BF16)
            v_s[sa, 1, cur, 0:128] = jnp.where(lo, zero, v_sw).astype(_BF16)
            v_s[sa, 2, cur, 0:128] = jnp.where(lo, v_sw, zero).astype(_BF16)
            v_s[sa, 3, cur, 0:128] = jnp.where(lo, zero, v).astype(_BF16)
            v_s[sa, 0, cur, 128:256] = ones_lo
            v_s[sa, 1, cur, 128:256] = ones_hi
            v_s[sa, 2, cur, 128:256] = ones_lo
            v_s[sa, 3, cur, 128:256] = ones_hi

    def gate(c0, dst, half, slot):
        def run():
            cs = slice(256 * half, 256 * half + 256)
            for rr in _row_chunks():
                z = p_s.load(rr, c0 + 256 * half, c0 + 256 * half + 256)
                dst[slot, rr, cs] = _silu(z)
        return run

    def glu(half):
        def run():
            for blk in range(N_TBLK):
                tr = slice(T_BLK * blk, T_BLK * (blk + 1))
                a = p_s.load(tr, UA0 + 256 * half, UA0 + 256 * half + 256)
                g = p_s.load(tr, UG0 + 256 * half, UG0 + 256 * half + 256)
                u = a * _sigmoid(g)
                p0 = U_PITCH * (blk + 1)
                for sl in range(2):
                    u_s[sa * N_SLABS + 2 * half + sl, p0:p0 + T_BLK, :] = u[:, 128 * sl:128 * sl + 128]
        return run

    def pack(cb):
        taps = CONV_WIDTH - 1

        def run():
            for k in range(N_UTILES):
                tau = k - taps
                base = U_PITCH + tau if tau >= 0 else T_BLK + tau
                halves = [u_s[sa * N_SLABS + cb,
                              pl.ds(base + h * SUBLANES * U_PITCH, SUBLANES, stride=U_PITCH), :]
                          for h in range(N_TBLK // SUBLANES)]
                u2_s[cb, BF16_SUBLANES * k:BF16_SUBLANES * (k + 1), :] = (
                    jnp.concatenate(halves, axis=0).astype(_BF16))
        return run

    return dict(
        pack=[pack(cb) for cb in range(N_SLABS)],
        rms_next=[rms(x_ref, r) for r in range(0, ts, ROW_CHUNK)],
        rms_first=[rms(x0_ref, r) for r in range(0, ts, ROW_CHUNK)],
        dot_qkvz=dot(Q0, UA0), dot_zc=dot(ZC0, D_IN),
        dot_ag=[dot(c0 + 256 * half, c0 + 256 * half + 256) for half in range(2) for c0 in (UA0, UG0)],
        q_sumsq=[q_sumsq(0), q_sumsq(1)], q_norm=[q_norm(0), q_norm(1)],
        k_sumsq=k_sumsq, kv_store=kv_store,
        gate_a=[gate(ZA0, ga_s, 0, sa), gate(ZA0, ga_s, 1, sa)],
        gate_c_prev=[gate(ZC0, gc_s, 0, 1 - sa), gate(ZC0, gc_s, 1, 1 - sa)],
        glu=[glu(0), glu(1)])


def _stream_b(sb, first_b, sink_of, wb_ref, dwb_ref, lnw_ref, lnb_ref,
              q_s, k_s, v_s, u2_s, ga_s, gc_s, c_s, yc_s, y_s, bias_s, o_s, dn_s):
    ts = SEQ_TILE
    taps = CONV_WIDTH - 1

    def conv(cb):
        def run():
            ls = slice(128 * cb, 128 * cb + 128)
            bv = jnp.broadcast_to(dwb_ref[:, ls], (BF16_SUBLANES, 128))
            wv = [wb_ref[BF16_SUBLANES * j:BF16_SUBLANES * (j + 1), ls].astype(_F32) for j in range(CONV_WIDTH)]
            for r0 in range(0, T_BLK, N_VACC):
                accs = [None] * N_VACC
                for k in range(r0, r0 + N_VACC + taps):
                    win = u2_s[cb, BF16_SUBLANES * k:BF16_SUBLANES * (k + 1), :].astype(_F32)
                    for n in range(N_VACC):
                        j = k - (r0 + n)
                        if 0 <= j < CONV_WIDTH:
                            term = win * wv[j]
                            accs[n] = term if accs[n] is None else accs[n] + term
                for n in range(N_VACC):
                    r = r0 + n
                    c_s[BF16_SUBLANES * r:BF16_SUBLANES * (r + 1), ls] = accs[n] + bv
        return run

    def ln(r):
        def run():
            y = c_s[r:r + ROW_CHUNK, :]
            mu = jnp.sum(y, axis=-1, keepdims=True) * (1.0 / D_CONV)
            d = y - mu
            var = jnp.sum(d * d, axis=-1, keepdims=True) * (1.0 / D_CONV)
            yn = (d * lax.rsqrt(var + LN_EPS)) * lnw_ref[...] + lnb_ref[...]
            act = _silu(yn)
            for n in range(ROW_CHUNK // SUBLANES):
                row = r + SUBLANES * n
                tau, blk0 = divmod(row, BF16_SUBLANES)
                for cb in range(N_SLABS):
                    yc_s[cb, pl.ds(C_PITCH * blk0 + tau, SUBLANES, stride=C_PITCH), :] = (
                        act[SUBLANES * n:SUBLANES * (n + 1), 128 * cb:128 * cb + 128])
        return run

    def regate(blk):
        def run():
            tr = slice(T_BLK * blk, T_BLK * (blk + 1))
            for cb in range(N_SLABS):
                ls = slice(128 * cb, 128 * cb + 128)
                val = yc_s[cb, C_PITCH * blk:C_PITCH * blk + T_BLK, :] * gc_s[sb, tr, ls]
                y_s[sb, tr, D_ATTN + 128 * cb:D_ATTN + 128 * cb + 128] = val.astype(_BF16)
        return run

    def attn_front(b, g):
        def run():
            slot = (2 * b + g) % ATTN_SLOTS
            rows = slice(BLOCK * b, BLOCK * b + BLOCK)
            krows = slice(BLOCK * b, BLOCK * b + 2 * BLOCK)
            tbl = first_b if b == 0 else 0
            lo_q = lax.broadcasted_iota(jnp.int32, (BLOCK, 128), 1) < HEAD_DIM
            c0 = 256 * g
            qg = jnp.concatenate([q_s[sb, rows, c0:c0 + 128], q_s[sb, rows, c0 + 128:c0 + 256]], axis=0)
            kk = jnp.concatenate([k_s[sb, 2 * g, krows, :], k_s[sb, 2 * g + 1, krows, :]], axis=0)
            s = lax.dot_general(qg, kk, (((1,), (1,)), ((), ())),
                                preferred_element_type=_F32)
            ps = [[None, None], [None, None]]
            ex = [[None, None], [None, None]]
            for e in range(2):
                se = s[:, 2 * BLOCK * e:2 * BLOCK * (e + 1)] + bias_s[tbl, 2 * g + e]
                for j in range(2):
                    sink = sink_of(4 * g + 2 * j + e) * LOG2E
                    sj = se[BLOCK * j:BLOCK * j + BLOCK, :]
                    m = jnp.maximum(jnp.max(sj, axis=-1, keepdims=True), sink)
                    ps[j][e] = jnp.exp2(sj - m).astype(_BF16)
                    ex[j][e] = jnp.exp2(sink - m)
            p = jnp.concatenate([jnp.concatenate(ps[0], axis=1), jnp.concatenate(ps[1], axis=1)], axis=0)
            vv = jnp.concatenate([v_s[sb, 2 * g, krows, :], v_s[sb, 2 * g + 1, krows, :]], axis=0)
            o_s[slot] = jnp.dot(p, vv, preferred_element_type=_F32)
            for j in range(2):
                dn_s[slot, j] = jnp.where(lo_q, ex[j][0], ex[j][1])
        return run

    def attn_norm(b, g):
        def run():
            slot = (2 * b + g) % ATTN_SLOTS
            rows = slice(BLOCK * b, BLOCK * b + BLOCK)
            for j in range(2):
                oj = o_s[slot, BLOCK * j:BLOCK * j + BLOCK, :]
                den = oj[:, 128:256] + dn_s[slot, j]
                cs = slice(256 * g + 128 * j, 256 * g + 128 * j + 128)
                y_s[sb, rows, cs] = ((oj[:, 0:128] / den) * ga_s[sb, rows, cs]).astype(_BF16)
        return run

    pairs = [(b, g) for b in range(ts // BLOCK) for g in range(N_KV_HEADS)]
    return dict(attn_front=[attn_front(b, g) for b, g in pairs],
                attn_norm=[attn_norm(b, g) for b, g in pairs],
                conv=[conv(cb) for cb in range(N_SLABS)],
                ln=[ln(r) for r in range(0, ts, ROW_CHUNK)],
                regate=[regate(blk) for blk in range(N_TBLK)])


def _stream_c(sy, xres_ref, wout_ref, o_ref, y_s):
    def outproj(n):
        def run():
            o_ref[:, n:n + N_CHUNK] = xres_ref[:, n:n + N_CHUNK] + jnp.dot(
                y_s[sy], wout_ref[:, n:n + N_CHUNK], preferred_element_type=_F32)
        return run

    return [outproj(n) for n in range(0, D_MODEL, N_CHUNK)]


def _load_weight_bf16(w_hbm, w_s, stage, sem, side_work=()):
    rows, cols = w_s.shape
    n_chunks = rows // W_ROWS
    n_slots = stage.shape[0]
    per_chunk = -(-len(side_work) // n_chunks)

    def copy(c):
        slot = c % n_slots
        return pltpu.make_async_copy(w_hbm.at[pl.ds(c * W_ROWS, W_ROWS), :],
                                     stage.at[slot, :, pl.ds(0, cols)], sem.at[slot])

    for c in range(min(n_slots - 1, n_chunks)):
        copy(c).start()
    for c in range(n_chunks):
        nxt = c + n_slots - 1
        if nxt < n_chunks:
            copy(nxt).start()
        for item in side_work[c * per_chunk:(c + 1) * per_chunk]:
            item()
        copy(c).wait()
        w_s[c * W_ROWS:(c + 1) * W_ROWS, :] = stage[c % n_slots, :, 0:cols].astype(_BF16)


class _BackgroundWeightLoad:
    def __init__(self, w_hbm, w_s, stage, sem):
        rows, cols = w_s.shape
        per_slot = stage.shape[2] // cols
        self.n_chunks = rows // W_ROWS
        assert self.n_chunks <= per_slot * stage.shape[0]
        self.w_s, self.stage, self.cols, self.per_slot = w_s, stage, cols, per_slot
        self.copies = [
            pltpu.make_async_copy(w_hbm.at[pl.ds(c * W_ROWS, W_ROWS), :],
                                  stage.at[c // per_slot, :, pl.ds((c % per_slot) * cols, cols)], sem.at[c])
            for c in range(self.n_chunks)]

    def start(self):
        for copy in self.copies:
            copy.start()

    def finish(self):
        for c, copy in enumerate(self.copies):
            copy.wait()
            c0 = (c % self.per_slot) * self.cols
            self.w_s[c * W_ROWS:(c + 1) * W_ROWS, :] = (
                self.stage[c // self.per_slot, :, c0:c0 + self.cols].astype(_BF16))


def _layer_kernel(sink_ref, relb_ref, x_ref, xres_ref, nw_ref, win_hbm, qn_ref, kn_ref, bd_ref, dww_ref,
                  dwb_ref, lnw_ref, lnb_ref, wout_hbm, bucket_ref,
                  o_ref,
                  win_ref, wout_ref, wstage_s, wsem, wout_sem, wb_ref, qw_ref, kw_ref,
                  h_s, p1_s, p2_s, p3_s, ssq_s, q_s, k_s, v_s, u_s, u2_s, ga_s, gc_s, c_s, yc_s, y_s,
                  bias_s, o_s, dn_s, *, layer, tiles_per_seq, n_tiles):
    p_s = _RawProj(p1_s, p2_s, p3_s)
    ts = SEQ_TILE
    t = pl.program_id(0)
    sa = lax.rem(t, 2)
    sb = 1 - sa
    first_a = lax.rem(t, tiles_per_seq) == 0
    first_b = jnp.where(lax.rem(t + tiles_per_seq - 1, tiles_per_seq) == 0, 1, 0)

    A = _stream_a(sa, x_ref, xres_ref, nw_ref, win_ref, qw_ref, kw_ref, bd_ref,
                  h_s, p_s, ssq_s, q_s, k_s, v_s, u_s, u2_s, ga_s, gc_s)
    wout_load = _BackgroundWeightLoad(wout_hbm.at[layer], wout_ref, wstage_s, wout_sem)

    @pl.when(t == 0)
    def _():
        _load_weight_bf16(win_hbm.at[layer], win_ref, wstage_s, wsem,
                          side_work=_bias_table_items(relb_ref, bucket_ref, bias_s) + A["rms_first"])
        wout_load.start()
        qw_ref[...] = jnp.concatenate([qn_ref[...] * (HEAD_DIM ** -0.5 * LOG2E)] * N_HEADS, axis=1)
        kw_ref[...] = jnp.concatenate([kn_ref[...]] * N_KV_HEADS, axis=1)
        for j in range(CONV_WIDTH):
            wb_ref[BF16_SUBLANES * j:BF16_SUBLANES * (j + 1), :] = jnp.broadcast_to(
                dww_ref[j:j + 1, :], (BF16_SUBLANES, D_CONV)).astype(_BF16)
        y_s[1] = jnp.zeros(y_s.shape[1:], _BF16)

    @pl.when(first_a)
    def _():
        k_s[sa, :, 0:KV_HALO, :] = jnp.zeros((4, KV_HALO, 128), _BF16)
        v_s[sa, :, 0:KV_HALO, :] = jnp.zeros((4, KV_HALO, 256), _BF16)
        for cb in range(N_SLABS):
            u_s[sa * N_SLABS + cb, 0:T_BLK, :] = jnp.zeros((T_BLK, 128), _F32)

    @pl.when(jnp.logical_not(first_a))
    def _():
        k_s[sa, :, 0:KV_HALO, :] = k_s[sb, :, ts:ts + KV_HALO, :]
        v_s[sa, :, 0:KV_HALO, :] = v_s[sb, :, ts:ts + KV_HALO, :]
        for cb in range(N_SLABS):
            u_s[sa * N_SLABS + cb, 0:T_BLK, :] = u_s[sb * N_SLABS + cb,
                                                     N_TBLK * U_PITCH:N_TBLK * U_PITCH + T_BLK, :]

    Bm = _stream_b(sb, first_b, functools.partial(_sink, sink_ref, layer), wb_ref, dwb_ref, lnw_ref, lnb_ref,
                   q_s, k_s, v_s, u2_s, ga_s, gc_s, c_s, yc_s, y_s, bias_s, o_s, dn_s)
    C = _stream_c(sa, xres_ref, wout_ref, o_ref, y_s)
    af, an = Bm["attn_front"], Bm["attn_norm"]

    conv_and_out = [item for pair in zip(C, Bm["conv"]) for item in pair]
    conv_tail = A["gate_c_prev"] + Bm["ln"] + Bm["regate"]
    in_proj = ([A["dot_qkvz"]] + A["q_sumsq"] + [A["k_sumsq"]] + A["gate_a"] + A["q_norm"]
               + [A["kv_store"]] + A["dot_ag"] + A["glu"] + A["pack"] + [A["dot_zc"]] + A["rms_next"])
    regions = [af + an, conv_and_out, conv_tail + in_proj]
    fill_regions = [in_proj]
    drain_regions = [af + an, conv_and_out, conv_tail]
    last_regions = [C]
    schedule = [(regions, (t >= 1) & (t < n_tiles)),
                (fill_regions, t == 0),
                (drain_regions, t == n_tiles),
                (last_regions, t == n_tiles + 1)]
    for group, active in schedule:
        trips = jnp.where(active, 1, 0)
        for region in group:
            def body(_, carry, region=region):
                for item in region:
                    item()
                return carry
            lax.fori_loop(0, trips, body, 0)

    @pl.when(t == 0)
    def _():
        wout_load.finish()


def _layer(x, layer, norm_w, w_in, q_norm_w, k_norm_w, sinks, dw_w, dw_b, ln_w, ln_b, w_out, rel_bias):
    B, S, D = x.shape
    ts = SEQ_TILE
    assert D == D_MODEL and S % ts == 0 and ts % BLOCK == 0
    n_tiles = B * S // ts

    idx = np.arange(256) // HEAD_DIM
    bd = jnp.asarray((idx[:, None] == idx[None, :]).astype(np.float32), dtype=_BF16)
    xt = x.reshape(n_tiles, ts, D)

    const2 = lambda t: (0, 0)
    layer_row = lambda t: (layer, 0)
    in_specs = [
        pl.BlockSpec(memory_space=pltpu.SMEM),
        pl.BlockSpec(memory_space=pltpu.SMEM),
        pl.BlockSpec((None, ts, D), lambda t: (jnp.minimum(t + 1, n_tiles - 1), 0, 0)),
        pl.BlockSpec((None, ts, D), lambda t: (jnp.maximum(t - 2, 0), 0, 0)),
        pl.BlockSpec((1, D), layer_row),
        pl.BlockSpec(memory_space=pl.ANY),
        pl.BlockSpec((1, HEAD_DIM), layer_row),
        pl.BlockSpec((1, HEAD_DIM), layer_row),
        pl.BlockSpec((256, 256), const2),
        pl.BlockSpec((None, CONV_WIDTH, D_CONV), lambda t: (layer, 0, 0)),
        pl.BlockSpec((1, D_CONV), layer_row),
        pl.BlockSpec((1, D_CONV), layer_row),
        pl.BlockSpec((1, D_CONV), layer_row),
        pl.BlockSpec(memory_space=pl.ANY),
        pl.BlockSpec((BLOCK, 2 * BLOCK), const2),
    ]
    scratch = [
        pltpu.VMEM((D, D_IN), _BF16),
        pltpu.VMEM((D, D), _BF16),
        pltpu.VMEM((W_SLOTS, W_ROWS, D_IN), _F32),
        pltpu.SemaphoreType.DMA((W_SLOTS,)),
        pltpu.SemaphoreType.DMA((D // W_ROWS,)),
        pltpu.VMEM((CONV_WIDTH * BF16_SUBLANES, D_CONV), _BF16),
        pltpu.VMEM((1, D_ATTN), _F32),
        pltpu.VMEM((1, N_KV_HEADS * HEAD_DIM), _F32),
        pltpu.VMEM((ts, D), _BF16),
        pltpu.VMEM((ts, UA0 - Q0), _F32),
        pltpu.VMEM((ts, ZC0 - UA0), _F32),
        pltpu.VMEM((ts, D_IN - ZC0), _F32),
        pltpu.VMEM((ts, D_ATTN + 128), _F32),
        pltpu.VMEM((2, ts, D_ATTN), _BF16),
        pltpu.VMEM((2, 4, ts + KV_HALO, 128), _BF16),
        pltpu.VMEM((2, 4, ts + KV_HALO, 256), _BF16),
        pltpu.VMEM((2 * N_SLABS, U_ROWS, 128), _F32),
        pltpu.VMEM((N_SLABS, N_UTILES * BF16_SUBLANES, 128), _BF16),
        pltpu.VMEM((2, ts, D_ATTN), _F32),
        pltpu.VMEM((2, ts, D_CONV), _F32),
        pltpu.VMEM((ts, D_CONV), _F32),
        pltpu.VMEM((N_SLABS, C_ROWS, 128), _F32),
        pltpu.VMEM((2, ts, D), _BF16),
        pltpu.VMEM((2, 4, 2 * BLOCK, 2 * BLOCK), _F32),
        pltpu.VMEM((ATTN_SLOTS, 2 * BLOCK, 2 * BLOCK), _F32),
        pltpu.VMEM((ATTN_SLOTS, 2, BLOCK, 128), _F32),
    ]
    out = pl.pallas_call(
        functools.partial(_layer_kernel, layer=layer, tiles_per_seq=S // ts, n_tiles=n_tiles),
        grid=(n_tiles + 2,),
        in_specs=in_specs,
        out_specs=pl.BlockSpec((None, ts, D), lambda t: (jnp.maximum(t - 2, 0), 0, 0)),
        out_shape=jax.ShapeDtypeStruct((n_tiles, ts, D), x.dtype),
        scratch_shapes=scratch,
        compiler_params=pltpu.CompilerParams(
            dimension_semantics=("arbitrary",),
            vmem_limit_bytes=VMEM_LIMIT_BYTES),
        name="hybrid_layer",
    )(sinks.astype(_F32), rel_bias.astype(_F32), xt, xt,
      norm_w.astype(_F32), w_in.astype(_F32), q_norm_w.astype(_F32), k_norm_w.astype(_F32), bd,
      dw_w.astype(_F32), dw_b.astype(_F32), ln_w.astype(_F32),
      ln_b.astype(_F32), w_out.astype(_F32), jnp.asarray(_banded_buckets()))
    return out.reshape(B, S, D)


def kernel(x, norm_w, w_in, q_norm_w, k_norm_w, sinks, dw_w, dw_b, ln_w, ln_b, w_out, rel_bias):
    for layer in range(norm_w.shape[0]):
        x = _layer(x, layer, norm_w, w_in, q_norm_w, k_norm_w, sinks, dw_w, dw_b, ln_w, ln_b, w_out, rel_bias)
    return x
```

```python
import functools
import math

import numpy as np
import jax
import jax.numpy as jnp
from jax import lax
from jax.experimental import pallas as pl
from jax.experimental.pallas import tpu as pltpu

D_MODEL = 1024
D_ATTN = 512
D_CONV = 512
HEAD_DIM = 64
N_HEADS = 8
N_KV_HEADS = 2
BLOCK = 128
NUM_BUCKETS = 32
MAX_DISTANCE = 128
CONV_WIDTH = 31
D_IN = 2816
EPS = 1e-6
LN_EPS = 1e-5
NEG_INF = -1e30
LOG2E = math.log2(math.e)

SEQ_TILE = 512
KV_HALO = BLOCK
ROW_CHUNK = 64
SUBLANES = 8
N_SLABS = D_CONV // 128
BF16_SUBLANES = 16
N_VACC = 4
N_TBLK = BF16_SUBLANES
T_BLK = SEQ_TILE // N_TBLK
U_PITCH = T_BLK + 1
U_ROWS = ((N_TBLK + 1) * U_PITCH + 7) // 8 * 8
N_UTILES = CONV_WIDTH - 1 + T_BLK
C_PITCH = T_BLK + 8
C_ROWS = N_TBLK * C_PITCH
N_CHUNK = 256
W_ROWS = 128
W_SLOTS = 4
ATTN_SLOTS = 8
VMEM_LIMIT_BYTES = 56 * 1024 * 1024

Q0, K0, V0, ZA0, UA0, UG0, ZC0 = 0, 512, 640, 768, 1280, 1792, 2304

_F32 = jnp.float32
_BF16 = jnp.bfloat16


def _t5_bucket_table():
    qi = np.arange(BLOCK)[:, None]
    sj = np.arange(2 * BLOCK)[None, :]
    dist = qi + BLOCK - sj
    n = np.maximum(dist, 0)
    max_exact = NUM_BUCKETS // 2
    nf = np.maximum(n, 1).astype(np.float32)
    large = max_exact + (np.log(nf / max_exact) / math.log(MAX_DISTANCE / max_exact)
                         * (NUM_BUCKETS - max_exact)).astype(np.int32)
    large = np.minimum(large, NUM_BUCKETS - 1)
    bucket = np.where(n < max_exact, n, large).astype(np.int32)
    band = (dist >= 0) & (dist < BLOCK)
    return bucket, band


def _banded_buckets():
    bucket, band = _t5_bucket_table()
    return np.where(band, bucket, -1).astype(np.int32)


def _sigmoid(z):
    return 0.5 * jnp.tanh(0.5 * z) + 0.5


def _silu(z):
    h = 0.5 * z
    return h + h * jnp.tanh(h)


def _bias_table_items(relb_ref, bucket_ref, bias_s):
    def rows8(r0):
        def run():
            cur_keys = lax.broadcasted_iota(jnp.int32, (SUBLANES, 2 * BLOCK), 1) >= BLOCK
            bk = bucket_ref[r0:r0 + SUBLANES, :]
            accs = [jnp.full((SUBLANES, 2 * BLOCK), NEG_INF, _F32) for _ in range(N_HEADS)]
            for b in range(NUM_BUCKETS):
                hit = bk == b
                for h in range(N_HEADS):
                    accs[h] = jnp.where(hit, relb_ref[b, h] * LOG2E, accs[h])
            for h in range(N_HEADS):
                g, rem = divmod(h, 4)
                j, e = divmod(rem, 2)
                rows = slice(BLOCK * j + r0, BLOCK * j + r0 + SUBLANES)
                bias_s[0, 2 * g + e, rows, :] = accs[h]
                bias_s[1, 2 * g + e, rows, :] = jnp.where(cur_keys, accs[h], NEG_INF)
        return run

    return [rows8(r0) for r0 in range(0, BLOCK, SUBLANES)]


class _RawProj:
    bounds = (Q0, UA0, ZC0, D_IN)

    def __init__(self, *refs):
        self.refs = refs

    def _find(self, c0, c1):
        for ref, lo, hi in zip(self.refs, self.bounds[:-1], self.bounds[1:]):
            if lo <= c0 and c1 <= hi:
                return ref, c0 - lo, c1 - lo
        raise ValueError((c0, c1))

    def load(self, rows, c0, c1):
        ref, a, b = self._find(c0, c1)
        return ref[rows, a:b]

    def store(self, c0, c1, val):
        ref, a, b = self._find(c0, c1)
        ref[:, a:b] = val


def _sink(sink_ref, layer, head):
    return sink_ref[layer, head]


def _row_chunks():
    return [slice(r, r + ROW_CHUNK) for r in range(0, SEQ_TILE, ROW_CHUNK)]


def _stream_a(sa, x_ref, x0_ref, nw_ref, win_ref, qw_ref, kw_ref, bd_ref,
              h_s, p_s, ssq_s, q_s, k_s, v_s, u_s, u2_s, ga_s, gc_s):
    ts = SEQ_TILE

    def rms(src_ref, r):
        def run():
            xc = src_ref[r:r + ROW_CHUNK, :]
            ss = jnp.sum(xc * xc, axis=-1, keepdims=True)
            rs = lax.rsqrt(ss * (1.0 / D_MODEL) + EPS)
            h_s[r:r + ROW_CHUNK, :] = ((xc * rs) * nw_ref[...]).astype(_BF16)
        return run

    def dot(c0, c1):
        def run():
            p_s.store(c0, c1, jnp.dot(h_s[...], win_ref[:, c0:c1], preferred_element_type=_F32))
        return run

    def q_sumsq(half):
        def run():
            cs = slice(256 * half, 256 * half + 256)
            q = p_s.load(slice(None), Q0 + cs.start, Q0 + cs.stop)
            ssq_s[:, cs] = jnp.dot((q * q).astype(_BF16), bd_ref[...], preferred_element_type=_F32)
        return run

    def q_norm(half):
        def run():
            cs = slice(256 * half, 256 * half + 256)
            for rr in _row_chunks():
                q = p_s.load(rr, Q0 + cs.start, Q0 + cs.stop)
                qn =(q * lax.rsqrt(ssq_s[rr, cs] * (1.0 / HEAD_DIM) + EPS)) * qw_ref[:, cs]
                q_s[sa, rr, cs] = qn.astype(_BF16)
        return run

    def k_sumsq():
        k = p_s.load(slice(None), K0, K0 + 128)
        ssq_s[:, D_ATTN:D_ATTN + 128] = jnp.dot((k * k).astype(_BF16), bd_ref[0:128, 0:128],
                                                preferred_element_type=_F32)

    def kv_store():
        lo = lax.broadcasted_iota(jnp.int32, (ROW_CHUNK, 128), 1) < HEAD_DIM
        zero = jnp.zeros((ROW_CHUNK, 128), _F32)
        one = jnp.ones((ROW_CHUNK, 128), _F32)
        ones_lo = jnp.where(lo, one, zero).astype(_BF16)
        ones_hi = jnp.where(lo, zero, one).astype(_BF16)
        for rr in _row_chunks():
            cur = slice(KV_HALO + rr.start, KV_HALO + rr.stop)
            k = p_s.load(rr, K0, K0 + 128)
            v = p_s.load(rr, V0, V0 + 128)
            kn = (k * lax.rsqrt(ssq_s[rr, D_ATTN:D_ATTN + 128] * (1.0 / HEAD_DIM) + EPS)) * kw_ref[...]
            kn_sw = pltpu.roll(kn, 64, axis=1)
            v_sw = pltpu.roll(v, 64, axis=1)
            k_s[sa, 0, cur, :] = jnp.where(lo, kn, zero).astype(_BF16)
            k_s[sa, 1, cur, :] = jnp.where(lo, zero, kn_sw).astype(_BF16)
            k_s[sa, 2, cur, :] = jnp.where(lo, kn_sw, zero).astype(_BF16)
            k_s[sa, 3, cur, :] = jnp.where(lo, zero, kn).astype(_BF16)
            v_s[sa, 0, cur, 0:128] = jnp.where(lo, v, zero).astype(_BF16)
            v_s[sa, 1, cur, 0:128] = jnp.where(lo, zero, v_sw).astype(_BF16)
            v_s[sa, 2, cur, 0:128] = jnp.where(lo, v_sw, zero).astype(_BF16)
            v_s[sa, 3, cur, 0:128] = jnp.where(lo, zero, v).astype(_BF16)
            v_s[sa, 0, cur, 128:256] = ones_lo
            v_s[sa, 1, cur, 128:256] = ones_hi
            v_s[sa, 2, cur, 128:256] = ones_lo
            v_s[sa, 3, cur, 128:256] = ones_hi

    def gate(c0, dst, half, slot):
        def run():
            cs = slice(256 * half, 256 * half + 256)
            for rr in _row_chunks():
                z = p_s.load(rr, c0 + 256 * half, c0 + 256 * half + 256)
                dst[slot, rr, cs] = _silu(z)
        return run

    def glu(half):
        def run():
            for blk in range(N_TBLK):
                tr = slice(T_BLK * blk, T_BLK * (blk + 1))
                a = p_s.load(tr, UA0 + 256 * half, UA0 + 256 * half + 256)
                g = p_s.load(tr, UG0 + 256 * half, UG0 + 256 * half + 256)
                u = a * _sigmoid(g)
                p0 = U_PITCH * (blk + 1)
                for sl in range(2):
                    u_s[sa * N_SLABS + 2 * half + sl, p0:p0 + T_BLK, :] = u[:, 128 * sl:128 * sl + 128]
        return run

    def pack(cb):
        taps = CONV_WIDTH - 1

        def run():
            for k in range(N_UTILES):
                tau = k - taps
                base = U_PITCH + tau if tau >= 0 else T_BLK + tau
                halves = [u_s[sa * N_SLABS + cb,
                              pl.ds(base + h * SUBLANES * U_PITCH, SUBLANES, stride=U_PITCH), :]
                          for h in range(N_TBLK // SUBLANES)]
                u2_s[cb, BF16_SUBLANES * k:BF16_SUBLANES * (k + 1), :] = (
                    jnp.concatenate(halves, axis=0).astype(_BF16))
        return run

    return dict(
        pack=[pack(cb) for cb in range(N_SLABS)],
        rms_next=[rms(x_ref, r) for r in range(0, ts, ROW_CHUNK)],
        rms_first=[rms(x0_ref, r) for r in range(0, ts, ROW_CHUNK)],
        dot_qkvz=dot(Q0, UA0), dot_zc=dot(ZC0, D_IN),
        dot_ag=[dot(c0 + 256 * half, c0 + 256 * half + 256) for half in range(2) for c0 in (UA0, UG0)],
        q_sumsq=[q_sumsq(0), q_sumsq(1)], q_norm=[q_norm(0), q_norm(1)],
        k_sumsq=k_sumsq, kv_store=kv_store,
        gate_a=[gate(ZA0, ga_s, 0, sa), gate(ZA0, ga_s, 1, sa)],
        gate_c_prev=[gate(ZC0, gc_s, 0, 1 - sa), gate(ZC0, gc_s, 1, 1 - sa)],
        glu=[glu(0), glu(1)])


def _stream_b(sb, first_b, sink_of, wb_ref, dwb_ref, lnw_ref, lnb_ref,
              q_s, k_s, v_s, u2_s, ga_s, gc_s, c_s, yc_s, y_s, bias_s, o_s, dn_s):
    ts = SEQ_TILE
    taps = CONV_WIDTH - 1

    def conv(cb):
        def run():
            ls = slice(128 * cb, 128 * cb + 128)
            bv = jnp.broadcast_to(dwb_ref[:, ls], (BF16_SUBLANES, 128))
            wv = [wb_ref[BF16_SUBLANES * j:BF16_SUBLANES * (j + 1), ls].astype(_F32) for j in range(CONV_WIDTH)]
            for r0 in range(0, T_BLK, N_VACC):
                accs = [None] * N_VACC
                for k in range(r0, r0 + N_VACC + taps):
                    win = u2_s[cb, BF16_SUBLANES * k:BF16_SUBLANES * (k + 1), :].astype(_F32)
                    for n in range(N_VACC):
                        j = k - (r0 + n)
                        if 0 <= j < CONV_WIDTH:
                            term = win * wv[j]
                            accs[n] = term if accs[n] is None else accs[n] + term
                for n in range(N_VACC):
                    r = r0 + n
                    c_s[BF16_SUBLANES * r:BF16_SUBLANES * (r + 1), ls] = accs[n] + bv
        return run

    def ln(r):
        def run():
            y = c_s[r:r + ROW_CHUNK, :]
            mu = jnp.sum(y, axis=-1, keepdims=True) * (1.0 / D_CONV)
            d = y - mu
            var = jnp.sum(d * d, axis=-1, keepdims=True) * (1.0 / D_CONV)
            yn = (d * lax.rsqrt(var + LN_EPS)) * lnw_ref[...] + lnb_ref[...]
            act = _silu(yn)
            for n in range(ROW_CHUNK // SUBLANES):
                row = r + SUBLANES * n
                tau, blk0 = divmod(row, BF16_SUBLANES)
                for cb in range(N_SLABS):
                    yc_s[cb, pl.ds(C_PITCH * blk0 + tau, SUBLANES, stride=C_PITCH), :] = (
                        act[SUBLANES * n:SUBLANES * (n + 1), 128 * cb:128 * cb + 128])
        return run

    def regate(blk):
        def run():
            tr = slice(T_BLK * blk, T_BLK * (blk + 1))
            for cb in range(N_SLABS):
                ls = slice(128 * cb, 128 * cb + 128)
                val = yc_s[cb, C_PITCH * blk:C_PITCH * blk + T_BLK, :] * gc_s[sb, tr, ls]
                y_s[sb, tr, D_ATTN + 128 * cb:D_ATTN + 128 * cb + 128] = val.astype(_BF16)
        return run

    def attn_front(b, g):
        def run():
            slot = (2 * b + g) % ATTN_SLOTS
            rows = slice(BLOCK * b, BLOCK * b + BLOCK)
            krows = slice(BLOCK * b, BLOCK * b + 2 * BLOCK)
            tbl = first_b if b == 0 else 0
            lo_q = lax.broadcasted_iota(jnp.int32, (BLOCK, 128), 1) < HEAD_DIM
            c0 = 256 * g
            qg = jnp.concatenate([q_s[sb, rows, c0:c0 + 128], q_s[sb, rows, c0 + 128:c0 + 256]], axis=0)
            kk = jnp.concatenate([k_s[sb, 2 * g, krows, :], k_s[sb, 2 * g + 1, krows, :]], axis=0)
            s = lax.dot_general(qg, kk, (((1,), (1,)), ((), ())),
                                preferred_element_type=_F32)
            ps = [[None, None], [None, None]]
            ex = [[None, None], [None, None]]
            for e in range(2):
                se = s[:, 2 * BLOCK * e:2 * BLOCK * (e + 1)] + bias_s[tbl, 2 * g + e]
                for j in range(2):
                    sink = sink_of(4 * g + 2 * j + e) * LOG2E
                    sj = se[BLOCK * j:BLOCK * j + BLOCK, :]
                    m = jnp.maximum(jnp.max(sj, axis=-1, keepdims=True), sink)
                    ps[j][e] = jnp.exp2(sj - m).astype(_BF16)
                    ex[j][e] = jnp.exp2(sink - m)
            p = jnp.concatenate([jnp.concatenate(ps[0], axis=1), jnp.concatenate(ps[1], axis=1)], axis=0)
            vv = jnp.concatenate([v_s[sb, 2 * g, krows, :], v_s[sb, 2 * g + 1, krows, :]], axis=0)
            o_s[slot] = jnp.dot(p, vv, preferred_element_type=_F32)
            for j in range(2):
                dn_s[slot, j] = jnp.where(lo_q, ex[j][0], ex[j][1])
        return run

    def attn_norm(b, g):
        def run():
            slot = (2 * b + g) % ATTN_SLOTS
            rows = slice(BLOCK * b, BLOCK * b + BLOCK)
            for j in range(2):
                oj = o_s[slot, BLOCK * j:BLOCK * j + BLOCK, :]
                den = oj[:, 128:256] + dn_s[slot, j]
                cs = slice(256 * g + 128 * j, 256 * g + 128 * j + 128)
                y_s[sb, rows, cs] = ((oj[:, 0:128] / den) * ga_s[sb, rows, cs]).astype(_BF16)
        return run

    pairs = [(b, g) for b in range(ts // BLOCK) for g in range(N_KV_HEADS)]
    return dict(attn_front=[attn_front(b, g) for b, g in pairs],
                attn_norm=[attn_norm(b, g) for b, g in pairs],
                conv=[conv(cb) for cb in range(N_SLABS)],
                ln=[ln(r) for r in range(0, ts, ROW_CHUNK)],
                regate=[regate(blk) for blk in range(N_TBLK)])


def _stream_c(sy, xres_ref, wout_ref, o_ref, y_s):
    def outproj(n):
        def run():
            o_ref[:, n:n + N_CHUNK] = xres_ref[:, n:n + N_CHUNK] + jnp.dot(
                y_s[sy], wout_ref[:, n:n + N_CHUNK], preferred_element_type=_F32)
        return run

    return [outproj(n) for n in range(0, D_MODEL, N_CHUNK)]


def _load_weight_bf16(w_hbm, w_s, stage, sem, side_work=()):
    rows, cols = w_s.shape
    n_chunks = rows // W_ROWS
    n_slots = stage.shape[0]
    per_chunk = -(-len(side_work) // n_chunks)

    def copy(c):
        slot = c % n_slots
        return pltpu.make_async_copy(w_hbm.at[pl.ds(c * W_ROWS, W_ROWS), :],
                                     stage.at[slot, :, pl.ds(0, cols)], sem.at[slot])

    for c in range(min(n_slots - 1, n_chunks)):
        copy(c).start()
    for c in range(n_chunks):
        nxt = c + n_slots - 1
        if nxt < n_chunks:
            copy(nxt).start()
        for item in side_work[c * per_chunk:(c + 1) * per_chunk]:
            item()
        copy(c).wait()
        w_s[c * W_ROWS:(c + 1) * W_ROWS, :] = stage[c % n_slots, :, 0:cols].astype(_BF16)


class _BackgroundWeightLoad:
    def __init__(self, w_hbm, w_s, stage, sem):
        rows, cols = w_s.shape
        per_slot = stage.shape[2] // cols
        self.n_chunks = rows // W_ROWS
        assert self.n_chunks <= per_slot * stage.shape[0]
        self.w_s, self.stage, self.cols, self.per_slot = w_s, stage, cols, per_slot
        self.copies = [
            pltpu.make_async_copy(w_hbm.at[pl.ds(c * W_ROWS, W_ROWS), :],
                                  stage.at[c // per_slot, :, pl.ds((c % per_slot) * cols, cols)], sem.at[c])
            for c in range(self.n_chunks)]

    def start(self):
        for copy in self.copies:
            copy.start()

    def finish(self):
        for c, copy in enumerate(self.copies):
            copy.wait()
            c0 = (c % self.per_slot) * self.cols
            self.w_s[c * W_ROWS:(c + 1) * W_ROWS, :] = (
                self.stage[c // self.per_slot, :, c0:c0 + self.cols].astype(_BF16))


def _layer_kernel(sink_ref, relb_ref, x_ref, xres_ref, nw_ref, win_hbm, qw_ref, kw_ref, bd_ref, dww_ref,
                  dwb_ref, lnw_ref, lnb_ref, wout_hbm, bucket_ref,
                  o_ref,
                  win_ref, wout_ref, wstage_s, wsem, wout_sem, wb_ref,
                  h_s, p1_s, p2_s, p3_s, ssq_s, q_s, k_s, v_s, u_s, u2_s, ga_s, gc_s, c_s, yc_s, y_s,
                  bias_s, o_s, dn_s, *, layer, tiles_per_seq, n_tiles):
    p_s = _RawProj(p1_s, p2_s, p3_s)
    ts = SEQ_TILE
    t = pl.program_id(0)
    sa = lax.rem(t, 2)
    sb = 1 - sa
    first_a = lax.rem(t, tiles_per_seq) == 0
    first_b = jnp.where(lax.rem(t + tiles_per_seq - 1, tiles_per_seq) == 0, 1, 0)

    A = _stream_a(sa, x_ref, xres_ref, nw_ref, win_ref, qw_ref, kw_ref, bd_ref,
                  h_s, p_s, ssq_s, q_s, k_s, v_s, u_s, u2_s, ga_s, gc_s)
    wout_load = _BackgroundWeightLoad(wout_hbm.at[layer], wout_ref, wstage_s, wout_sem)

    @pl.when(t == 0)
    def _():
        _load_weight_bf16(win_hbm.at[layer], win_ref, wstage_s, wsem,
                          side_work=_bias_table_items(relb_ref, bucket_ref, bias_s) + A["rms_first"])
        wout_load.start()
        for j in range(CONV_WIDTH):
            wb_ref[BF16_SUBLANES * j:BF16_SUBLANES * (j + 1), :] = jnp.broadcast_to(
                dww_ref[j:j + 1, :], (BF16_SUBLANES, D_CONV)).astype(_BF16)
        y_s[1] = jnp.zeros(y_s.shape[1:], _BF16)

    @pl.when(first_a)
    def _():
        k_s[sa, :, 0:KV_HALO, :] = jnp.zeros((4, KV_HALO, 128), _BF16)
        v_s[sa, :, 0:KV_HALO, :] = jnp.zeros((4, KV_HALO, 256), _BF16)
        for cb in range(N_SLABS):
            u_s[sa * N_SLABS + cb, 0:T_BLK, :] = jnp.zeros((T_BLK, 128), _F32)

    @pl.when(jnp.logical_not(first_a))
    def _():
        k_s[sa, :, 0:KV_HALO, :] = k_s[sb, :, ts:ts + KV_HALO, :]
        v_s[sa, :, 0:KV_HALO, :] = v_s[sb, :, ts:ts + KV_HALO, :]
        for cb in range(N_SLABS):
            u_s[sa * N_SLABS + cb, 0:T_BLK, :] = u_s[sb * N_SLABS + cb,
                                                     N_TBLK * U_PITCH:N_TBLK * U_PITCH + T_BLK, :]

    Bm = _stream_b(sb, first_b, functools.partial(_sink, sink_ref, layer), wb_ref, dwb_ref, lnw_ref, lnb_ref,
                   q_s, k_s, v_s, u2_s, ga_s, gc_s, c_s, yc_s, y_s, bias_s, o_s, dn_s)
    C = _stream_c(sa, xres_ref, wout_ref, o_ref, y_s)
    af, an = Bm["attn_front"], Bm["attn_norm"]

    conv_and_out = [item for pair in zip(C, Bm["conv"]) for item in pair]
    conv_tail = A["gate_c_prev"] + Bm["ln"] + Bm["regate"]
    in_proj = ([A["dot_qkvz"]] + A["q_sumsq"] + [A["k_sumsq"]] + A["gate_a"] + A["q_norm"]
               + [A["kv_store"]] + A["dot_ag"] + A["glu"] + A["pack"] + [A["dot_zc"]] + A["rms_next"])
    regions = [conv_and_out, af + an, conv_tail + in_proj]
    fill_regions = [in_proj]
    drain_regions = [af + an, conv_and_out, conv_tail]
    last_regions = [C]
    schedule = [(regions, (t >= 1) & (t < n_tiles)),
                (fill_regions, t == 0),
                (drain_regions, t == n_tiles),
                (last_regions, t == n_tiles + 1)]
    for group, active in schedule:
        trips = jnp.where(active, 1, 0)
        for region in group:
            def body(_, carry, region=region):
                for item in region:
                    item()
                return carry
            lax.fori_loop(0, trips, body, 0)

    @pl.when(t == 0)
    def _():
        wout_load.finish()


def _layer(x, layer, norm_w, w_in, q_norm_w, k_norm_w, sinks, dw_w, dw_b, ln_w, ln_b, w_out, rel_bias):
    B, S, D = x.shape
    ts = SEQ_TILE
    assert D == D_MODEL and S % ts == 0 and ts % BLOCK == 0
    n_tiles = B * S // ts

    idx = np.arange(256) // HEAD_DIM
    bd = jnp.asarray((idx[:, None] == idx[None, :]).astype(np.float32), dtype=_BF16)
    qw = (jnp.tile(q_norm_w[layer].astype(_F32), N_HEADS) * (HEAD_DIM ** -0.5 * LOG2E)).reshape(1, D_ATTN)
    kw = jnp.tile(k_norm_w[layer].astype(_F32), N_KV_HEADS).reshape(1, 128)
    xt = x.reshape(n_tiles, ts, D)

    const2 = lambda t: (0, 0)
    layer_row = lambda t: (layer, 0)
    in_specs = [
        pl.BlockSpec(memory_space=pltpu.SMEM),
        pl.BlockSpec(memory_space=pltpu.SMEM),
        pl.BlockSpec((None, ts, D), lambda t: (jnp.minimum(t + 1, n_tiles - 1), 0, 0)),
        pl.BlockSpec((None, ts, D), lambda t: (jnp.maximum(t - 2, 0), 0, 0)),
        pl.BlockSpec((1, D), layer_row),
        pl.BlockSpec(memory_space=pl.ANY),
        pl.BlockSpec((1, D_ATTN), const2),
        pl.BlockSpec((1, 128), const2),
        pl.BlockSpec((256, 256), const2),
        pl.BlockSpec((None, CONV_WIDTH, D_CONV), lambda t: (layer, 0, 0)),
        pl.BlockSpec((1, D_CONV), layer_row),
        pl.BlockSpec((1, D_CONV), layer_row),
        pl.BlockSpec((1, D_CONV), layer_row),
        pl.BlockSpec(memory_space=pl.ANY),
        pl.BlockSpec((BLOCK, 2 * BLOCK), const2),
    ]
    scratch = [
        pltpu.VMEM((D, D_IN), _BF16),
        pltpu.VMEM((D, D), _BF16),
        pltpu.VMEM((W_SLOTS, W_ROWS, D_IN), _F32),
        pltpu.SemaphoreType.DMA((W_SLOTS,)),
        pltpu.SemaphoreType.DMA((D // W_ROWS,)),
        pltpu.VMEM((CONV_WIDTH * BF16_SUBLANES, D_CONV), _BF16),
        pltpu.VMEM((ts, D), _BF16),
        pltpu.VMEM((ts, UA0 - Q0), _F32),
        pltpu.VMEM((ts, ZC0 - UA0), _F32),
        pltpu.VMEM((ts, D_IN - ZC0), _F32),
        pltpu.VMEM((ts, D_ATTN + 128), _F32),
        pltpu.VMEM((2, ts, D_ATTN), _BF16),
        pltpu.VMEM((2, 4, ts + KV_HALO, 128), _BF16),
        pltpu.VMEM((2, 4, ts + KV_HALO, 256), _BF16),
        pltpu.VMEM((2 * N_SLABS, U_ROWS, 128), _F32),
        pltpu.VMEM((N_SLABS, N_UTILES * BF16_SUBLANES, 128), _BF16),
        pltpu.VMEM((2, ts, D_ATTN), _F32),
        pltpu.VMEM((2, ts, D_CONV), _F32),
        pltpu.VMEM((ts, D_CONV), _F32),
        pltpu.VMEM((N_SLABS, C_ROWS, 128), _F32),
        pltpu.VMEM((2, ts, D), _BF16),
        pltpu.VMEM((2, 4, 2 * BLOCK, 2 * BLOCK), _F32),
        pltpu.VMEM((ATTN_SLOTS, 2 * BLOCK, 2 * BLOCK), _F32),
        pltpu.VMEM((ATTN_SLOTS, 2, BLOCK, 128), _F32),
    ]
    out = pl.pallas_call(
        functools.partial(_layer_kernel, layer=layer, tiles_per_seq=S // ts, n_tiles=n_tiles),
        grid=(n_tiles + 2,),
        in_specs=in_specs,
        out_specs=pl.BlockSpec((None, ts, D), lambda t: (jnp.maximum(t - 2, 0), 0, 0)),
        out_shape=jax.ShapeDtypeStruct((n_tiles, ts, D), x.dtype),
        scratch_shapes=scratch,
        compiler_params=pltpu.CompilerParams(
            dimension_semantics=("arbitrary",),
            vmem_limit_bytes=VMEM_LIMIT_BYTES),
        name="hybrid_layer",
    )(sinks.astype(_F32), rel_bias.astype(_F32), xt, xt,
      norm_w.astype(_F32), w_in.astype(_F32), qw, kw, bd,
      dw_w.astype(_F32), dw_b.astype(_F32), ln_w.astype(_F32),
      ln_b.astype(_F32), w_out.astype(_F32), jnp.asarray(_banded_buckets()))
    return out.reshape(B, S, D)


def kernel(x, norm_w, w_in, q_norm_w, k_norm_w, sinks, dw_w, dw_b, ln_w, ln_b, w_out, rel_bias):
    for layer in range(norm_w.shape[0]):
        x = _layer(x, layer, norm_w, w_in, q_norm_w, k_norm_w, sinks, dw_w, dw_b, ln_w, ln_b, w_out, rel_bias)
    return x
```
